```python
import math
import jax, jax.numpy as jnp
from jax import lax
import numpy as np

D_MODEL = 1024
BATCH = 16
SEQ = 2048
DEPTH = 2

CHUNK = 64
N_MIXERS = 2
N_S5_LAYERS = (DEPTH + 1) // 2
N_SGU_LAYERS = DEPTH // 2
BRANCH = D_MODEL
S5_GROUP = 16
S5_GROUPS = BRANCH // S5_GROUP
S5_STATE = 64
SGU_BLOCK = 128
SGU_HEADS = 8
SGU_HEAD_DIM = BRANCH // SGU_HEADS
RMS_EPS = 1e-6
LN_EPS = 1e-5
DT_MIN = 1e-3
DT_MAX = 1e-1
RE_CLIP = -1e-4

kernel_name = "hybrid_s5_gmlp_streaming_trunk"


def rmsnorm(x, g):
    xf = x.astype(jnp.float32)
    y = xf * lax.rsqrt(jnp.mean(xf * xf, axis=-1, keepdims=True) + RMS_EPS)
    return (y * g.astype(jnp.float32)).astype(x.dtype)


def layernorm(x, g, b):
    xf = x.astype(jnp.float32)
    mu = jnp.mean(xf, axis=-1, keepdims=True)
    var = jnp.mean(jnp.square(xf - mu), axis=-1, keepdims=True)
    y = (xf - mu) * lax.rsqrt(var + LN_EPS)
    return (y * g.astype(jnp.float32) + b.astype(jnp.float32)).astype(x.dtype)


def _linear_recurrence(left, right):
    a_l, b_l = left
    a_r, b_r = right
    return a_r * a_l, a_r * b_l + b_r


def s5_mixer(u, A_re, A_im, log_dt, B_re, B_im, C_re, C_im, D):
    bsz, seq, _ = u.shape
    uf = u.astype(jnp.float32)
    lam = lax.complex(jnp.minimum(A_re.astype(jnp.float32), RE_CLIP), A_im.astype(jnp.float32))
    dt = jnp.exp(log_dt.astype(jnp.float32))[:, None]
    lam_bar = jnp.exp(lam * dt)
    b_cplx = lax.complex(B_re.astype(jnp.float32), B_im.astype(jnp.float32))
    b_bar = ((lam_bar - 1.0) / lam)[..., None] * b_cplx
    c_cplx = lax.complex(C_re.astype(jnp.float32), C_im.astype(jnp.float32))
    steps = jnp.arange(1, CHUNK + 1, dtype=jnp.float32)[:, None, None]
    lam_pow = jnp.exp(lam[None] * dt[None] * steps)
    a_elems = jnp.broadcast_to(lam_bar[None, None], (CHUNK, bsz, S5_GROUPS, S5_STATE))
    n_chunks = seq // CHUNK
    u_chunks = uf.reshape(bsz, n_chunks, CHUNK, S5_GROUPS, S5_GROUP).transpose(1, 2, 0, 3, 4)

    def step(h_prev, u_c):
        bu = jnp.einsum('lbgh,gph->lbgp', u_c, b_bar)
        _, h = lax.associative_scan(_linear_recurrence, (a_elems, bu), axis=0)
        h = h + lam_pow[:, None] * h_prev[None]
        y = jnp.real(jnp.einsum('lbgp,ghp->lbgh', h, c_cplx))
        return h[-1], y

    h0 = jnp.zeros((bsz, S5_GROUPS, S5_STATE), dtype=jnp.complex64)
    _, ys = lax.scan(step, h0, u_chunks)
    y = ys.transpose(2, 0, 1, 3, 4).reshape(bsz, seq, BRANCH)
    return y + D.astype(jnp.float32) * uf


def sgu_mixer(u, v, ln_g, ln_b, w_s, b_s):
    bsz, seq, _ = v.shape
    vn = layernorm(v, ln_g, ln_b)
    vb = vn.reshape(bsz, seq // SGU_BLOCK, SGU_BLOCK, SGU_HEADS, SGU_HEAD_DIM)
    mask = jnp.tril(jnp.ones((SGU_BLOCK, SGU_BLOCK), dtype=bool))
    w = jnp.where(mask[None], w_s, jnp.zeros_like(w_s))
    mixed = jnp.einsum('hts,bnshc->bnthc', w, vb) + jnp.transpose(b_s)[:, :, None]
    return u * mixed.reshape(bsz, seq, BRANCH)


def setup_inputs(seed: int = 0) -> dict:
    key = jax.random.key(seed)
    ks = jax.random.split(key, 24)
    f32 = jnp.float32
    nA, nB, G, P, H = N_S5_LAYERS, N_SGU_LAYERS, S5_GROUPS, S5_STATE, S5_GROUP
    x = jax.random.normal(ks[0], (BATCH, SEQ, D_MODEL), f32)
    norm_g = 1.0 + 0.02 * jax.random.normal(ks[1], (DEPTH, D_MODEL), f32)
    final_g = 1.0 + 0.02 * jax.random.normal(ks[2], (D_MODEL,), f32)
    s5_w_in = jax.random.normal(ks[3], (nA, D_MODEL, 2 * BRANCH), f32) * D_MODEL ** -0.5
    n_idx = jnp.arange(P, dtype=f32)
    s5_A_re = -0.5 + 0.01 * jax.random.normal(ks[4], (nA, G, P), f32)
    s5_A_im = math.pi * n_idx + 0.01 * jax.random.normal(ks[5], (nA, G, P), f32)
    s5_log_dt = jax.random.uniform(ks[6], (nA, G), f32, math.log(DT_MIN), math.log(DT_MAX))
    s5_B_re = jax.random.normal(ks[7], (nA, G, P, H), f32) * (2.0 * H) ** -0.5
    s5_B_im = jax.random.normal(ks[8], (nA, G, P, H), f32) * (2.0 * H) ** -0.5
    s5_C_re = jax.random.normal(ks[9], (nA, G, H, P), f32) * (2.0 * P) ** -0.5
    s5_C_im = jax.random.normal(ks[10], (nA, G, H, P), f32) * (2.0 * P) ** -0.5
    s5_D = jax.random.normal(ks[11], (nA, BRANCH), f32)
    s5_w_glu = jax.random.normal(ks[12], (nA, BRANCH, BRANCH), f32) * BRANCH ** -0.5
    s5_b_glu = 0.01 * jax.random.normal(ks[13], (nA, BRANCH), f32)
    s5_w_out = jax.random.normal(ks[14], (nA, BRANCH, D_MODEL), f32) * BRANCH ** -0.5
    sgu_w_in = jax.random.normal(ks[15], (nB, D_MODEL, 3 * BRANCH), f32) * D_MODEL ** -0.5
    sgu_ln_g = 1.0 + 0.02 * jax.random.normal(ks[16], (nB, BRANCH), f32)
    sgu_ln_b = 0.01 * jax.random.normal(ks[17], (nB, BRANCH), f32)
    sgu_w_s = jax.random.normal(ks[18], (nB, SGU_HEADS, SGU_BLOCK, SGU_BLOCK), f32) * SGU_BLOCK ** -0.5
    sgu_b_s = 1.0 + 0.01 * jax.random.normal(ks[19], (nB, SGU_HEADS, SGU_BLOCK), f32)
    sgu_w_out = jax.random.normal(ks[20], (nB, BRANCH, D_MODEL), f32) * BRANCH ** -0.5
    return {"x": x, "norm_g": norm_g, "final_g": final_g,
            "s5_w_in": s5_w_in, "s5_A_re": s5_A_re, "s5_A_im": s5_A_im, "s5_log_dt": s5_log_dt,
            "s5_B_re": s5_B_re, "s5_B_im": s5_B_im, "s5_C_re": s5_C_re, "s5_C_im": s5_C_im,
            "s5_D": s5_D, "s5_w_glu": s5_w_glu, "s5_b_glu": s5_b_glu, "s5_w_out": s5_w_out,
            "sgu_w_in": sgu_w_in, "sgu_ln_g": sgu_ln_g, "sgu_ln_b": sgu_ln_b,
            "sgu_w_s": sgu_w_s, "sgu_b_s": sgu_b_s, "sgu_w_out": sgu_w_out}


def reference(x, norm_g, final_g,
              s5_w_in, s5_A_re, s5_A_im, s5_log_dt, s5_B_re, s5_B_im, s5_C_re, s5_C_im,
              s5_D, s5_w_glu, s5_b_glu, s5_w_out,
              sgu_w_in, sgu_ln_g, sgu_ln_b, sgu_w_s, sgu_b_s, sgu_w_out):
    for i in range(DEPTH):
        h = rmsnorm(x, norm_g[i])
        j = i // N_MIXERS
        if i % N_MIXERS == 0:
            proj = h @ s5_w_in[j]
            branch, gate = jnp.split(proj, 2, axis=-1)
            y = s5_mixer(branch, s5_A_re[j], s5_A_im[j], s5_log_dt[j], s5_B_re[j], s5_B_im[j],
                         s5_C_re[j], s5_C_im[j], s5_D[j]).astype(x.dtype)
            y = jax.nn.gelu(y)
            y = y * jax.nn.sigmoid(y @ s5_w_glu[j] + s5_b_glu[j])
            out = (y * jax.nn.silu(gate)) @ s5_w_out[j]
        else:
            proj = h @ sgu_w_in[j]
            u, v, gate = jnp.split(proj, 3, axis=-1)
            y = sgu_mixer(jax.nn.gelu(u), jax.nn.gelu(v), sgu_ln_g[j], sgu_ln_b[j],
                          sgu_w_s[j], sgu_b_s[j])
            out = (y * jax.nn.silu(gate)) @ sgu_w_out[j]
        x = x + out
    return rmsnorm(x, final_g)
```

```python
import functools
import math

import numpy as np
import jax
import jax.numpy as jnp
from jax import lax
from jax.experimental import pallas as pl
from jax.experimental.pallas import tpu as pltpu

D_MODEL = 1024
BATCH = 16
SEQ = 2048
BRANCH = D_MODEL
S5_GROUP = 16
S5_GROUPS = BRANCH // S5_GROUP
S5_STATE = 64
SGU_BLOCK = 128
SGU_HEADS = 8
SGU_HEAD_DIM = BRANCH // SGU_HEADS
RMS_EPS = 1e-6
LN_EPS = 1e-5
RE_CLIP = -1e-4

LANES = 128
TCH = 16
NCHUNK = SEQ // TCH
SLOTS = LANES // S5_GROUP
NJ = BRANCH // LANES
GW = TCH * S5_GROUP
PAIRS_PER_J = SLOTS // 2
VMEM_LIMIT = 56 * 1024 * 1024

_HI = lax.Precision.HIGHEST


def _gelu(x):
    return jax.nn.gelu(x)


def _sigmoid(x):
    return jax.nn.sigmoid(x)


def _slot_frame_table():
    a = np.arange(SLOTS)[:, None]
    x = np.arange(TCH)[None, :]
    half, k = x // SLOTS, x % SLOTS
    return SLOTS * half + (k - a) % SLOTS


_FRAME_OF_SLOT = _slot_frame_table()


def _s5_operators(A_re, A_im, log_dt, B_re, B_im, C_re, C_im, dtype=jnp.bfloat16):
    f32 = jnp.float32
    G, P, H = S5_GROUPS, S5_STATE, S5_GROUP
    lam_re = jnp.minimum(A_re.astype(f32), RE_CLIP)
    lam_im = A_im.astype(f32)
    dt = jnp.exp(log_dt.astype(f32))[:, None]
    k = jnp.arange(TCH + 1, dtype=f32)[:, None, None]
    mag = jnp.exp(k * (lam_re * dt)[None])
    ang = k * (lam_im * dt)[None]
    pw_re, pw_im = mag * jnp.cos(ang), mag * jnp.sin(ang)
    nr, ni = pw_re[1] - 1.0, pw_im[1]
    den = lam_re * lam_re + lam_im * lam_im
    cf_re = (nr * lam_re + ni * lam_im) / den
    cf_im = (ni * lam_re - nr * lam_im) / den
    bb_re = cf_re[..., None] * B_re - cf_im[..., None] * B_im
    bb_im = cf_re[..., None] * B_im + cf_im[..., None] * B_re
    c_re = jnp.swapaxes(C_re.astype(f32), 1, 2)
    c_im = jnp.swapaxes(C_im.astype(f32), 1, 2)

    lb_re = pw_re[..., None] * bb_re[None] - pw_im[..., None] * bb_im[None]
    lb_im = pw_re[..., None] * bb_im[None] + pw_im[..., None] * bb_re[None]
    taps = (jnp.einsum('kgph,gpc->gkhc', lb_re[:TCH], c_re, precision=_HI)
            - jnp.einsum('kgph,gpc->gkhc', lb_im[:TCH], c_im, precision=_HI))

    frame = jnp.asarray(_FRAME_OF_SLOT[np.arange(G) % SLOTS])
    tau = frame[:, None, :] - frame[:, :, None]
    gi = jnp.arange(G)[:, None, None]
    tz = taps[gi, jnp.clip(tau, 0, TCH - 1)]
    tz = jnp.where((tau >= 0)[..., None, None], tz, 0.0)
    tz = tz.transpose(0, 1, 3, 2, 4).reshape(G, GW, GW)

    kin = (TCH - 1) - frame
    gi2 = jnp.arange(G)[:, None]
    in_re = lb_re.transpose(1, 0, 2, 3)[gi2, kin]
    in_im = lb_im.transpose(1, 0, 2, 3)[gi2, kin]
    in_re = in_re.transpose(0, 1, 3, 2).reshape(G, GW, P)
    in_im = in_im.transpose(0, 1, 3, 2).reshape(G, GW, P)
    odd = (jnp.arange(G) % 2 == 1)[:, None, None]
    z = jnp.zeros_like(in_re)
    mb = jnp.concatenate([jnp.where(odd, z, in_re), jnp.where(odd, in_re, z),
                          jnp.where(odd, z, in_im), jnp.where(odd, in_im, z)], axis=-1)

    kout = frame + 1
    po_re = pw_re.transpose(1, 0, 2)[gi2, kout]
    po_im = pw_im.transpose(1, 0, 2)[gi2, kout]
    o_re = po_re[..., None] * c_re[:, None] - po_im[..., None] * c_im[:, None]
    o_im = po_re[..., None] * c_im[:, None] + po_im[..., None] * c_re[:, None]
    o_re = o_re.transpose(0, 2, 1, 3).reshape(G, P, GW)
    o_im = (-o_im).transpose(0, 2, 1, 3).reshape(G, P, GW)
    z = jnp.zeros_like(o_re)
    mc = jnp.concatenate([jnp.where(odd, z, o_re), jnp.where(odd, o_re, z),
                          jnp.where(odd, z, o_im), jnp.where(odd, o_im, z)], axis=1)

    dec_re = pw_re[TCH].reshape(G // 2, 1, 2 * P)
    dec_im = pw_im[TCH].reshape(G // 2, 1, 2 * P)
    return tz.astype(dtype), mb.astype(dtype), mc.astype(dtype), dec_re, dec_im


def _chunk_perm():
    p = np.zeros((BATCH * TCH, BATCH * TCH), np.float32)
    for b in range(BATCH):
        for s in range(TCH):
            p[s * BATCH + b, b * TCH + s] = 1.0
    return p


_PERM = _chunk_perm()

_BF = jnp.bfloat16
_F32 = jnp.float32


def _dot(a, b):
    return jnp.dot(a, b, preferred_element_type=_F32)


def _rms(x, g):
    return x * lax.rsqrt(jnp.mean(x * x, axis=-1, keepdims=True) + RMS_EPS) * g


IN_CT = 4


def _s5_in_kernel(x_ref, g_ref, perm_ref, w_ref, u_ref, gate_ref, hp_ref):
    rows = BATCH * TCH
    for c in range(IN_CT):
        xc = x_ref[:, c * TCH:(c + 1) * TCH, :].reshape(rows, D_MODEL)
        hc = _rms(xc, g_ref[...]).astype(_BF)
        hp_ref[c * rows:(c + 1) * rows, :] = _dot(perm_ref[...], hc).astype(_BF)
    hp = hp_ref[...]
    u = _dot(hp, w_ref[:, :BRANCH])
    u_ref[...] = u.reshape(IN_CT, TCH, BATCH, BRANCH)
    gate = _dot(hp, w_ref[:, BRANCH:])
    gate_ref[...] = gate.astype(_BF).reshape(IN_CT, TCH, BATCH, BRANCH)


def _s5_in(x, g, w_in):
    grid = (NCHUNK // IN_CT,)
    blk4 = (IN_CT, TCH, BATCH, BRANCH)
    return pl.pallas_call(
        _s5_in_kernel,
        grid=grid,
        in_specs=[
            pl.BlockSpec((BATCH, IN_CT * TCH, D_MODEL), lambda i: (0, i, 0)),
            pl.BlockSpec((1, D_MODEL), lambda i: (0, 0)),
            pl.BlockSpec((BATCH * TCH, BATCH * TCH), lambda i: (0, 0)),
            pl.BlockSpec((D_MODEL, 2 * BRANCH), lambda i: (0, 0)),
        ],
        out_specs=[
            pl.BlockSpec(blk4, lambda i: (i, 0, 0, 0)),
            pl.BlockSpec(blk4, lambda i: (i, 0, 0, 0)),
        ],
        out_shape=[
            jax.ShapeDtypeStruct((NCHUNK, TCH, BATCH, BRANCH), _F32),
            jax.ShapeDtypeStruct((NCHUNK, TCH, BATCH, BRANCH), _BF),
        ],
        scratch_shapes=[pltpu.VMEM((IN_CT * BATCH * TCH, D_MODEL), _BF)],
        compiler_params=pltpu.CompilerParams(
            dimension_semantics=("parallel",), vmem_limit_bytes=VMEM_LIMIT),
        name="s5_in",
    )(x, g, jnp.asarray(_PERM, _BF), w_in)


MIX_CT = 32
MIX_R = MIX_CT * BATCH
MIX_CB = 4
MIX_RT = MIX_CB * BATCH


def _s5_mix_kernel(u_ref, tz_ref, mb_ref, mc_ref, dre_ref, dim_ref, d_ref, y_ref,
                   xs_ref, ys_ref, ss_ref, hs_ref, carry_ref):
    ct = pl.program_id(1)

    @pl.when(ct == 0)
    def _():
        carry_ref[...] = jnp.zeros_like(carry_ref)

    slot = lax.broadcasted_iota(jnp.int32, (MIX_RT, LANES), 1) // S5_GROUP

    def relayout_in(rt, carry):
        c0 = pl.multiple_of(rt * MIX_CB, MIX_CB)
        r0 = pl.multiple_of(rt * MIX_RT, MIX_RT)
        rolled = []
        for s in range(TCH):
            p = u_ref[pl.ds(c0, MIX_CB), s, :, :].reshape(MIX_RT, LANES)
            r = s % SLOTS
            rolled.append(p if r == 0 else pltpu.roll(p, r * S5_GROUP, axis=1))
        for a in range(SLOTS):
            halves = []
            for half in range(2):
                acc = rolled[SLOTS * half]
                for sp in range(1, SLOTS):
                    acc = jnp.where(slot == (a + sp) % SLOTS, rolled[SLOTS * half + sp], acc)
                halves.append(acc)
            xs_ref[a, pl.ds(r0, MIX_RT), :] = jnp.concatenate(halves, axis=1).astype(_BF)
        return carry

    lax.fori_loop(0, MIX_CT // MIX_CB, relayout_in, 0)

    for q in range(PAIRS_PER_J):
        a0, a1 = 2 * q, 2 * q + 1
        x0 = xs_ref[a0]
        x1 = xs_ref[a1]
        ys_ref[a0] = _dot(x0, tz_ref[a0])
        ys_ref[a1] = _dot(x1, tz_ref[a1])
        ss_ref[q] = _dot(x0, mb_ref[a0]) + _dot(x1, mb_ref[a1])

    dre = [dre_ref[q] for q in range(PAIRS_PER_J)]
    dim_ = [dim_ref[q] for q in range(PAIRS_PER_J)]

    def chunk_step(c, state):
        r0 = pl.multiple_of(c * BATCH, BATCH)
        new = []
        for q in range(PAIRS_PER_J):
            hre, him = state[2 * q], state[2 * q + 1]
            hs_ref[q, pl.ds(r0, BATCH), 0:LANES] = hre.astype(_BF)
            hs_ref[q, pl.ds(r0, BATCH), LANES:2 * LANES] = him.astype(_BF)
            sre = ss_ref[q, pl.ds(r0, BATCH), 0:LANES]
            sim = ss_ref[q, pl.ds(r0, BATCH), LANES:2 * LANES]
            new.append(dre[q] * hre - dim_[q] * him + sre)
            new.append(dre[q] * him + dim_[q] * hre + sim)
        return tuple(new)

    init = tuple(carry_ref[q, k] for q in range(PAIRS_PER_J) for k in range(2))
    final = lax.fori_loop(0, MIX_CT, chunk_step, init)
    for q in range(PAIRS_PER_J):
        carry_ref[q, 0] = final[2 * q]
        carry_ref[q, 1] = final[2 * q + 1]

    for a in range(SLOTS):
        ys_ref[a] = ys_ref[a] + _dot(hs_ref[a // 2], mc_ref[a])

    dvec = d_ref[...]

    def relayout_out(rt, carry):
        c0 = pl.multiple_of(rt * MIX_CB, MIX_CB)
        r0 = pl.multiple_of(rt * MIX_RT, MIX_RT)
        for t in range(TCH):
            half, r = t // SLOTS, t % SLOTS
            lo = half * LANES
            acc = ys_ref[0, pl.ds(r0, MIX_RT), lo:lo + LANES]
            for a in range(1, SLOTS):
                acc = jnp.where(slot == (a + r) % SLOTS, ys_ref[a, pl.ds(r0, MIX_RT), lo:lo + LANES], acc)
            if r:
                acc = pltpu.roll(acc, (SLOTS - r) * S5_GROUP, axis=1)
            u_t = u_ref[pl.ds(c0, MIX_CB), t, :, :].reshape(MIX_RT, LANES)
            y_ref[pl.ds(c0, MIX_CB), t, :, :] = (acc + dvec * u_t).reshape(MIX_CB, BATCH, LANES)
        return carry

    lax.fori_loop(0, MIX_CT // MIX_CB, relayout_out, 0)


def _s5_mix(u, tz, mb, mc, dec_re, dec_im, d):
    grid = (NJ, NCHUNK // MIX_CT)
    blk = (MIX_CT, TCH, BATCH, LANES)
    wspec = pl.BlockSpec((SLOTS, GW, GW), lambda j, c: (j, 0, 0))
    dspec = pl.BlockSpec((PAIRS_PER_J, 1, LANES), lambda j, c: (j, 0, 0))
    return pl.pallas_call(
        _s5_mix_kernel,
        grid=grid,
        in_specs=[
            pl.BlockSpec(blk, lambda j, c: (c, 0, 0, j)),
            wspec, wspec, wspec, dspec, dspec,
            pl.BlockSpec((1, LANES), lambda j, c: (0, j)),
        ],
        out_specs=pl.BlockSpec(blk, lambda j, c: (c, 0, 0, j)),
        out_shape=jax.ShapeDtypeStruct((NCHUNK, TCH, BATCH, BRANCH), _F32),
        scratch_shapes=[
            pltpu.VMEM((SLOTS, MIX_R, GW), _BF),
            pltpu.VMEM((SLOTS, MIX_R, GW), _F32),
            pltpu.VMEM((PAIRS_PER_J, MIX_R, GW), _F32),
            pltpu.VMEM((PAIRS_PER_J, MIX_R, GW), _BF),
            pltpu.VMEM((PAIRS_PER_J, 2, BATCH, LANES), _F32),
        ],
        compiler_params=pltpu.CompilerParams(
            dimension_semantics=("parallel", "arbitrary"), vmem_limit_bytes=VMEM_LIMIT),
        name="s5_mix",
    )(u, tz, mb, mc, dec_re, dec_im, d)


def _s5_out_kernel(y_ref, gate_ref, x_ref, permt_ref, wglu_ref, bglu_ref, wout_ref, o_ref, zn_ref):
    rows = BATCH * TCH
    n = IN_CT * rows
    y = _gelu(y_ref[...].reshape(n, BRANCH))
    glu = _dot(y.astype(_BF), wglu_ref[...]) + bglu_ref[...]
    y = y * _sigmoid(glu)
    gate = gate_ref[...].reshape(n, BRANCH).astype(_F32)
    z = (y * (gate * _sigmoid(gate))).astype(_BF)
    for c in range(IN_CT):
        zn_ref[c * rows:(c + 1) * rows, :] = _dot(permt_ref[...], z[c * rows:(c + 1) * rows, :]).astype(_BF)
    out = _dot(zn_ref[...], wout_ref[...])
    for c in range(IN_CT):
        oc = out[c * rows:(c + 1) * rows, :].reshape(BATCH, TCH, D_MODEL)
        o_ref[:, c * TCH:(c + 1) * TCH, :] = x_ref[:, c * TCH:(c + 1) * TCH, :] + oc


def _s5_out(y, gate, x, w_glu, b_glu, w_out):
    grid = (NCHUNK // IN_CT,)
    blk4 = (IN_CT, TCH, BATCH, BRANCH)
    xblk = pl.BlockSpec((BATCH, IN_CT * TCH, D_MODEL), lambda i: (0, i, 0))
    return pl.pallas_call(
        _s5_out_kernel,
        grid=grid,
        in_specs=[
            pl.BlockSpec(blk4, lambda i: (i, 0, 0, 0)),
            pl.BlockSpec(blk4, lambda i: (i, 0, 0, 0)),
            xblk,
            pl.BlockSpec((BATCH * TCH, BATCH * TCH), lambda i: (0, 0)),
            pl.BlockSpec((BRANCH, BRANCH), lambda i: (0, 0)),
            pl.BlockSpec((1, BRANCH), lambda i: (0, 0)),
            pl.BlockSpec((BRANCH, D_MODEL), lambda i: (0, 0)),
        ],
        out_specs=xblk,
        out_shape=jax.ShapeDtypeStruct((BATCH, SEQ, D_MODEL), _F32),
        scratch_shapes=[pltpu.VMEM((IN_CT * BATCH * TCH, BRANCH), _BF)],
        compiler_params=pltpu.CompilerParams(
            dimension_semantics=("parallel",), vmem_limit_bytes=VMEM_LIMIT),
        name="s5_out",
    )(y, gate, x, jnp.asarray(_PERM.T, _BF), w_glu, b_glu, w_out)


SGU_TB = 512
SGU_NB = SGU_TB // SGU_BLOCK


def _sgu_kernel(x_ref, g_ref, win_ref, lng_ref, lnb_ref, ws_ref, bs_ref, wout_ref, fg_ref, o_ref, mix_ref):
    x = x_ref[...]
    h = _rms(x, g_ref[...]).astype(_BF)
    v = _gelu(_dot(h, win_ref[:, BRANCH:2 * BRANCH]))
    mu = jnp.mean(v, axis=-1, keepdims=True)
    vc = v - mu
    var = jnp.mean(vc * vc, axis=-1, keepdims=True)
    vn = (vc * lax.rsqrt(var + LN_EPS) * lng_ref[...] + lnb_ref[...]).astype(_BF)
    for hd in range(SGU_HEADS):
        lo = hd * SGU_HEAD_DIM
        rhs = jnp.concatenate(
            [vn[b * SGU_BLOCK:(b + 1) * SGU_BLOCK, lo:lo + SGU_HEAD_DIM] for b in range(SGU_NB)], axis=1)
        res = _dot(ws_ref[hd], rhs)
        for b in range(SGU_NB):
            mix_ref[b * SGU_BLOCK:(b + 1) * SGU_BLOCK, lo:lo + SGU_HEAD_DIM] = (
                res[:, b * SGU_HEAD_DIM:(b + 1) * SGU_HEAD_DIM] + bs_ref[:, lo:lo + SGU_HEAD_DIM])
    u = _gelu(_dot(h, win_ref[:, :BRANCH]))
    y = u * mix_ref[...]
    gate = _dot(h, win_ref[:, 2 * BRANCH:])
    z = (y * (gate * _sigmoid(gate))).astype(_BF)
    x2 = x + _dot(z, wout_ref[...])
    o_ref[...] = _rms(x2, fg_ref[...])


def _sgu_layer(x, g, w_in, ln_g, ln_b, ws, bs, w_out, final_g):
    n = BATCH * SEQ
    grid = (n // SGU_TB,)
    row = lambda i: (0, 0)
    xblk = pl.BlockSpec((SGU_TB, D_MODEL), lambda i: (i, 0))
    return pl.pallas_call(
        _sgu_kernel,
        grid=grid,
        in_specs=[
            xblk,
            pl.BlockSpec((1, D_MODEL), row),
            pl.BlockSpec((D_MODEL, 3 * BRANCH), row),
            pl.BlockSpec((1, BRANCH), row),
            pl.BlockSpec((1, BRANCH), row),
            pl.BlockSpec((SGU_HEADS, SGU_BLOCK, SGU_BLOCK), lambda i: (0, 0, 0)),
            pl.BlockSpec((SGU_BLOCK, BRANCH), row),
            pl.BlockSpec((BRANCH, D_MODEL), row),
            pl.BlockSpec((1, D_MODEL), row),
        ],
        out_specs=xblk,
        out_shape=jax.ShapeDtypeStruct((n, D_MODEL), _F32),
        scratch_shapes=[pltpu.VMEM((SGU_TB, BRANCH), _F32)],
        compiler_params=pltpu.CompilerParams(
            dimension_semantics=("parallel",), vmem_limit_bytes=VMEM_LIMIT),
        name="sgu_layer",
    )(x, g, w_in, ln_g, ln_b, ws, bs, w_out, final_g)


def kernel(x, norm_g, final_g, s5_w_in, s5_A_re, s5_A_im, s5_log_dt, s5_B_re, s5_B_im, s5_C_re, s5_C_im, s5_D, s5_w_glu, s5_b_glu, s5_w_out, sgu_w_in, sgu_ln_g, sgu_ln_b, sgu_w_s, sgu_b_s, sgu_w_out):
    tz, mb, mc, dec_re, dec_im = _s5_operators(
        s5_A_re[0], s5_A_im[0], s5_log_dt[0], s5_B_re[0], s5_B_im[0], s5_C_re[0], s5_C_im[0])
    u, gate = _s5_in(x, norm_g[0][None, :], s5_w_in[0].astype(_BF))
    y = _s5_mix(u, tz, mb, mc, dec_re, dec_im, s5_D[0][None, :])
    x1 = _s5_out(y, gate, x, s5_w_glu[0].astype(_BF), s5_b_glu[0][None, :], s5_w_out[0].astype(_BF))

    mask = jnp.tril(jnp.ones((SGU_BLOCK, SGU_BLOCK), dtype=bool))
    ws = jnp.where(mask[None], sgu_w_s[0], 0.0).astype(_BF)
    bs = jnp.repeat(jnp.transpose(sgu_b_s[0]), SGU_HEAD_DIM, axis=1)
    out = _sgu_layer(x1.reshape(BATCH * SEQ, D_MODEL), norm_g[1][None, :], sgu_w_in[0].astype(_BF),
                     sgu_ln_g[0][None, :], sgu_ln_b[0][None, :], ws, bs,
                     sgu_w_out[0].astype(_BF), final_g[None, :])
    return out.reshape(BATCH, SEQ, D_MODEL)
```

```python
import functools
import math

import numpy as np
import jax
import jax.numpy as jnp
from jax import lax
from jax.experimental import pallas as pl
from jax.experimental.pallas import tpu as pltpu

D_MODEL = 1024
BATCH = 16
SEQ = 2048
BRANCH = D_MODEL
S5_GROUP = 16
S5_GROUPS = BRANCH // S5_GROUP
S5_STATE = 64
SGU_BLOCK = 128
SGU_HEADS = 8
SGU_HEAD_DIM = BRANCH // SGU_HEADS
RMS_EPS = 1e-6
LN_EPS = 1e-5
RE_CLIP = -1e-4

LANES = 128
TCH = 16
NCHUNK = SEQ // TCH
SLOTS = LANES // S5_GROUP
NJ = BRANCH // LANES
GW = TCH * S5_GROUP
PAIRS_PER_J = SLOTS // 2
VMEM_LIMIT = 56 * 1024 * 1024

_HI = lax.Precision.HIGHEST


def _gelu(x):
    return jax.nn.gelu(x)


def _sigmoid(x):
    return jax.nn.sigmoid(x)


def _slot_frame_table():
    a = np.arange(SLOTS)[:, None]
    x = np.arange(TCH)[None, :]
    half, k = x // SLOTS, x % SLOTS
    return SLOTS * half + (k - a) % SLOTS


_FRAME_OF_SLOT = _slot_frame_table()


def _selector_tables():
    frame = _FRAME_OF_SLOT[np.arange(S5_GROUPS) % SLOTS]
    tau = frame[:, None, :] - frame[:, :, None]
    sel = (tau[..., None] == np.arange(TCH)).astype(np.float32)
    k_in = (TCH - 1 - frame).astype(np.float32)
    k_out = (frame + 1).astype(np.float32)
    return sel, k_in, k_out


_SEL_TAU, _K_IN, _K_OUT = _selector_tables()


def _s5_operators(A_re, A_im, log_dt, B_re, B_im, C_re, C_im, dtype=jnp.bfloat16):
    f32 = jnp.float32
    G, P, H = S5_GROUPS, S5_STATE, S5_GROUP
    lam_re = jnp.minimum(A_re.astype(f32), RE_CLIP)
    lam_im = A_im.astype(f32)
    dt = jnp.exp(log_dt.astype(f32))[:, None]
    ldr, ldi = lam_re * dt, lam_im * dt

    def power(k, r=ldr, i=ldi):
        mag = jnp.exp(k * r)
        return mag * jnp.cos(k * i), mag * jnp.sin(k * i)

    lb1_re, lb1_im = power(1.0)
    nr, ni = lb1_re - 1.0, lb1_im
    den = lam_re * lam_re + lam_im * lam_im
    cf_re = (nr * lam_re + ni * lam_im) / den
    cf_im = (ni * lam_re - nr * lam_im) / den
    bb_re = cf_re[..., None] * B_re - cf_im[..., None] * B_im
    bb_im = cf_re[..., None] * B_im + cf_im[..., None] * B_re
    c_re = jnp.swapaxes(C_re.astype(f32), 1, 2)
    c_im = jnp.swapaxes(C_im.astype(f32), 1, 2)

    pt_re, pt_im = power(jnp.arange(TCH, dtype=f32)[:, None, None])
    lb_re = pt_re[..., None] * bb_re - pt_im[..., None] * bb_im
    lb_im = pt_re[..., None] * bb_im + pt_im[..., None] * bb_re
    taps = (jnp.einsum('kgph,gpc->gkhc', lb_re, c_re, precision=_HI)
            - jnp.einsum('kgph,gpc->gkhc', lb_im, c_im, precision=_HI))
    tz = jnp.einsum('gxyt,gthc->gxhyc', jnp.asarray(_SEL_TAU, dtype), taps.astype(dtype),
                    preferred_element_type=dtype).reshape(G, GW, GW)

    even = (np.arange(G) % 2 == 0).astype(np.float32)[:, None, None]
    odd = 1.0 - even

    pi_re, pi_im = power(jnp.asarray(_K_IN)[:, :, None], ldr[:, None], ldi[:, None])
    bt_re, bt_im = jnp.swapaxes(bb_re, 1, 2), jnp.swapaxes(bb_im, 1, 2)
    in_re = (pi_re[:, :, None] * bt_re[:, None] - pi_im[:, :, None] * bt_im[:, None]).reshape(G, GW, P)
    in_im = (pi_re[:, :, None] * bt_im[:, None] + pi_im[:, :, None] * bt_re[:, None]).reshape(G, GW, P)
    mb = jnp.concatenate([in_re * even, in_re * odd, in_im * even, in_im * odd], axis=-1)

    po_re, po_im = power(jnp.asarray(_K_OUT)[:, None, :], ldr[:, :, None], ldi[:, :, None])
    o_re = (po_re[..., None] * c_re[:, :, None] - po_im[..., None] * c_im[:, :, None]).reshape(G, P, GW)
    o_im = (po_re[..., None] * c_im[:, :, None] + po_im[..., None] * c_re[:, :, None]).reshape(G, P, GW)
    mc = jnp.concatenate([o_re * even, o_re * odd, -o_im * even, -o_im * odd], axis=1)

    dre, dim_ = power(float(TCH))
    dec_re = dre.reshape(G // 2, 1, 2 * P)
    dec_im = dim_.reshape(G // 2, 1, 2 * P)
    return tz, mb.astype(dtype), mc.astype(dtype), dec_re, dec_im


def _chunk_perm():
    p = np.zeros((BATCH * TCH, BATCH * TCH), np.float32)
    for b in range(BATCH):
        for s in range(TCH):
            p[s * BATCH + b, b * TCH + s] = 1.0
    return p


_PERM = _chunk_perm()

_BF = jnp.bfloat16
_F32 = jnp.float32


def _dot(a, b):
    return jnp.dot(a, b, preferred_element_type=_F32)


def _rms(x, g):
    return x * lax.rsqrt(jnp.mean(x * x, axis=-1, keepdims=True) + RMS_EPS) * g


IN_CT = 4


def _s5_in_kernel(x_ref, g_ref, perm_ref, w_ref, u_ref, gate_ref, hp_ref):
    rows = BATCH * TCH
    for c in range(IN_CT):
        xc = x_ref[:, c * TCH:(c + 1) * TCH, :].reshape(rows, D_MODEL)
        hc = _rms(xc, g_ref[...]).astype(_BF)
        hp_ref[c * rows:(c + 1) * rows, :] = _dot(perm_ref[...], hc).astype(_BF)
    hp = hp_ref[...]
    u = _dot(hp, w_ref[:, :BRANCH])
    u_ref[...] = u.reshape(IN_CT, TCH, BATCH, BRANCH)
    gate = _dot(hp, w_ref[:, BRANCH:])
    gate_ref[...] = gate.astype(_BF).reshape(IN_CT, TCH, BATCH, BRANCH)


def _s5_in(x, g, w_in):
    grid = (NCHUNK // IN_CT,)
    blk4 = (IN_CT, TCH, BATCH, BRANCH)
    return pl.pallas_call(
        _s5_in_kernel,
        grid=grid,
        in_specs=[
            pl.BlockSpec((BATCH, IN_CT * TCH, D_MODEL), lambda i: (0, i, 0)),
            pl.BlockSpec((1, D_MODEL), lambda i: (0, 0)),
            pl.BlockSpec((BATCH * TCH, BATCH * TCH), lambda i: (0, 0)),
            pl.BlockSpec((D_MODEL, 2 * BRANCH), lambda i: (0, 0)),
        ],
        out_specs=[
            pl.BlockSpec(blk4, lambda i: (i, 0, 0, 0)),
            pl.BlockSpec(blk4, lambda i: (i, 0, 0, 0)),
        ],
        out_shape=[
            jax.ShapeDtypeStruct((NCHUNK, TCH, BATCH, BRANCH), _F32),
            jax.ShapeDtypeStruct((NCHUNK, TCH, BATCH, BRANCH), _BF),
        ],
        scratch_shapes=[pltpu.VMEM((IN_CT * BATCH * TCH, D_MODEL), _BF)],
        compiler_params=pltpu.CompilerParams(
            dimension_semantics=("parallel",), vmem_limit_bytes=VMEM_LIMIT),
        name="s5_in",
    )(x, g, jnp.asarray(_PERM, _BF), w_in)


MIX_CT = 32
MIX_R = MIX_CT * BATCH
MIX_CB = 4
MIX_RT = MIX_CB * BATCH


def _s5_mix_kernel(u_ref, tz_ref, mb_ref, mc_ref, dre_ref, dim_ref, d_ref, y_ref,
                   xs_ref, ys_ref, ss_ref, hs_ref, carry_ref):
    ct = pl.program_id(1)

    @pl.when(ct == 0)
    def _():
        carry_ref[...] = jnp.zeros_like(carry_ref)

    slot = lax.broadcasted_iota(jnp.int32, (MIX_RT, LANES), 1) // S5_GROUP

    def relayout_in(rt, carry):
        c0 = pl.multiple_of(rt * MIX_CB, MIX_CB)
        r0 = pl.multiple_of(rt * MIX_RT, MIX_RT)
        rolled = []
        for s in range(TCH):
            p = u_ref[pl.ds(c0, MIX_CB), s, :, :].reshape(MIX_RT, LANES)
            r = s % SLOTS
            rolled.append(p if r == 0 else pltpu.roll(p, r * S5_GROUP, axis=1))
        for a in range(SLOTS):
            halves = []
            for half in range(2):
                acc = rolled[SLOTS * half]
                for sp in range(1, SLOTS):
                    acc = jnp.where(slot == (a + sp) % SLOTS, rolled[SLOTS * half + sp], acc)
                halves.append(acc)
            xs_ref[a, pl.ds(r0, MIX_RT), :] = jnp.concatenate(halves, axis=1).astype(_BF)
        return carry

    lax.fori_loop(0, MIX_CT // MIX_CB, relayout_in, 0)

    for q in range(PAIRS_PER_J):
        a0, a1 = 2 * q, 2 * q + 1
        x0 = xs_ref[a0]
        x1 = xs_ref[a1]
        ys_ref[a0] = _dot(x0, tz_ref[a0])
        ys_ref[a1] = _dot(x1, tz_ref[a1])
        ss_ref[q] = _dot(x0, mb_ref[a0]) + _dot(x1, mb_ref[a1])

    dre = [dre_ref[q] for q in range(PAIRS_PER_J)]
    dim_ = [dim_ref[q] for q in range(PAIRS_PER_J)]

    def chunk_step(c, state):
        r0 = pl.multiple_of(c * BATCH, BATCH)
        new = []
        for q in range(PAIRS_PER_J):
            hre, him = state[2 * q], state[2 * q + 1]
            hs_ref[q, pl.ds(r0, BATCH), 0:LANES] = hre.astype(_BF)
            hs_ref[q, pl.ds(r0, BATCH), LANES:2 * LANES] = him.astype(_BF)
            sre = ss_ref[q, pl.ds(r0, BATCH), 0:LANES]
            sim = ss_ref[q, pl.ds(r0, BATCH), LANES:2 * LANES]
            new.append(dre[q] * hre - dim_[q] * him + sre)
            new.append(dre[q] * him + dim_[q] * hre + sim)
        return tuple(new)

    init = tuple(carry_ref[q, k] for q in range(PAIRS_PER_J) for k in range(2))
    final = lax.fori_loop(0, MIX_CT, chunk_step, init)
    for q in range(PAIRS_PER_J):
        carry_ref[q, 0] = final[2 * q]
        carry_ref[q, 1] = final[2 * q + 1]

    for a in range(SLOTS):
        ys_ref[a] = ys_ref[a] + _dot(hs_ref[a // 2], mc_ref[a])

    dvec = d_ref[...]

    def relayout_out(rt, carry):
        c0 = pl.multiple_of(rt * MIX_CB, MIX_CB)
        r0 = pl.multiple_of(rt * MIX_RT, MIX_RT)
        for t in range(TCH):
            half, r = t // SLOTS, t % SLOTS
            lo = half * LANES
            acc = ys_ref[0, pl.ds(r0, MIX_RT), lo:lo + LANES]
            for a in range(1, SLOTS):
                acc = jnp.where(slot == (a + r) % SLOTS, ys_ref[a, pl.ds(r0, MIX_RT), lo:lo + LANES], acc)
            if r:
                acc = pltpu.roll(acc, (SLOTS - r) * S5_GROUP, axis=1)
            u_t = u_ref[pl.ds(c0, MIX_CB), t, :, :].reshape(MIX_RT, LANES)
            y_ref[pl.ds(c0, MIX_CB), t, :, :] = (acc + dvec * u_t).reshape(MIX_CB, BATCH, LANES)
        return carry

    lax.fori_loop(0, MIX_CT // MIX_CB, relayout_out, 0)


def _s5_mix(u, tz, mb, mc, dec_re, dec_im, d):
    grid = (NJ, NCHUNK // MIX_CT)
    blk = (MIX_CT, TCH, BATCH, LANES)
    wspec = pl.BlockSpec((SLOTS, GW, GW), lambda j, c: (j, 0, 0))
    dspec = pl.BlockSpec((PAIRS_PER_J, 1, LANES), lambda j, c: (j, 0, 0))
    return pl.pallas_call(
        _s5_mix_kernel,
        grid=grid,
        in_specs=[
            pl.BlockSpec(blk, lambda j, c: (c, 0, 0, j)),
            wspec, wspec, wspec, dspec, dspec,
            pl.BlockSpec((1, LANES), lambda j, c: (0, j)),
        ],
        out_specs=pl.BlockSpec(blk, lambda j, c: (c, 0, 0, j)),
        out_shape=jax.ShapeDtypeStruct((NCHUNK, TCH, BATCH, BRANCH), _F32),
        scratch_shapes=[
            pltpu.VMEM((SLOTS, MIX_R, GW), _BF),
            pltpu.VMEM((SLOTS, MIX_R, GW), _F32),
            pltpu.VMEM((PAIRS_PER_J, MIX_R, GW), _F32),
            pltpu.VMEM((PAIRS_PER_J, MIX_R, GW), _BF),
            pltpu.VMEM((PAIRS_PER_J, 2, BATCH, LANES), _F32),
        ],
        compiler_params=pltpu.CompilerParams(
            dimension_semantics=("parallel", "arbitrary"), vmem_limit_bytes=VMEM_LIMIT),
        name="s5_mix",
    )(u, tz, mb, mc, dec_re, dec_im, d)


def _s5_out_kernel(y_ref, gate_ref, x_ref, permt_ref, wglu_ref, bglu_ref, wout_ref, o_ref, zn_ref):
    rows = BATCH * TCH
    n = IN_CT * rows
    y = _gelu(y_ref[...].reshape(n, BRANCH))
    glu = _dot(y.astype(_BF), wglu_ref[...]) + bglu_ref[...]
    y = y * _sigmoid(glu)
    gate = gate_ref[...].reshape(n, BRANCH).astype(_F32)
    z = (y * (gate * _sigmoid(gate))).astype(_BF)
    for c in range(IN_CT):
        zn_ref[c * rows:(c + 1) * rows, :] = _dot(permt_ref[...], z[c * rows:(c + 1) * rows, :]).astype(_BF)
    out = _dot(zn_ref[...], wout_ref[...])
    for c in range(IN_CT):
        oc = out[c * rows:(c + 1) * rows, :].reshape(BATCH, TCH, D_MODEL)
        o_ref[:, c * TCH:(c + 1) * TCH, :] = x_ref[:, c * TCH:(c + 1) * TCH, :] + oc


def _s5_out(y, gate, x, w_glu, b_glu, w_out):
    grid = (NCHUNK // IN_CT,)
    blk4 = (IN_CT, TCH, BATCH, BRANCH)
    xblk = pl.BlockSpec((BATCH, IN_CT * TCH, D_MODEL), lambda i: (0, i, 0))
    return pl.pallas_call(
        _s5_out_kernel,
        grid=grid,
        in_specs=[
            pl.BlockSpec(blk4, lambda i: (i, 0, 0, 0)),
            pl.BlockSpec(blk4, lambda i: (i, 0, 0, 0)),
            xblk,
            pl.BlockSpec((BATCH * TCH, BATCH * TCH), lambda i: (0, 0)),
            pl.BlockSpec((BRANCH, BRANCH), lambda i: (0, 0)),
            pl.BlockSpec((1, BRANCH), lambda i: (0, 0)),
            pl.BlockSpec((BRANCH, D_MODEL), lambda i: (0, 0)),
        ],
        out_specs=xblk,
        out_shape=jax.ShapeDtypeStruct((BATCH, SEQ, D_MODEL), _F32),
        scratch_shapes=[pltpu.VMEM((IN_CT * BATCH * TCH, BRANCH), _BF)],
        compiler_params=pltpu.CompilerParams(
            dimension_semantics=("parallel",), vmem_limit_bytes=VMEM_LIMIT),
        name="s5_out",
    )(y, gate, x, jnp.asarray(_PERM.T, _BF), w_glu, b_glu, w_out)


SGU_TB = 512
SGU_NB = SGU_TB // SGU_BLOCK


def _sgu_kernel(x_ref, g_ref, win_ref, lng_ref, lnb_ref, ws_ref, bs_ref, wout_ref, fg_ref, o_ref, mix_ref):
    x = x_ref[...]
    h = _rms(x, g_ref[...]).astype(_BF)
    v = _gelu(_dot(h, win_ref[:, BRANCH:2 * BRANCH]))
    mu = jnp.mean(v, axis=-1, keepdims=True)
    vc = v - mu
    var = jnp.mean(vc * vc, axis=-1, keepdims=True)
    vn = (vc * lax.rsqrt(var + LN_EPS) * lng_ref[...] + lnb_ref[...]).astype(_BF)
    for hd in range(SGU_HEADS):
        lo = hd * SGU_HEAD_DIM
        rhs = jnp.concatenate(
            [vn[b * SGU_BLOCK:(b + 1) * SGU_BLOCK, lo:lo + SGU_HEAD_DIM] for b in range(SGU_NB)], axis=1)
        res = _dot(ws_ref[hd], rhs)
        for b in range(SGU_NB):
            mix_ref[b * SGU_BLOCK:(b + 1) * SGU_BLOCK, lo:lo + SGU_HEAD_DIM] = (
                res[:, b * SGU_HEAD_DIM:(b + 1) * SGU_HEAD_DIM] + bs_ref[:, lo:lo + SGU_HEAD_DIM])
    u = _gelu(_dot(h, win_ref[:, :BRANCH]))
    y = u * mix_ref[...]
    gate = _dot(h, win_ref[:, 2 * BRANCH:])
    z = (y * (gate * _sigmoid(gate))).astype(_BF)
    x2 = x + _dot(z, wout_ref[...])
    o_ref[...] = _rms(x2, fg_ref[...])


def _sgu_layer(x, g, w_in, ln_g, ln_b, ws, bs, w_out, final_g):
    n = BATCH * SEQ
    grid = (n // SGU_TB,)
    row = lambda i: (0, 0)
    xblk = pl.BlockSpec((SGU_TB, D_MODEL), lambda i: (i, 0))
    return pl.pallas_call(
        _sgu_kernel,
        grid=grid,
        in_specs=[
            xblk,
            pl.BlockSpec((1, D_MODEL), row),
            pl.BlockSpec((D_MODEL, 3 * BRANCH), row),
            pl.BlockSpec((1, BRANCH), row),
            pl.BlockSpec((1, BRANCH), row),
            pl.BlockSpec((SGU_HEADS, SGU_BLOCK, SGU_BLOCK), lambda i: (0, 0, 0)),
            pl.BlockSpec((SGU_BLOCK, BRANCH), row),
            pl.BlockSpec((BRANCH, D_MODEL), row),
            pl.BlockSpec((1, D_MODEL), row),
        ],
        out_specs=xblk,
        out_shape=jax.ShapeDtypeStruct((n, D_MODEL), _F32),
        scratch_shapes=[pltpu.VMEM((SGU_TB, BRANCH), _F32)],
        compiler_params=pltpu.CompilerParams(
            dimension_semantics=("parallel",), vmem_limit_bytes=VMEM_LIMIT),
        name="sgu_layer",
    )(x, g, w_in, ln_g, ln_b, ws, bs, w_out, final_g)


def kernel(x, norm_g, final_g, s5_w_in, s5_A_re, s5_A_im, s5_log_dt, s5_B_re, s5_B_im, s5_C_re, s5_C_im, s5_D, s5_w_glu, s5_b_glu, s5_w_out, sgu_w_in, sgu_ln_g, sgu_ln_b, sgu_w_s, sgu_b_s, sgu_w_out):
    tz, mb, mc, dec_re, dec_im = _s5_operators(
        s5_A_re[0], s5_A_im[0], s5_log_dt[0], s5_B_re[0], s5_B_im[0], s5_C_re[0], s5_C_im[0])
    u, gate = _s5_in(x, norm_g[0][None, :], s5_w_in[0].astype(_BF))
    y = _s5_mix(u, tz, mb, mc, dec_re, dec_im, s5_D[0][None, :])
    x1 = _s5_out(y, gate, x, s5_w_glu[0].astype(_BF), s5_b_glu[0][None, :], s5_w_out[0].astype(_BF))

    mask = jnp.tril(jnp.ones((SGU_BLOCK, SGU_BLOCK), dtype=bool))
    ws = jnp.where(mask[None], sgu_w_s[0], 0.0).astype(_BF)
    bs = jnp.repeat(jnp.transpose(sgu_b_s[0]), SGU_HEAD_DIM, axis=1)
    out = _sgu_layer(x1.reshape(BATCH * SEQ, D_MODEL), norm_g[1][None, :], sgu_w_in[0].astype(_BF),
                     sgu_ln_g[0][None, :], sgu_ln_b[0][None, :], ws, bs,
                     sgu_w_out[0].astype(_BF), final_g[None, :])
    return out.reshape(BATCH, SEQ, D_MODEL)
```

```python
import numpy as np
import jax
import jax.numpy as jnp
from jax import lax
from jax.experimental import pallas as pl
from jax.experimental.pallas import tpu as pltpu

D_MODEL = 1024
BATCH = 16
SEQ = 2048
BRANCH = D_MODEL
S5_GROUP = 16
S5_GROUPS = BRANCH // S5_GROUP
S5_STATE = 64
SGU_BLOCK = 128
SGU_HEADS = 8
SGU_HEAD_DIM = BRANCH // SGU_HEADS
RMS_EPS = 1e-6
LN_EPS = 1e-5
RE_CLIP = -1e-4

LANES = 128
TCH = 16
NCHUNK = SEQ // TCH
SLOTS = LANES // S5_GROUP
NJ = BRANCH // LANES
GW = TCH * S5_GROUP
PAIRS_PER_J = SLOTS // 2
VMEM_LIMIT = 56 * 1024 * 1024

_HI = lax.Precision.HIGHEST
_BF = jnp.bfloat16
_F32 = jnp.float32


def _gelu(x):
    return jax.nn.gelu(x)


def _sigmoid(x):
    return jax.nn.sigmoid(x)


def _dot(a, b):
    return jnp.dot(a, b, preferred_element_type=_F32)


def _rms(x, g):
    return x * lax.rsqrt(jnp.mean(x * x, axis=-1, keepdims=True) + RMS_EPS) * g


KPAD = 2 * GW


def _slot_frame_table():
    a = np.arange(SLOTS)[:, None]
    x = np.arange(TCH)[None, :]
    half, k = x // SLOTS, x % SLOTS
    return SLOTS * half + (k - a) % SLOTS


_FRAME_OF_SLOT = _slot_frame_table()


def _s5_operators(A_re, A_im, log_dt, B_re, B_im, C_re, C_im, D, dtype=jnp.bfloat16):
    f32 = jnp.float32
    G, P, H = S5_GROUPS, S5_STATE, S5_GROUP
    lam_re = jnp.minimum(A_re.astype(f32), RE_CLIP)
    lam_im = A_im.astype(f32)
    dt = jnp.exp(log_dt.astype(f32))[:, None]
    ldr, ldi = lam_re * dt, lam_im * dt

    def power(k):
        mag = jnp.exp(k * ldr)
        return mag * jnp.cos(k * ldi), mag * jnp.sin(k * ldi)

    pw_re, pw_im = power(jnp.arange(TCH + 1, dtype=f32)[:, None, None])
    nr, ni = pw_re[1] - 1.0, pw_im[1]
    den = lam_re * lam_re + lam_im * lam_im
    cf_re = (nr * lam_re + ni * lam_im) / den
    cf_im = (ni * lam_re - nr * lam_im) / den
    bb_re = cf_re[..., None] * B_re - cf_im[..., None] * B_im
    bb_im = cf_re[..., None] * B_im + cf_im[..., None] * B_re
    c_re = jnp.swapaxes(C_re.astype(f32), 1, 2)
    c_im = jnp.swapaxes(C_im.astype(f32), 1, 2)

    lc_re = pw_re[..., None] * c_re - pw_im[..., None] * c_im
    lc_im = pw_re[..., None] * c_im + pw_im[..., None] * c_re
    taps = (jnp.einsum('gph,kgpc->ghkc', bb_re, lc_re[:TCH], precision=_HI)
            - jnp.einsum('gph,kgpc->ghkc', bb_im, lc_im[:TCH], precision=_HI))
    skip = D.astype(f32).reshape(G, H, 1) * jnp.eye(H, dtype=f32)
    taps = taps.at[:, :, 0, :].add(skip)
    kpad = jnp.pad(taps.reshape(G, H, GW), ((0, 0), (0, 0), (GW - S5_GROUP, S5_GROUP)))

    even = (np.arange(G) % 2 == 0).astype(np.float32)[:, None, None]
    odd = 1.0 - even

    frame = jnp.asarray(_FRAME_OF_SLOT[np.arange(G) % SLOTS], f32)
    ex_in = (TCH - 1.0) - frame
    ex_out = frame + 1.0

    mag = jnp.exp(ex_in[:, :, None] * ldr[:, None])
    pi_re = mag * jnp.cos(ex_in[:, :, None] * ldi[:, None])
    pi_im = mag * jnp.sin(ex_in[:, :, None] * ldi[:, None])
    bt_re, bt_im = jnp.swapaxes(bb_re, 1, 2), jnp.swapaxes(bb_im, 1, 2)
    in_re = (pi_re[:, :, None] * bt_re[:, None] - pi_im[:, :, None] * bt_im[:, None]).reshape(G, GW, P)
    in_im = (pi_re[:, :, None] * bt_im[:, None] + pi_im[:, :, None] * bt_re[:, None]).reshape(G, GW, P)
    mb = jnp.concatenate([in_re * even, in_re * odd, in_im * even, in_im * odd], axis=-1)
    mb = mb.astype(dtype).reshape(G // 2, 2 * GW, GW)

    mag = jnp.exp(ex_out[:, None, :] * ldr[:, :, None])
    po_re = mag * jnp.cos(ex_out[:, None, :] * ldi[:, :, None])
    po_im = mag * jnp.sin(ex_out[:, None, :] * ldi[:, :, None])
    o_re = (po_re[..., None] * c_re[:, :, None] - po_im[..., None] * c_im[:, :, None]).reshape(G, P, GW)
    o_im = (po_re[..., None] * c_im[:, :, None] + po_im[..., None] * c_re[:, :, None]).reshape(G, P, GW)
    mc = jnp.concatenate([o_re * even, o_re * odd, -o_im * even, -o_im * odd], axis=1)

    dec_re = pw_re[TCH].reshape(G // 2, 1, 2 * P)
    dec_im = pw_im[TCH].reshape(G // 2, 1, 2 * P)
    return kpad, mb, mc.astype(dtype), dec_re, dec_im


def _chunk_perm():
    p = np.zeros((BATCH * TCH, BATCH * TCH), np.float32)
    for b in range(BATCH):
        for s in range(TCH):
            p[s * BATCH + b, b * TCH + s] = 1.0
    return p


_PERM = _chunk_perm()


IN_CT = 4


def _s5_in_kernel(x_ref, g_ref, perm_ref, w_ref, u_ref, gate_ref, hp_ref):
    rows = BATCH * TCH
    for c in range(IN_CT):
        xc = x_ref[:, c * TCH:(c + 1) * TCH, :].reshape(rows, D_MODEL)
        hc = _rms(xc, g_ref[...]).astype(_BF)
        hp_ref[c * rows:(c + 1) * rows, :] = _dot(perm_ref[...], hc).astype(_BF)
    hp = hp_ref[...]
    u = _dot(hp, w_ref[:, :BRANCH])
    u_ref[...] = u.astype(_BF).reshape(IN_CT, TCH, BATCH, BRANCH)
    gate = _dot(hp, w_ref[:, BRANCH:])
    gate_ref[...] = gate.astype(_BF).reshape(IN_CT, TCH, BATCH, BRANCH)


def _s5_in(x, g, w_in):
    grid = (NCHUNK // IN_CT,)
    blk4 = (IN_CT, TCH, BATCH, BRANCH)
    return pl.pallas_call(
        _s5_in_kernel,
        grid=grid,
        in_specs=[
            pl.BlockSpec((BATCH, IN_CT * TCH, D_MODEL), lambda i: (0, i, 0)),
            pl.BlockSpec((1, D_MODEL), lambda i: (0, 0)),
            pl.BlockSpec((BATCH * TCH, BATCH * TCH), lambda i: (0, 0)),
            pl.BlockSpec((D_MODEL, 2 * BRANCH), lambda i: (0, 0)),
        ],
        out_specs=[
            pl.BlockSpec(blk4, lambda i: (i, 0, 0, 0)),
            pl.BlockSpec(blk4, lambda i: (i, 0, 0, 0)),
        ],
        out_shape=[
            jax.ShapeDtypeStruct((NCHUNK, TCH, BATCH, BRANCH), _BF),
            jax.ShapeDtypeStruct((NCHUNK, TCH, BATCH, BRANCH), _BF),
        ],
        scratch_shapes=[pltpu.VMEM((IN_CT * BATCH * TCH, D_MODEL), _BF)],
        compiler_params=pltpu.CompilerParams(
            dimension_semantics=("parallel",), vmem_limit_bytes=VMEM_LIMIT),
        name="s5_in",
    )(x, g, jnp.asarray(_PERM, _BF), w_in)


MIX_CT = 32
MIX_R = MIX_CT * BATCH
MIX_CB = 4
MIX_RT = MIX_CB * BATCH

def _gather_slots(v, slot):
    rolled = [v[0]] + [pltpu.roll(v[k], k * S5_GROUP, axis=1) for k in range(1, SLOTS)]
    out = []
    for a in range(SLOTS):
        acc = rolled[0]
        for k in range(1, SLOTS):
            acc = jnp.where(slot == (a + k) % SLOTS, rolled[k], acc)
        out.append(acc)
    return out


def _scatter_slots(v, slot):
    out = []
    for k in range(SLOTS):
        acc = v[0]
        for a in range(1, SLOTS):
            acc = jnp.where(slot == (a + k) % SLOTS, v[a], acc)
        out.append(acc if k == 0 else pltpu.roll(acc, (SLOTS - k) * S5_GROUP, axis=1))
    return out


def _column_perms():
    p = np.zeros((SLOTS, GW, GW), np.float32)
    for a in range(SLOTS):
        for x in range(TCH):
            f = int(_FRAME_OF_SLOT[a, x])
            for c in range(S5_GROUP):
                p[a, f * S5_GROUP + c, x * S5_GROUP + c] = 1.0
    return p


_CPERM = _column_perms()


def _s5_mix_kernel(u_ref, kpad_ref, mb_ref, mc_ref, cperm_ref, dre_ref, dim_ref, y_ref,
                   op_ref, xs_ref, ys_ref, ss_ref, hs_ref, carry_ref):
    ct = pl.program_id(1)

    @pl.when(ct == 0)
    def _():
        carry_ref[...] = jnp.zeros_like(carry_ref)
        for a in range(SLOTS):
            kp = kpad_ref[a]
            rows = []
            for x in range(TCH):
                off = (TCH - 1 - int(_FRAME_OF_SLOT[a, x])) * S5_GROUP
                rows.append(kp[:, off:off + GW])
            toep = jnp.concatenate(rows, axis=0).astype(_BF)
            op_ref[a, 0:GW, :] = _dot(toep, cperm_ref[a]).astype(_BF)
            op_ref[a, GW:2 * GW, :] = mc_ref[a]

    slot = lax.broadcasted_iota(jnp.int32, (MIX_RT, LANES), 1) // S5_GROUP

    def relayout_in(rt, carry):
        c0 = pl.multiple_of(rt * MIX_CB, MIX_CB)
        r0 = pl.multiple_of(rt * MIX_RT, MIX_RT)
        for half in range(2):
            pieces = [u_ref[pl.ds(c0, MIX_CB), SLOTS * half + k, :, :].reshape(MIX_RT, LANES)
                      for k in range(SLOTS)]
            groups = _gather_slots(pieces, slot)
            for a in range(SLOTS):
                lo = (a % 2) * GW + half * LANES
                xs_ref[a // 2, pl.ds(r0, MIX_RT), lo:lo + LANES] = groups[a]
        return carry

    lax.fori_loop(0, MIX_CT // MIX_CB, relayout_in, 0)

    for q in range(PAIRS_PER_J):
        ss_ref[q] = _dot(xs_ref[q], mb_ref[q])

    dre = [dre_ref[q] for q in range(PAIRS_PER_J)]
    dim_ = [dim_ref[q] for q in range(PAIRS_PER_J)]

    def chunk_step(c, state):
        r0 = pl.multiple_of(c * BATCH, BATCH)
        new = []
        for q in range(PAIRS_PER_J):
            hre, him = state[2 * q], state[2 * q + 1]
            hs_ref[q, pl.ds(r0, BATCH), 0:LANES] = hre.astype(_BF)
            hs_ref[q, pl.ds(r0, BATCH), LANES:2 * LANES] = him.astype(_BF)
            sre = ss_ref[q, pl.ds(r0, BATCH), 0:LANES]
            sim = ss_ref[q, pl.ds(r0, BATCH), LANES:2 * LANES]
            new.append(dre[q] * hre - dim_[q] * him + sre)
            new.append(dre[q] * him + dim_[q] * hre + sim)
        return tuple(new)

    init = tuple(carry_ref[q, k] for q in range(PAIRS_PER_J) for k in range(2))
    final = lax.fori_loop(0, MIX_CT, chunk_step, init)
    for q in range(PAIRS_PER_J):
        carry_ref[q, 0] = final[2 * q]
        carry_ref[q, 1] = final[2 * q + 1]

    for a in range(SLOTS):
        q, m = a // 2, a % 2
        lhs = jnp.concatenate([xs_ref[q, :, m * GW:(m + 1) * GW], hs_ref[q]], axis=1)
        ys_ref[a] = _dot(lhs, op_ref[a])

    def relayout_out(rt, carry):
        c0 = pl.multiple_of(rt * MIX_CB, MIX_CB)
        r0 = pl.multiple_of(rt * MIX_RT, MIX_RT)
        for half in range(2):
            groups = [ys_ref[a, pl.ds(r0, MIX_RT), half * LANES:(half + 1) * LANES].astype(_BF)
                      for a in range(SLOTS)]
            frames = _scatter_slots(groups, slot)
            for k in range(SLOTS):
                y_ref[pl.ds(c0, MIX_CB), SLOTS * half + k, :, :] = frames[k].reshape(MIX_CB, BATCH, LANES)
        return carry

    lax.fori_loop(0, MIX_CT // MIX_CB, relayout_out, 0)


def _s5_mix(u, kpad, mb, mc, dec_re, dec_im):
    grid = (NJ, NCHUNK // MIX_CT)
    blk = (MIX_CT, TCH, BATCH, LANES)
    dspec = pl.BlockSpec((PAIRS_PER_J, 1, LANES), lambda j, c: (j, 0, 0))
    return pl.pallas_call(
        _s5_mix_kernel,
        grid=grid,
        in_specs=[
            pl.BlockSpec(blk, lambda j, c: (c, 0, 0, j)),
            pl.BlockSpec((SLOTS, S5_GROUP, KPAD), lambda j, c: (j, 0, 0)),
            pl.BlockSpec((PAIRS_PER_J, 2 * GW, GW), lambda j, c: (j, 0, 0)),
            pl.BlockSpec((SLOTS, GW, GW), lambda j, c: (j, 0, 0)),
            pl.BlockSpec((SLOTS, GW, GW), lambda j, c: (0, 0, 0)),
            dspec, dspec,
        ],
        out_specs=pl.BlockSpec(blk, lambda j, c: (c, 0, 0, j)),
        out_shape=jax.ShapeDtypeStruct((NCHUNK, TCH, BATCH, BRANCH), _BF),
        scratch_shapes=[
            pltpu.VMEM((SLOTS, 2 * GW, GW), _BF),
            pltpu.VMEM((PAIRS_PER_J, MIX_R, 2 * GW), _BF),
            pltpu.VMEM((SLOTS, MIX_R, GW), _F32),
            pltpu.VMEM((PAIRS_PER_J, MIX_R, GW), _F32),
            pltpu.VMEM((PAIRS_PER_J, MIX_R, GW), _BF),
            pltpu.VMEM((PAIRS_PER_J, 2, BATCH, LANES), _F32),
        ],
        compiler_params=pltpu.CompilerParams(
            dimension_semantics=("parallel", "arbitrary"), vmem_limit_bytes=VMEM_LIMIT),
        name="s5_mix",
    )(u, kpad, mb, mc, jnp.asarray(_CPERM, _BF), dec_re, dec_im)


def _s5_out_kernel(y_ref, gate_ref, x_ref, permt_ref, wglu_ref, bglu_ref, wout_ref, o_ref, zn_ref):
    rows = BATCH * TCH
    n = IN_CT * rows
    y = _gelu(y_ref[...].reshape(n, BRANCH).astype(_F32))
    glu = _dot(y.astype(_BF), wglu_ref[...]) + bglu_ref[...]
    y = y * _sigmoid(glu)
    gate = gate_ref[...].reshape(n, BRANCH).astype(_F32)
    z = (y * (gate * _sigmoid(gate))).astype(_BF)
    for c in range(IN_CT):
        zn_ref[c * rows:(c + 1) * rows, :] = _dot(permt_ref[...], z[c * rows:(c + 1) * rows, :]).astype(_BF)
    out = _dot(zn_ref[...], wout_ref[...])
    for c in range(IN_CT):
        oc = out[c * rows:(c + 1) * rows, :].reshape(BATCH, TCH, D_MODEL)
        o_ref[:, c * TCH:(c + 1) * TCH, :] = x_ref[:, c * TCH:(c + 1) * TCH, :] + oc


def _s5_out(y, gate, x, w_glu, b_glu, w_out):
    grid = (NCHUNK // IN_CT,)
    blk4 = (IN_CT, TCH, BATCH, BRANCH)
    xblk = pl.BlockSpec((BATCH, IN_CT * TCH, D_MODEL), lambda i: (0, i, 0))
    return pl.pallas_call(
        _s5_out_kernel,
        grid=grid,
        in_specs=[
            pl.BlockSpec(blk4, lambda i: (i, 0, 0, 0)),
            pl.BlockSpec(blk4, lambda i: (i, 0, 0, 0)),
            xblk,
            pl.BlockSpec((BATCH * TCH, BATCH * TCH), lambda i: (0, 0)),
            pl.BlockSpec((BRANCH, BRANCH), lambda i: (0, 0)),
            pl.BlockSpec((1, BRANCH), lambda i: (0, 0)),
            pl.BlockSpec((BRANCH, D_MODEL), lambda i: (0, 0)),
        ],
        out_specs=xblk,
        out_shape=jax.ShapeDtypeStruct((BATCH, SEQ, D_MODEL), _F32),
        scratch_shapes=[pltpu.VMEM((IN_CT * BATCH * TCH, BRANCH), _BF)],
        compiler_params=pltpu.CompilerParams(
            dimension_semantics=("parallel",), vmem_limit_bytes=VMEM_LIMIT),
        name="s5_out",
    )(y, gate, x, jnp.asarray(_PERM.T, _BF), w_glu, b_glu, w_out)


SGU_TB = 512
SGU_NB = SGU_TB // SGU_BLOCK


def _sgu_kernel(x_ref, g_ref, win_ref, lng_ref, lnb_ref, ws_ref, bs_ref, wout_ref, fg_ref, o_ref, mix_ref):
    x = x_ref[...]
    h = _rms(x, g_ref[...]).astype(_BF)
    v = _gelu(_dot(h, win_ref[:, BRANCH:2 * BRANCH]))
    mu = jnp.mean(v, axis=-1, keepdims=True)
    vc = v - mu
    var = jnp.mean(vc * vc, axis=-1, keepdims=True)
    vn = (vc * lax.rsqrt(var + LN_EPS) * lng_ref[...] + lnb_ref[...]).astype(_BF)
    for hd in range(SGU_HEADS):
        lo = hd * SGU_HEAD_DIM
        rhs = jnp.concatenate(
            [vn[b * SGU_BLOCK:(b + 1) * SGU_BLOCK, lo:lo + SGU_HEAD_DIM] for b in range(SGU_NB)], axis=1)
        res = _dot(ws_ref[hd], rhs)
        for b in range(SGU_NB):
            mix_ref[b * SGU_BLOCK:(b + 1) * SGU_BLOCK, lo:lo + SGU_HEAD_DIM] = (
                res[:, b * SGU_HEAD_DIM:(b + 1) * SGU_HEAD_DIM] + bs_ref[:, lo:lo + SGU_HEAD_DIM])
    u = _gelu(_dot(h, win_ref[:, :BRANCH]))
    y = u * mix_ref[...]
    gate = _dot(h, win_ref[:, 2 * BRANCH:])
    z = (y * (gate * _sigmoid(gate))).astype(_BF)
    x2 = x + _dot(z, wout_ref[...])
    o_ref[...] = _rms(x2, fg_ref[...])


def _sgu_layer(x, g, w_in, ln_g, ln_b, ws, bs, w_out, final_g):
    n = BATCH * SEQ
    grid = (n // SGU_TB,)
    row = lambda i: (0, 0)
    xblk = pl.BlockSpec((SGU_TB, D_MODEL), lambda i: (i, 0))
    return pl.pallas_call(
        _sgu_kernel,
        grid=grid,
        in_specs=[
            xblk,
            pl.BlockSpec((1, D_MODEL), row),
            pl.BlockSpec((D_MODEL, 3 * BRANCH), row),
            pl.BlockSpec((1, BRANCH), row),
            pl.BlockSpec((1, BRANCH), row),
            pl.BlockSpec((SGU_HEADS, SGU_BLOCK, SGU_BLOCK), lambda i: (0, 0, 0)),
            pl.BlockSpec((SGU_BLOCK, BRANCH), row),
            pl.BlockSpec((BRANCH, D_MODEL), row),
            pl.BlockSpec((1, D_MODEL), row),
        ],
        out_specs=xblk,
        out_shape=jax.ShapeDtypeStruct((n, D_MODEL), _F32),
        scratch_shapes=[pltpu.VMEM((SGU_TB, BRANCH), _F32)],
        compiler_params=pltpu.CompilerParams(
            dimension_semantics=("parallel",), vmem_limit_bytes=VMEM_LIMIT),
        name="sgu_layer",
    )(x, g, w_in, ln_g, ln_b, ws, bs, w_out, final_g)


def kernel(x, norm_g, final_g, s5_w_in, s5_A_re, s5_A_im, s5_log_dt, s5_B_re, s5_B_im, s5_C_re, s5_C_im, s5_D, s5_w_glu, s5_b_glu, s5_w_out, sgu_w_in, sgu_ln_g, sgu_ln_b, sgu_w_s, sgu_b_s, sgu_w_out):
    kpad, mb, mc, dec_re, dec_im = _s5_operators(
        s5_A_re[0], s5_A_im[0], s5_log_dt[0], s5_B_re[0], s5_B_im[0], s5_C_re[0], s5_C_im[0], s5_D[0])
    u, gate = _s5_in(x, norm_g[0][None, :], s5_w_in[0].astype(_BF))
    y = _s5_mix(u, kpad, mb, mc, dec_re, dec_im)
    x1 = _s5_out(y, gate, x, s5_w_glu[0].astype(_BF), s5_b_glu[0][None, :], s5_w_out[0].astype(_BF))

    mask = jnp.tril(jnp.ones((SGU_BLOCK, SGU_BLOCK), dtype=bool))
    ws = jnp.where(mask[None], sgu_w_s[0], 0.0).astype(_BF)
    bs = jnp.repeat(jnp.transpose(sgu_b_s[0]), SGU_HEAD_DIM, axis=1)
    out = _sgu_layer(x1.reshape(BATCH * SEQ, D_MODEL), norm_g[1][None, :], sgu_w_in[0].astype(_BF),
                     sgu_ln_g[0][None, :], sgu_ln_b[0][None, :], ws, bs,
                     sgu_w_out[0].astype(_BF), final_g[None, :])
    return out.reshape(BATCH, SEQ, D_MODEL)
```

```python
import math

import numpy as np
import jax
import jax.numpy as jnp
from jax import lax
from jax.experimental import pallas as pl
from jax.experimental.pallas import tpu as pltpu

D_MODEL = 1024
BATCH = 16
SEQ = 2048
BRANCH = D_MODEL
S5_GROUP = 16
S5_GROUPS = BRANCH // S5_GROUP
S5_STATE = 64
SGU_BLOCK = 128
SGU_HEADS = 8
SGU_HEAD_DIM = BRANCH // SGU_HEADS
RMS_EPS = 1e-6
LN_EPS = 1e-5
RE_CLIP = -1e-4

LANES = 128
TCH = 16
NCHUNK = SEQ // TCH
SLOTS = LANES // S5_GROUP
NJ = BRANCH // LANES
GW = TCH * S5_GROUP
PAIRS_PER_J = SLOTS // 2
VMEM_LIMIT = 56 * 1024 * 1024

_HI = lax.Precision.HIGHEST
_BF = jnp.bfloat16
_F32 = jnp.float32


_GELU_C1 = math.sqrt(2.0 / math.pi)
_GELU_C2 = _GELU_C1 * 0.044715


def _gelu(x):
    inner = x * (_GELU_C1 + _GELU_C2 * (x * x))
    return (0.5 * x) * (1.0 + jnp.tanh(inner))


def _sigmoid(x):
    return jax.nn.sigmoid(x)


def _dot(a, b):
    return jnp.dot(a, b, preferred_element_type=_F32)


def _rms(x, g):
    return x * lax.rsqrt(jnp.mean(x * x, axis=-1, keepdims=True) + RMS_EPS) * g


KPAD = 2 * GW


def _slot_frame_table():
    a = np.arange(SLOTS)[:, None]
    x = np.arange(TCH)[None, :]
    half, k = x // SLOTS, x % SLOTS
    return SLOTS * half + (k - a) % SLOTS


_FRAME_OF_SLOT = _slot_frame_table()


def _s5_operators(A_re, A_im, log_dt, B_re, B_im, C_re, C_im, D, dtype=jnp.bfloat16):
    f32 = jnp.float32
    G, P, H = S5_GROUPS, S5_STATE, S5_GROUP
    lam_re = jnp.minimum(A_re.astype(f32), RE_CLIP)
    lam_im = A_im.astype(f32)
    dt = jnp.exp(log_dt.astype(f32))[:, None]
    ldr, ldi = lam_re * dt, lam_im * dt

    def power(k):
        mag = jnp.exp(k * ldr)
        return mag * jnp.cos(k * ldi), mag * jnp.sin(k * ldi)

    lb_re, lb_im = power(1.0)
    nr, ni = lb_re - 1.0, lb_im
    den = lam_re * lam_re + lam_im * lam_im
    cf_re = (nr * lam_re + ni * lam_im) / den
    cf_im = (ni * lam_re - nr * lam_im) / den
    bb_re = cf_re[..., None] * B_re - cf_im[..., None] * B_im
    bb_im = cf_re[..., None] * B_im + cf_im[..., None] * B_re
    c_re = jnp.swapaxes(C_re.astype(f32), 1, 2)
    c_im = jnp.swapaxes(C_im.astype(f32), 1, 2)

    tau = jnp.asarray(np.repeat(np.arange(TCH, dtype=np.float32), S5_GROUP))
    mag = jnp.exp(ldr[..., None] * tau)
    pt_re = mag * jnp.cos(ldi[..., None] * tau)
    pt_im = mag * jnp.sin(ldi[..., None] * tau)
    ct_re, ct_im = jnp.tile(c_re, (1, 1, TCH)), jnp.tile(c_im, (1, 1, TCH))
    lc_re = pt_re * ct_re - pt_im * ct_im
    lc_im = pt_re * ct_im + pt_im * ct_re
    taps = jnp.sum(bb_re[..., None] * lc_re[:, :, None, :] - bb_im[..., None] * lc_im[:, :, None, :], axis=1)
    skip = D.astype(f32).reshape(G, H, 1) * jnp.eye(H, dtype=f32)
    taps = taps + jnp.pad(skip, ((0, 0), (0, 0), (0, GW - S5_GROUP)))
    kpad = jnp.pad(taps, ((0, 0), (0, 0), (GW - S5_GROUP, S5_GROUP)))

    even = (np.arange(G) % 2 == 0).astype(np.float32)[:, None, None]
    odd = 1.0 - even

    frame = jnp.asarray(_FRAME_OF_SLOT[np.arange(G) % SLOTS], f32)
    ex_in = (TCH - 1.0) - frame
    ex_out = frame + 1.0

    mag = jnp.exp(ex_in[:, :, None] * ldr[:, None])
    pi_re = mag * jnp.cos(ex_in[:, :, None] * ldi[:, None])
    pi_im = mag * jnp.sin(ex_in[:, :, None] * ldi[:, None])
    bt_re, bt_im = jnp.swapaxes(bb_re, 1, 2), jnp.swapaxes(bb_im, 1, 2)
    in_re = (pi_re[:, :, None] * bt_re[:, None] - pi_im[:, :, None] * bt_im[:, None]).reshape(G, GW, P)
    in_im = (pi_re[:, :, None] * bt_im[:, None] + pi_im[:, :, None] * bt_re[:, None]).reshape(G, GW, P)
    mb = jnp.concatenate([in_re * even, in_re * odd, in_im * even, in_im * odd], axis=-1)
    mb = mb.astype(dtype).reshape(G // 2, 2 * GW, GW)

    mag = jnp.exp(ex_out[:, None, :] * ldr[:, :, None])
    po_re = mag * jnp.cos(ex_out[:, None, :] * ldi[:, :, None])
    po_im = mag * jnp.sin(ex_out[:, None, :] * ldi[:, :, None])
    o_re = (po_re[..., None] * c_re[:, :, None] - po_im[..., None] * c_im[:, :, None]).reshape(G, P, GW)
    o_im = (po_re[..., None] * c_im[:, :, None] + po_im[..., None] * c_re[:, :, None]).reshape(G, P, GW)
    mc = jnp.concatenate([o_re * even, o_re * odd, -o_im * even, -o_im * odd], axis=1)

    dre, dim_ = power(float(TCH))
    dec_re = dre.reshape(G // 2, 1, 2 * P)
    dec_im = dim_.reshape(G // 2, 1, 2 * P)
    return kpad, mb, mc.astype(dtype), dec_re, dec_im


def _chunk_perm():
    p = np.zeros((BATCH * TCH, BATCH * TCH), np.float32)
    for b in range(BATCH):
        for s in range(TCH):
            p[s * BATCH + b, b * TCH + s] = 1.0
    return p


_PERM = _chunk_perm()


IN_CT = 4


def _s5_in_kernel(x_ref, g_ref, perm_ref, w_ref, u_ref, gate_ref, hp_ref):
    rows = BATCH * TCH
    for c in range(IN_CT):
        xc = x_ref[:, c * TCH:(c + 1) * TCH, :].reshape(rows, D_MODEL)
        hc = _rms(xc, g_ref[...]).astype(_BF)
        hp_ref[c * rows:(c + 1) * rows, :] = _dot(perm_ref[...], hc).astype(_BF)
    hp = hp_ref[...]
    gate = _dot(hp, w_ref[:, BRANCH:])
    gate_ref[...] = (gate * _sigmoid(gate)).astype(_BF).reshape(IN_CT, TCH, BATCH, BRANCH)
    u = _dot(hp, w_ref[:, :BRANCH])
    u_ref[...] = u.astype(_BF).reshape(IN_CT, TCH, BATCH, BRANCH)


def _s5_in(x, g, w_in):
    grid = (NCHUNK // IN_CT,)
    blk4 = (IN_CT, TCH, BATCH, BRANCH)
    return pl.pallas_call(
        _s5_in_kernel,
        grid=grid,
        in_specs=[
            pl.BlockSpec((BATCH, IN_CT * TCH, D_MODEL), lambda i: (0, i, 0)),
            pl.BlockSpec((1, D_MODEL), lambda i: (0, 0)),
            pl.BlockSpec((BATCH * TCH, BATCH * TCH), lambda i: (0, 0)),
            pl.BlockSpec((D_MODEL, 2 * BRANCH), lambda i: (0, 0)),
        ],
        out_specs=[
            pl.BlockSpec(blk4, lambda i: (i, 0, 0, 0)),
            pl.BlockSpec(blk4, lambda i: (i, 0, 0, 0)),
        ],
        out_shape=[
            jax.ShapeDtypeStruct((NCHUNK, TCH, BATCH, BRANCH), _BF),
            jax.ShapeDtypeStruct((NCHUNK, TCH, BATCH, BRANCH), _BF),
        ],
        scratch_shapes=[pltpu.VMEM((IN_CT * BATCH * TCH, D_MODEL), _BF)],
        compiler_params=pltpu.CompilerParams(
            dimension_semantics=("parallel",), vmem_limit_bytes=VMEM_LIMIT),
        name="s5_in",
    )(x, g, jnp.asarray(_PERM, _BF), w_in)


MIX_CT = 32
MIX_R = MIX_CT * BATCH
MIX_CB = 4
MIX_RT = MIX_CB * BATCH

def _gather_slots(v, slot):
    rolled = [v[0]] + [pltpu.roll(v[k], k * S5_GROUP, axis=1) for k in range(1, SLOTS)]
    out = []
    for a in range(SLOTS):
        acc = rolled[0]
        for k in range(1, SLOTS):
            acc = jnp.where(slot == (a + k) % SLOTS, rolled[k], acc)
        out.append(acc)
    return out


def _scatter_slots(v, slot):
    out = []
    for k in range(SLOTS):
        acc = v[0]
        for a in range(1, SLOTS):
            acc = jnp.where(slot == (a + k) % SLOTS, v[a], acc)
        out.append(acc if k == 0 else pltpu.roll(acc, (SLOTS - k) * S5_GROUP, axis=1))
    return out


def _column_perms():
    p = np.zeros((SLOTS, GW, GW), np.float32)
    for a in range(SLOTS):
        for x in range(TCH):
            f = int(_FRAME_OF_SLOT[a, x])
            for c in range(S5_GROUP):
                p[a, f * S5_GROUP + c, x * S5_GROUP + c] = 1.0
    return p


_CPERM = _column_perms()


def _s5_mix_kernel(u_ref, kpad_ref, mb_ref, mc_ref, cperm_ref, dre_ref, dim_ref, y_ref,
                   op_ref, xs_ref, ys_ref, ss_ref, hs_ref, carry_ref):
    ct = pl.program_id(1)

    @pl.when(ct == 0)
    def _():
        carry_ref[...] = jnp.zeros_like(carry_ref)
        for a in range(SLOTS):
            kp = kpad_ref[a]
            rows = []
            for x in range(TCH):
                off = (TCH - 1 - int(_FRAME_OF_SLOT[a, x])) * S5_GROUP
                rows.append(kp[:, off:off + GW])
            toep = jnp.concatenate(rows, axis=0).astype(_BF)
            op_ref[a, 0:GW, :] = _dot(toep, cperm_ref[a]).astype(_BF)
            op_ref[a, GW:2 * GW, :] = mc_ref[a]

    slot = lax.broadcasted_iota(jnp.int32, (MIX_RT, LANES), 1) // S5_GROUP

    def relayout_in(rt, carry):
        c0 = pl.multiple_of(rt * MIX_CB, MIX_CB)
        r0 = pl.multiple_of(rt * MIX_RT, MIX_RT)
        for half in range(2):
            pieces = [u_ref[pl.ds(c0, MIX_CB), SLOTS * half + k, :, :].reshape(MIX_RT, LANES)
                      for k in range(SLOTS)]
            groups = _gather_slots(pieces, slot)
            for a in range(SLOTS):
                lo = (a % 2) * GW + half * LANES
                xs_ref[a // 2, pl.ds(r0, MIX_RT), lo:lo + LANES] = groups[a]
        return carry

    lax.fori_loop(0, MIX_CT // MIX_CB, relayout_in, 0)

    for q in range(PAIRS_PER_J):
        ss_ref[q] = _dot(xs_ref[q], mb_ref[q])

    dre = [dre_ref[q] for q in range(PAIRS_PER_J)]
    dim_ = [dim_ref[q] for q in range(PAIRS_PER_J)]

    def chunk_step(c, state):
        r0 = pl.multiple_of(c * BATCH, BATCH)
        new = []
        for q in range(PAIRS_PER_J):
            hre, him = state[2 * q], state[2 * q + 1]
            hs_ref[q, pl.ds(r0, BATCH), 0:LANES] = hre.astype(_BF)
            hs_ref[q, pl.ds(r0, BATCH), LANES:2 * LANES] = him.astype(_BF)
            sre = ss_ref[q, pl.ds(r0, BATCH), 0:LANES]
            sim = ss_ref[q, pl.ds(r0, BATCH), LANES:2 * LANES]
            new.append(dre[q] * hre - dim_[q] * him + sre)
            new.append(dre[q] * him + dim_[q] * hre + sim)
        return tuple(new)

    init = tuple(carry_ref[q, k] for q in range(PAIRS_PER_J) for k in range(2))
    final = lax.fori_loop(0, MIX_CT, chunk_step, init)
    for q in range(PAIRS_PER_J):
        carry_ref[q, 0] = final[2 * q]
        carry_ref[q, 1] = final[2 * q + 1]

    for a in range(SLOTS):
        q, m = a // 2, a % 2
        lhs = jnp.concatenate([xs_ref[q, :, m * GW:(m + 1) * GW], hs_ref[q]], axis=1)
        ys_ref[a] = _dot(lhs, op_ref[a])

    def relayout_out(rt, carry):
        c0 = pl.multiple_of(rt * MIX_CB, MIX_CB)
        r0 = pl.multiple_of(rt * MIX_RT, MIX_RT)
        for half in range(2):
            groups = [ys_ref[a, pl.ds(r0, MIX_RT), half * LANES:(half + 1) * LANES].astype(_BF)
                      for a in range(SLOTS)]
            frames = _scatter_slots(groups, slot)
            for k in range(SLOTS):
                y_ref[pl.ds(c0, MIX_CB), SLOTS * half + k, :, :] = frames[k].reshape(MIX_CB, BATCH, LANES)
        return carry

    lax.fori_loop(0, MIX_CT // MIX_CB, relayout_out, 0)


def _s5_mix(u, kpad, mb, mc, dec_re, dec_im):
    grid = (NJ, NCHUNK // MIX_CT)
    blk = (MIX_CT, TCH, BATCH, LANES)
    dspec = pl.BlockSpec((PAIRS_PER_J, 1, LANES), lambda j, c: (j, 0, 0))
    return pl.pallas_call(
        _s5_mix_kernel,
        grid=grid,
        in_specs=[
            pl.BlockSpec(blk, lambda j, c: (c, 0, 0, j)),
            pl.BlockSpec((SLOTS, S5_GROUP, KPAD), lambda j, c: (j, 0, 0)),
            pl.BlockSpec((PAIRS_PER_J, 2 * GW, GW), lambda j, c: (j, 0, 0)),
            pl.BlockSpec((SLOTS, GW, GW), lambda j, c: (j, 0, 0)),
            pl.BlockSpec((SLOTS, GW, GW), lambda j, c: (0, 0, 0)),
            dspec, dspec,
        ],
        out_specs=pl.BlockSpec(blk, lambda j, c: (c, 0, 0, j)),
        out_shape=jax.ShapeDtypeStruct((NCHUNK, TCH, BATCH, BRANCH), _BF),
        scratch_shapes=[
            pltpu.VMEM((SLOTS, 2 * GW, GW), _BF),
            pltpu.VMEM((PAIRS_PER_J, MIX_R, 2 * GW), _BF),
            pltpu.VMEM((SLOTS, MIX_R, GW), _F32),
            pltpu.VMEM((PAIRS_PER_J, MIX_R, GW), _F32),
            pltpu.VMEM((PAIRS_PER_J, MIX_R, GW), _BF),
            pltpu.VMEM((PAIRS_PER_J, 2, BATCH, LANES), _F32),
        ],
        compiler_params=pltpu.CompilerParams(
            dimension_semantics=("parallel", "arbitrary"), vmem_limit_bytes=VMEM_LIMIT),
        name="s5_mix",
    )(u, kpad, mb, mc, jnp.asarray(_CPERM, _BF), dec_re, dec_im)


def _s5_out_kernel(y_ref, gate_ref, x_ref, permt_ref, wglu_ref, bglu_ref, wout_ref, o_ref, zn_ref):
    rows = BATCH * TCH
    n = IN_CT * rows
    y = _gelu(y_ref[...].reshape(n, BRANCH).astype(_F32))
    glu = _dot(y.astype(_BF), wglu_ref[...]) + bglu_ref[...]
    y = y * _sigmoid(glu)
    z = (y * gate_ref[...].reshape(n, BRANCH).astype(_F32)).astype(_BF)
    for c in range(IN_CT):
        zn_ref[c * rows:(c + 1) * rows, :] = _dot(permt_ref[...], z[c * rows:(c + 1) * rows, :]).astype(_BF)
    out = _dot(zn_ref[...], wout_ref[...])
    for c in range(IN_CT):
        oc = out[c * rows:(c + 1) * rows, :].reshape(BATCH, TCH, D_MODEL)
        o_ref[:, c * TCH:(c + 1) * TCH, :] = x_ref[:, c * TCH:(c + 1) * TCH, :] + oc


def _s5_out(y, gate, x, w_glu, b_glu, w_out):
    grid = (NCHUNK // IN_CT,)
    blk4 = (IN_CT, TCH, BATCH, BRANCH)
    xblk = pl.BlockSpec((BATCH, IN_CT * TCH, D_MODEL), lambda i: (0, i, 0))
    return pl.pallas_call(
        _s5_out_kernel,
        grid=grid,
        in_specs=[
            pl.BlockSpec(blk4, lambda i: (i, 0, 0, 0)),
            pl.BlockSpec(blk4, lambda i: (i, 0, 0, 0)),
            xblk,
            pl.BlockSpec((BATCH * TCH, BATCH * TCH), lambda i: (0, 0)),
            pl.BlockSpec((BRANCH, BRANCH), lambda i: (0, 0)),
            pl.BlockSpec((1, BRANCH), lambda i: (0, 0)),
            pl.BlockSpec((BRANCH, D_MODEL), lambda i: (0, 0)),
        ],
        out_specs=xblk,
        out_shape=jax.ShapeDtypeStruct((BATCH, SEQ, D_MODEL), _F32),
        scratch_shapes=[pltpu.VMEM((IN_CT * BATCH * TCH, BRANCH), _BF)],
        compiler_params=pltpu.CompilerParams(
            dimension_semantics=("parallel",), vmem_limit_bytes=VMEM_LIMIT),
        name="s5_out",
    )(y, gate, x, jnp.asarray(_PERM.T, _BF), w_glu, b_glu, w_out)


SGU_TB = 512
SGU_NB = SGU_TB // SGU_BLOCK


def _sgu_kernel(x_ref, g_ref, win_ref, lng_ref, lnb_ref, ws_ref, bs_ref, wout_ref, fg_ref, o_ref, mix_ref):
    x = x_ref[...]
    h = _rms(x, g_ref[...]).astype(_BF)
    v = _dot(h, win_ref[:, BRANCH:2 * BRANCH])
    u = _dot(h, win_ref[:, :BRANCH])
    gate = _dot(h, win_ref[:, 2 * BRANCH:])
    v = _gelu(v)
    mu = jnp.mean(v, axis=-1, keepdims=True)
    vc = v - mu
    var = jnp.mean(vc * vc, axis=-1, keepdims=True)
    vn = (vc * lax.rsqrt(var + LN_EPS) * lng_ref[...] + lnb_ref[...]).astype(_BF)
    for hd in range(SGU_HEADS):
        lo = hd * SGU_HEAD_DIM
        rhs = jnp.concatenate(
            [vn[b * SGU_BLOCK:(b + 1) * SGU_BLOCK, lo:lo + SGU_HEAD_DIM] for b in range(SGU_NB)], axis=1)
        res = _dot(ws_ref[hd], rhs)
        for b in range(SGU_NB):
            mix_ref[b * SGU_BLOCK:(b + 1) * SGU_BLOCK, lo:lo + SGU_HEAD_DIM] = (
                res[:, b * SGU_HEAD_DIM:(b + 1) * SGU_HEAD_DIM] + bs_ref[:, lo:lo + SGU_HEAD_DIM])
    sg = gate * _sigmoid(gate)
    y = _gelu(u) * mix_ref[...]
    z = (y * sg).astype(_BF)
    x2 = x + _dot(z, wout_ref[...])
    o_ref[...] = _rms(x2, fg_ref[...])


def _sgu_layer(x, g, w_in, ln_g, ln_b, ws, bs, w_out, final_g):
    n = BATCH * SEQ
    grid = (n // SGU_TB,)
    row = lambda i: (0, 0)
    xblk = pl.BlockSpec((SGU_TB, D_MODEL), lambda i: (i, 0))
    return pl.pallas_call(
        _sgu_kernel,
        grid=grid,
        in_specs=[
            xblk,
            pl.BlockSpec((1, D_MODEL), row),
            pl.BlockSpec((D_MODEL, 3 * BRANCH), row),
            pl.BlockSpec((1, BRANCH), row),
            pl.BlockSpec((1, BRANCH), row),
            pl.BlockSpec((SGU_HEADS, SGU_BLOCK, SGU_BLOCK), lambda i: (0, 0, 0)),
            pl.BlockSpec((SGU_BLOCK, BRANCH), row),
            pl.BlockSpec((BRANCH, D_MODEL), row),
            pl.BlockSpec((1, D_MODEL), row),
        ],
        out_specs=xblk,
        out_shape=jax.ShapeDtypeStruct((n, D_MODEL), _F32),
        scratch_shapes=[pltpu.VMEM((SGU_TB, BRANCH), _F32)],
        compiler_params=pltpu.CompilerParams(
            dimension_semantics=("parallel",), vmem_limit_bytes=VMEM_LIMIT),
        name="sgu_layer",
    )(x, g, w_in, ln_g, ln_b, ws, bs, w_out, final_g)


def kernel(x, norm_g, final_g, s5_w_in, s5_A_re, s5_A_im, s5_log_dt, s5_B_re, s5_B_im, s5_C_re, s5_C_im, s5_D, s5_w_glu, s5_b_glu, s5_w_out, sgu_w_in, sgu_ln_g, sgu_ln_b, sgu_w_s, sgu_b_s, sgu_w_out):
    kpad, mb, mc, dec_re, dec_im = _s5_operators(
        s5_A_re[0], s5_A_im[0], s5_log_dt[0], s5_B_re[0], s5_B_im[0], s5_C_re[0], s5_C_im[0], s5_D[0])
    u, gate = _s5_in(x, norm_g[0][None, :], s5_w_in[0].astype(_BF))
    y = _s5_mix(u, kpad, mb, mc, dec_re, dec_im)
    x1 = _s5_out(y, gate, x, s5_w_glu[0].astype(_BF), s5_b_glu[0][None, :], s5_w_out[0].astype(_BF))

    mask = jnp.tril(jnp.ones((SGU_BLOCK, SGU_BLOCK), dtype=bool))
    ws = jnp.where(mask[None], sgu_w_s[0], 0.0).astype(_BF)
    bs = jnp.repeat(jnp.transpose(sgu_b_s[0]), SGU_HEAD_DIM, axis=1)
    out = _sgu_layer(x1.reshape(BATCH * SEQ, D_MODEL), norm_g[1][None, :], sgu_w_in[0].astype(_BF),
                     sgu_ln_g[0][None, :], sgu_ln_b[0][None, :], ws, bs,
                     sgu_w_out[0].astype(_BF), final_g[None, :])
    return out.reshape(BATCH, SEQ, D_MODEL)
```

```python
import math

import numpy as np
import jax
import jax.numpy as jnp
from jax import lax
from jax.experimental import pallas as pl
from jax.experimental.pallas import tpu as pltpu

D_MODEL = 1024
BATCH = 16
SEQ = 2048
BRANCH = D_MODEL
S5_GROUP = 16
S5_GROUPS = BRANCH // S5_GROUP
S5_STATE = 64
SGU_BLOCK = 128
SGU_HEADS = 8
SGU_HEAD_DIM = BRANCH // SGU_HEADS
RMS_EPS = 1e-6
LN_EPS = 1e-5
RE_CLIP = -1e-4

LANES = 128
TCH = 16
NCHUNK = SEQ // TCH
SLOTS = LANES // S5_GROUP
NJ = BRANCH // LANES
GW = TCH * S5_GROUP
PAIRS_PER_J = SLOTS // 2
VMEM_LIMIT = 56 * 1024 * 1024

_HI = lax.Precision.HIGHEST
_BF = jnp.bfloat16
_F32 = jnp.float32


_GELU_C1 = math.sqrt(2.0 / math.pi)
_GELU_C2 = _GELU_C1 * 0.044715


def _gelu(x):
    inner = x * (_GELU_C1 + _GELU_C2 * (x * x))
    return (0.5 * x) * (1.0 + jnp.tanh(inner))


def _sigmoid(x):
    return jax.nn.sigmoid(x)


def _dot(a, b):
    return jnp.dot(a, b, preferred_element_type=_F32)


def _rms(x, g):
    return x * lax.rsqrt(jnp.mean(x * x, axis=-1, keepdims=True) + RMS_EPS) * g


KPAD = 2 * GW


def _slot_frame_table():
    a = np.arange(SLOTS)[:, None]
    x = np.arange(TCH)[None, :]
    half, k = x // SLOTS, x % SLOTS
    return SLOTS * half + (k - a) % SLOTS


_FRAME_OF_SLOT = _slot_frame_table()


def _s5_operators(A_re, A_im, log_dt, B_re, B_im, C_re, C_im, D, dtype=jnp.bfloat16):
    f32 = jnp.float32
    G, P, H = S5_GROUPS, S5_STATE, S5_GROUP
    lam_re = jnp.minimum(A_re.astype(f32), RE_CLIP)
    lam_im = A_im.astype(f32)
    dt = jnp.exp(log_dt.astype(f32))[:, None]
    ldr, ldi = lam_re * dt, lam_im * dt

    def power(k):
        mag = jnp.exp(k * ldr)
        return mag * jnp.cos(k * ldi), mag * jnp.sin(k * ldi)

    lb_re, lb_im = power(1.0)
    nr, ni = lb_re - 1.0, lb_im
    den = lam_re * lam_re + lam_im * lam_im
    cf_re = (nr * lam_re + ni * lam_im) / den
    cf_im = (ni * lam_re - nr * lam_im) / den
    bb_re = cf_re[..., None] * B_re - cf_im[..., None] * B_im
    bb_im = cf_re[..., None] * B_im + cf_im[..., None] * B_re
    c_re = jnp.swapaxes(C_re.astype(f32), 1, 2)
    c_im = jnp.swapaxes(C_im.astype(f32), 1, 2)

    tau = jnp.asarray(np.repeat(np.arange(TCH, dtype=np.float32), S5_GROUP))
    mag = jnp.exp(ldr[..., None] * tau)
    pt_re = mag * jnp.cos(ldi[..., None] * tau)
    pt_im = mag * jnp.sin(ldi[..., None] * tau)
    expand = jnp.asarray(np.tile(np.eye(S5_GROUP, dtype=np.float32), (1, TCH)))
    ct_re = jnp.einsum('gpc,cl->gpl', c_re, expand, precision=_HI)
    ct_im = jnp.einsum('gpc,cl->gpl', c_im, expand, precision=_HI)
    lc_re = pt_re * ct_re - pt_im * ct_im
    lc_im = pt_re * ct_im + pt_im * ct_re
    taps = jnp.sum(bb_re[..., None] * lc_re[:, :, None, :] - bb_im[..., None] * lc_im[:, :, None, :], axis=1)
    skip = D.astype(f32).reshape(G, H, 1) * jnp.eye(H, dtype=f32)
    taps = taps + jnp.pad(skip, ((0, 0), (0, 0), (0, GW - S5_GROUP)))
    kpad = jnp.pad(taps, ((0, 0), (0, 0), (GW - S5_GROUP, S5_GROUP)))

    even = (np.arange(G) % 2 == 0).astype(np.float32)[:, None, None]
    odd = 1.0 - even

    frame = jnp.asarray(_FRAME_OF_SLOT[np.arange(G) % SLOTS], f32)
    ex_in = (TCH - 1.0) - frame

    mag = jnp.exp(ex_in[:, :, None] * ldr[:, None])
    pi_re = mag * jnp.cos(ex_in[:, :, None] * ldi[:, None])
    pi_im = mag * jnp.sin(ex_in[:, :, None] * ldi[:, None])
    bt_re, bt_im = jnp.swapaxes(bb_re, 1, 2), jnp.swapaxes(bb_im, 1, 2)
    in_re = (pi_re[:, :, None] * bt_re[:, None] - pi_im[:, :, None] * bt_im[:, None]).reshape(G, GW, P)
    in_im = (pi_re[:, :, None] * bt_im[:, None] + pi_im[:, :, None] * bt_re[:, None]).reshape(G, GW, P)
    mb = jnp.concatenate([in_re * even, in_re * odd, in_im * even, in_im * odd], axis=-1)
    mb = mb.astype(dtype).reshape(G // 2, 2 * GW, GW)

    ex_lane = jnp.asarray(np.repeat(_FRAME_OF_SLOT[np.arange(G) % SLOTS] + 1.0, S5_GROUP, axis=1), f32)
    mag = jnp.exp(ex_lane[:, None, :] * ldr[:, :, None])
    po_re = mag * jnp.cos(ex_lane[:, None, :] * ldi[:, :, None])
    po_im = mag * jnp.sin(ex_lane[:, None, :] * ldi[:, :, None])
    o_re = po_re * ct_re - po_im * ct_im
    o_im = po_re * ct_im + po_im * ct_re
    mc = jnp.concatenate([o_re * even, o_re * odd, -o_im * even, -o_im * odd], axis=1)

    dre, dim_ = power(float(TCH))
    dec_re = dre.reshape(G // 2, 1, 2 * P)
    dec_im = dim_.reshape(G // 2, 1, 2 * P)
    return kpad, mb, mc.astype(dtype), dec_re, dec_im


def _chunk_perm():
    p = np.zeros((BATCH * TCH, BATCH * TCH), np.float32)
    for b in range(BATCH):
        for s in range(TCH):
            p[s * BATCH + b, b * TCH + s] = 1.0
    return p


_PERM = _chunk_perm()


IN_CT = 4


def _s5_in_kernel(x_ref, g_ref, perm_ref, w_ref, u_ref, gate_ref, hp_ref):
    rows = BATCH * TCH
    for c in range(IN_CT):
        xc = x_ref[:, c * TCH:(c + 1) * TCH, :].reshape(rows, D_MODEL)
        hc = _rms(xc, g_ref[...]).astype(_BF)
        hp_ref[c * rows:(c + 1) * rows, :] = _dot(perm_ref[...], hc).astype(_BF)
    hp = hp_ref[...]
    gate = _dot(hp, w_ref[:, BRANCH:])
    gate_ref[...] = (gate * _sigmoid(gate)).astype(_BF).reshape(IN_CT, TCH, BATCH, BRANCH)
    u = _dot(hp, w_ref[:, :BRANCH])
    u_ref[...] = u.astype(_BF).reshape(IN_CT, TCH, BATCH, BRANCH)


def _s5_in(x, g, w_in):
    grid = (NCHUNK // IN_CT,)
    blk4 = (IN_CT, TCH, BATCH, BRANCH)
    return pl.pallas_call(
        _s5_in_kernel,
        grid=grid,
        in_specs=[
            pl.BlockSpec((BATCH, IN_CT * TCH, D_MODEL), lambda i: (0, i, 0)),
            pl.BlockSpec((1, D_MODEL), lambda i: (0, 0)),
            pl.BlockSpec((BATCH * TCH, BATCH * TCH), lambda i: (0, 0)),
            pl.BlockSpec((D_MODEL, 2 * BRANCH), lambda i: (0, 0)),
        ],
        out_specs=[
            pl.BlockSpec(blk4, lambda i: (i, 0, 0, 0)),
            pl.BlockSpec(blk4, lambda i: (i, 0, 0, 0)),
        ],
        out_shape=[
            jax.ShapeDtypeStruct((NCHUNK, TCH, BATCH, BRANCH), _BF),
            jax.ShapeDtypeStruct((NCHUNK, TCH, BATCH, BRANCH), _BF),
        ],
        scratch_shapes=[pltpu.VMEM((IN_CT * BATCH * TCH, D_MODEL), _BF)],
        compiler_params=pltpu.CompilerParams(
            dimension_semantics=("parallel",), vmem_limit_bytes=VMEM_LIMIT),
        name="s5_in",
    )(x, g, jnp.asarray(_PERM, _BF), w_in)


MIX_CT = 32
MIX_R = MIX_CT * BATCH
MIX_CB = 4
MIX_RT = MIX_CB * BATCH

def _gather_slots(v, slot):
    rolled = [v[0]] + [pltpu.roll(v[k], k * S5_GROUP, axis=1) for k in range(1, SLOTS)]
    out = []
    for a in range(SLOTS):
        acc = rolled[0]
        for k in range(1, SLOTS):
            acc = jnp.where(slot == (a + k) % SLOTS, rolled[k], acc)
        out.append(acc)
    return out


def _scatter_slots(v, slot):
    out = []
    for k in range(SLOTS):
        acc = v[0]
        for a in range(1, SLOTS):
            acc = jnp.where(slot == (a + k) % SLOTS, v[a], acc)
        out.append(acc if k == 0 else pltpu.roll(acc, (SLOTS - k) * S5_GROUP, axis=1))
    return out


def _column_perms():
    p = np.zeros((SLOTS, GW, GW), np.float32)
    for a in range(SLOTS):
        for x in range(TCH):
            f = int(_FRAME_OF_SLOT[a, x])
            for c in range(S5_GROUP):
                p[a, f * S5_GROUP + c, x * S5_GROUP + c] = 1.0
    return p


_CPERM = _column_perms()


def _s5_mix_kernel(u_ref, kpad_ref, mb_ref, mc_ref, cperm_ref, dre_ref, dim_ref, y_ref,
                   op_ref, xs_ref, ys_ref, ss_ref, hs_ref, carry_ref):
    ct = pl.program_id(1)

    @pl.when(ct == 0)
    def _():
        carry_ref[...] = jnp.zeros_like(carry_ref)
        for a in range(SLOTS):
            kp = kpad_ref[a]
            rows = []
            for x in range(TCH):
                off = (TCH - 1 - int(_FRAME_OF_SLOT[a, x])) * S5_GROUP
                rows.append(kp[:, off:off + GW])
            toep = jnp.concatenate(rows, axis=0).astype(_BF)
            op_ref[a, 0:GW, :] = _dot(toep, cperm_ref[a]).astype(_BF)
            op_ref[a, GW:2 * GW, :] = mc_ref[a]

    slot = lax.broadcasted_iota(jnp.int32, (MIX_RT, LANES), 1) // S5_GROUP

    def relayout_in(rt, carry):
        c0 = pl.multiple_of(rt * MIX_CB, MIX_CB)
        r0 = pl.multiple_of(rt * MIX_RT, MIX_RT)
        for half in range(2):
            pieces = [u_ref[pl.ds(c0, MIX_CB), SLOTS * half + k, :, :].reshape(MIX_RT, LANES)
                      for k in range(SLOTS)]
            groups = _gather_slots(pieces, slot)
            for a in range(SLOTS):
                lo = (a % 2) * GW + half * LANES
                xs_ref[a // 2, pl.ds(r0, MIX_RT), lo:lo + LANES] = groups[a]
        return carry

    lax.fori_loop(0, MIX_CT // MIX_CB, relayout_in, 0, unroll=True)

    for q in range(PAIRS_PER_J):
        ss_ref[q] = _dot(xs_ref[q], mb_ref[q])

    dre = [dre_ref[q] for q in range(PAIRS_PER_J)]
    dim_ = [dim_ref[q] for q in range(PAIRS_PER_J)]

    def chunk_step(c, state):
        r0 = pl.multiple_of(c * BATCH, BATCH)
        new = []
        for q in range(PAIRS_PER_J):
            hre, him = state[2 * q], state[2 * q + 1]
            hs_ref[q, pl.ds(r0, BATCH), 0:LANES] = hre.astype(_BF)
            hs_ref[q, pl.ds(r0, BATCH), LANES:2 * LANES] = him.astype(_BF)
            sre = ss_ref[q, pl.ds(r0, BATCH), 0:LANES]
            sim = ss_ref[q, pl.ds(r0, BATCH), LANES:2 * LANES]
            new.append(dre[q] * hre - dim_[q] * him + sre)
            new.append(dre[q] * him + dim_[q] * hre + sim)
        return tuple(new)

    init = tuple(carry_ref[q, k] for q in range(PAIRS_PER_J) for k in range(2))
    final = lax.fori_loop(0, MIX_CT, chunk_step, init, unroll=2)
    for q in range(PAIRS_PER_J):
        carry_ref[q, 0] = final[2 * q]
        carry_ref[q, 1] = final[2 * q + 1]

    for a in range(SLOTS):
        q, m = a // 2, a % 2
        lhs = jnp.concatenate([xs_ref[q, :, m * GW:(m + 1) * GW], hs_ref[q]], axis=1)
        ys_ref[a] = _dot(lhs, op_ref[a])

    def relayout_out(rt, carry):
        c0 = pl.multiple_of(rt * MIX_CB, MIX_CB)
        r0 = pl.multiple_of(rt * MIX_RT, MIX_RT)
        for half in range(2):
            groups = [ys_ref[a, pl.ds(r0, MIX_RT), half * LANES:(half + 1) * LANES].astype(_BF)
                      for a in range(SLOTS)]
            frames = _scatter_slots(groups, slot)
            for k in range(SLOTS):
                y_ref[pl.ds(c0, MIX_CB), SLOTS * half + k, :, :] = frames[k].reshape(MIX_CB, BATCH, LANES)
        return carry

    lax.fori_loop(0, MIX_CT // MIX_CB, relayout_out, 0, unroll=True)


def _s5_mix(u, kpad, mb, mc, dec_re, dec_im):
    grid = (NJ, NCHUNK // MIX_CT)
    blk = (MIX_CT, TCH, BATCH, LANES)
    dspec = pl.BlockSpec((PAIRS_PER_J, 1, LANES), lambda j, c: (j, 0, 0))
    return pl.pallas_call(
        _s5_mix_kernel,
        grid=grid,
        in_specs=[
            pl.BlockSpec(blk, lambda j, c: (c, 0, 0, j)),
            pl.BlockSpec((SLOTS, S5_GROUP, KPAD), lambda j, c: (j, 0, 0)),
            pl.BlockSpec((PAIRS_PER_J, 2 * GW, GW), lambda j, c: (j, 0, 0)),
            pl.BlockSpec((SLOTS, GW, GW), lambda j, c: (j, 0, 0)),
            pl.BlockSpec((SLOTS, GW, GW), lambda j, c: (0, 0, 0)),
            dspec, dspec,
        ],
        out_specs=pl.BlockSpec(blk, lambda j, c: (c, 0, 0, j)),
        out_shape=jax.ShapeDtypeStruct((NCHUNK, TCH, BATCH, BRANCH), _BF),
        scratch_shapes=[
            pltpu.VMEM((SLOTS, 2 * GW, GW), _BF),
            pltpu.VMEM((PAIRS_PER_J, MIX_R, 2 * GW), _BF),
            pltpu.VMEM((SLOTS, MIX_R, GW), _F32),
            pltpu.VMEM((PAIRS_PER_J, MIX_R, GW), _F32),
            pltpu.VMEM((PAIRS_PER_J, MIX_R, GW), _BF),
            pltpu.VMEM((PAIRS_PER_J, 2, BATCH, LANES), _F32),
        ],
        compiler_params=pltpu.CompilerParams(
            dimension_semantics=("parallel", "arbitrary"), vmem_limit_bytes=VMEM_LIMIT),
        name="s5_mix",
    )(u, kpad, mb, mc, jnp.asarray(_CPERM, _BF), dec_re, dec_im)


def _s5_out_kernel(y_ref, gate_ref, x_ref, permt_ref, wglu_ref, bglu_ref, wout_ref, o_ref, zn_ref):
    rows = BATCH * TCH
    n = IN_CT * rows
    y = _gelu(y_ref[...].reshape(n, BRANCH).astype(_F32))
    glu = _dot(y.astype(_BF), wglu_ref[...]) + bglu_ref[...]
    y = y * _sigmoid(glu)
    z = (y * gate_ref[...].reshape(n, BRANCH).astype(_F32)).astype(_BF)
    for c in range(IN_CT):
        zn_ref[c * rows:(c + 1) * rows, :] = _dot(permt_ref[...], z[c * rows:(c + 1) * rows, :]).astype(_BF)
    out = _dot(zn_ref[...], wout_ref[...])
    for c in range(IN_CT):
        oc = out[c * rows:(c + 1) * rows, :].reshape(BATCH, TCH, D_MODEL)
        o_ref[:, c * TCH:(c + 1) * TCH, :] = x_ref[:, c * TCH:(c + 1) * TCH, :] + oc


def _s5_out(y, gate, x, w_glu, b_glu, w_out):
    grid = (NCHUNK // IN_CT,)
    blk4 = (IN_CT, TCH, BATCH, BRANCH)
    xblk = pl.BlockSpec((BATCH, IN_CT * TCH, D_MODEL), lambda i: (0, i, 0))
    return pl.pallas_call(
        _s5_out_kernel,
        grid=grid,
        in_specs=[
            pl.BlockSpec(blk4, lambda i: (i, 0, 0, 0)),
            pl.BlockSpec(blk4, lambda i: (i, 0, 0, 0)),
            xblk,
            pl.BlockSpec((BATCH * TCH, BATCH * TCH), lambda i: (0, 0)),
            pl.BlockSpec((BRANCH, BRANCH), lambda i: (0, 0)),
            pl.BlockSpec((1, BRANCH), lambda i: (0, 0)),
            pl.BlockSpec((BRANCH, D_MODEL), lambda i: (0, 0)),
        ],
        out_specs=xblk,
        out_shape=jax.ShapeDtypeStruct((BATCH, SEQ, D_MODEL), _F32),
        scratch_shapes=[pltpu.VMEM((IN_CT * BATCH * TCH, BRANCH), _BF)],
        compiler_params=pltpu.CompilerParams(
            dimension_semantics=("parallel",), vmem_limit_bytes=VMEM_LIMIT),
        name="s5_out",
    )(y, gate, x, jnp.asarray(_PERM.T, _BF), w_glu, b_glu, w_out)


SGU_TB = 512
SGU_NB = SGU_TB // SGU_BLOCK


def _sgu_kernel(x_ref, g_ref, win_ref, lng_ref, lnb_ref, ws_ref, bs_ref, wout_ref, fg_ref, o_ref, mix_ref):
    x = x_ref[...]
    h = _rms(x, g_ref[...]).astype(_BF)
    v = _dot(h, win_ref[:, BRANCH:2 * BRANCH])
    u = _dot(h, win_ref[:, :BRANCH])
    gate = _dot(h, win_ref[:, 2 * BRANCH:])
    v = _gelu(v)
    mu = jnp.mean(v, axis=-1, keepdims=True)
    vc = v - mu
    var = jnp.mean(vc * vc, axis=-1, keepdims=True)
    vn = (vc * lax.rsqrt(var + LN_EPS) * lng_ref[...] + lnb_ref[...]).astype(_BF)
    for hd in range(SGU_HEADS):
        lo = hd * SGU_HEAD_DIM
        rhs = jnp.concatenate(
            [vn[b * SGU_BLOCK:(b + 1) * SGU_BLOCK, lo:lo + SGU_HEAD_DIM] for b in range(SGU_NB)], axis=1)
        res = _dot(ws_ref[hd], rhs)
        for b in range(SGU_NB):
            mix_ref[b * SGU_BLOCK:(b + 1) * SGU_BLOCK, lo:lo + SGU_HEAD_DIM] = (
                res[:, b * SGU_HEAD_DIM:(b + 1) * SGU_HEAD_DIM] + bs_ref[:, lo:lo + SGU_HEAD_DIM])
    sg = gate * _sigmoid(gate)
    y = _gelu(u) * mix_ref[...]
    z = (y * sg).astype(_BF)
    x2 = x + _dot(z, wout_ref[...])
    o_ref[...] = _rms(x2, fg_ref[...])


def _sgu_layer(x, g, w_in, ln_g, ln_b, ws, bs, w_out, final_g):
    n = BATCH * SEQ
    grid = (n // SGU_TB,)
    row = lambda i: (0, 0)
    xblk = pl.BlockSpec((SGU_TB, D_MODEL), lambda i: (i, 0))
    return pl.pallas_call(
        _sgu_kernel,
        grid=grid,
        in_specs=[
            xblk,
            pl.BlockSpec((1, D_MODEL), row),
            pl.BlockSpec((D_MODEL, 3 * BRANCH), row),
            pl.BlockSpec((1, BRANCH), row),
            pl.BlockSpec((1, BRANCH), row),
            pl.BlockSpec((SGU_HEADS, SGU_BLOCK, SGU_BLOCK), lambda i: (0, 0, 0)),
            pl.BlockSpec((SGU_BLOCK, BRANCH), row),
            pl.BlockSpec((BRANCH, D_MODEL), row),
            pl.BlockSpec((1, D_MODEL), row),
        ],
        out_specs=xblk,
        out_shape=jax.ShapeDtypeStruct((n, D_MODEL), _F32),
        scratch_shapes=[pltpu.VMEM((SGU_TB, BRANCH), _F32)],
        compiler_params=pltpu.CompilerParams(
            dimension_semantics=("parallel",), vmem_limit_bytes=VMEM_LIMIT),
        name="sgu_layer",
    )(x, g, w_in, ln_g, ln_b, ws, bs, w_out, final_g)


def kernel(x, norm_g, final_g, s5_w_in, s5_A_re, s5_A_im, s5_log_dt, s5_B_re, s5_B_im, s5_C_re, s5_C_im, s5_D, s5_w_glu, s5_b_glu, s5_w_out, sgu_w_in, sgu_ln_g, sgu_ln_b, sgu_w_s, sgu_b_s, sgu_w_out):
    kpad, mb, mc, dec_re, dec_im = _s5_operators(
        s5_A_re[0], s5_A_im[0], s5_log_dt[0], s5_B_re[0], s5_B_im[0], s5_C_re[0], s5_C_im[0], s5_D[0])
    u, gate = _s5_in(x, norm_g[0][None, :], s5_w_in[0].astype(_BF))
    y = _s5_mix(u, kpad, mb, mc, dec_re, dec_im)
    x1 = _s5_out(y, gate, x, s5_w_glu[0].astype(_BF), s5_b_glu[0][None, :], s5_w_out[0].astype(_BF))

    mask = jnp.tril(jnp.ones((SGU_BLOCK, SGU_BLOCK), dtype=bool))
    ws = jnp.where(mask[None], sgu_w_s[0], 0.0).astype(_BF)
    bs = jnp.repeat(jnp.transpose(sgu_b_s[0]), SGU_HEAD_DIM, axis=1)
    out = _sgu_layer(x1.reshape(BATCH * SEQ, D_MODEL), norm_g[1][None, :], sgu_w_in[0].astype(_BF),
                     sgu_ln_g[0][None, :], sgu_ln_b[0][None, :], ws, bs,
                     sgu_w_out[0].astype(_BF), final_g[None, :])
    return out.reshape(BATCH, SEQ, D_MODEL)
```

```python
import math

import numpy as np
import jax
import jax.numpy as jnp
from jax import lax
from jax.experimental import pallas as pl
from jax.experimental.pallas import tpu as pltpu

D_MODEL = 1024
BATCH = 16
SEQ = 2048
BRANCH = D_MODEL
S5_GROUP = 16
S5_GROUPS = BRANCH // S5_GROUP
S5_STATE = 64
SGU_BLOCK = 128
SGU_HEADS = 8
SGU_HEAD_DIM = BRANCH // SGU_HEADS
RMS_EPS = 1e-6
LN_EPS = 1e-5
RE_CLIP = -1e-4

LANES = 128
TCH = 16
NCHUNK = SEQ // TCH
SLOTS = LANES // S5_GROUP
NJ = BRANCH // LANES
GW = TCH * S5_GROUP
PAIRS_PER_J = SLOTS // 2
VMEM_LIMIT = 56 * 1024 * 1024

_HI = lax.Precision.HIGHEST
_BF = jnp.bfloat16
_F32 = jnp.float32


_GELU_C1 = math.sqrt(2.0 / math.pi)
_GELU_C2 = _GELU_C1 * 0.044715


def _gelu(x):
    inner = x * (_GELU_C1 + _GELU_C2 * (x * x))
    return (0.5 * x) * (1.0 + jnp.tanh(inner))


def _sigmoid(x):
    return jax.nn.sigmoid(x)


def _dot(a, b):
    return jnp.dot(a, b, preferred_element_type=_F32)


def _rms(x, g):
    return x * lax.rsqrt(jnp.mean(x * x, axis=-1, keepdims=True) + RMS_EPS) * g


KPAD = 2 * GW


def _slot_frame_table():
    a = np.arange(SLOTS)[:, None]
    x = np.arange(TCH)[None, :]
    half, k = x // SLOTS, x % SLOTS
    return SLOTS * half + (k - a) % SLOTS


_FRAME_OF_SLOT = _slot_frame_table()


def _s5_operators(A_re, A_im, log_dt, B_re, B_im, C_re, C_im, D, dtype=jnp.bfloat16):
    f32 = jnp.float32
    G, P, H = S5_GROUPS, S5_STATE, S5_GROUP
    lam_re = jnp.minimum(A_re.astype(f32), RE_CLIP)
    lam_im = A_im.astype(f32)
    dt = jnp.exp(log_dt.astype(f32))[:, None]
    ldr, ldi = lam_re * dt, lam_im * dt

    kk = jnp.arange(TCH + 1, dtype=f32)[:, None, None]
    mag = jnp.exp(kk * ldr)
    pw_re, pw_im = mag * jnp.cos(kk * ldi), mag * jnp.sin(kk * ldi)

    nr, ni = pw_re[1] - 1.0, pw_im[1]
    den = lam_re * lam_re + lam_im * lam_im
    cf_re = (nr * lam_re + ni * lam_im) / den
    cf_im = (ni * lam_re - nr * lam_im) / den
    bb_re = cf_re[..., None] * B_re - cf_im[..., None] * B_im
    bb_im = cf_re[..., None] * B_im + cf_im[..., None] * B_re
    c_re = jnp.swapaxes(C_re.astype(f32), 1, 2)
    c_im = jnp.swapaxes(C_im.astype(f32), 1, 2)

    lane_tau = np.arange(GW) // S5_GROUP
    sel_tau = jnp.asarray((np.arange(TCH)[:, None] == lane_tau[None, :]).astype(np.float32))
    pt_re = jnp.einsum('kgp,kl->gpl', pw_re[:TCH], sel_tau, precision=_HI)
    pt_im = jnp.einsum('kgp,kl->gpl', pw_im[:TCH], sel_tau, precision=_HI)
    expand = jnp.asarray(np.tile(np.eye(S5_GROUP, dtype=np.float32), (1, TCH)))
    ct_re = jnp.einsum('gpc,cl->gpl', c_re, expand, precision=_HI)
    ct_im = jnp.einsum('gpc,cl->gpl', c_im, expand, precision=_HI)
    lc_re = pt_re * ct_re - pt_im * ct_im
    lc_im = pt_re * ct_im + pt_im * ct_re
    taps = (jnp.einsum('gph,gpl->ghl', bb_re, lc_re, precision=_HI)
            - jnp.einsum('gph,gpl->ghl', bb_im, lc_im, precision=_HI))
    skip = D.astype(f32).reshape(G, H, 1) * jnp.eye(H, dtype=f32)
    taps = taps + jnp.pad(skip, ((0, 0), (0, 0), (0, GW - S5_GROUP)))
    kpad = jnp.pad(taps, ((0, 0), (0, 0), (GW - S5_GROUP, S5_GROUP)))

    even = (np.arange(G) % 2 == 0).astype(np.float32)[:, None, None]
    odd = 1.0 - even

    frame = jnp.asarray(_FRAME_OF_SLOT[np.arange(G) % SLOTS], f32)
    ex_in = (TCH - 1.0) - frame

    mag = jnp.exp(ex_in[:, :, None] * ldr[:, None])
    pi_re = mag * jnp.cos(ex_in[:, :, None] * ldi[:, None])
    pi_im = mag * jnp.sin(ex_in[:, :, None] * ldi[:, None])
    bt_re, bt_im = jnp.swapaxes(bb_re, 1, 2), jnp.swapaxes(bb_im, 1, 2)
    in_re = (pi_re[:, :, None] * bt_re[:, None] - pi_im[:, :, None] * bt_im[:, None]).reshape(G, GW, P)
    in_im = (pi_re[:, :, None] * bt_im[:, None] + pi_im[:, :, None] * bt_re[:, None]).reshape(G, GW, P)
    mb = jnp.concatenate([in_re * even, in_re * odd, in_im * even, in_im * odd], axis=-1)
    mb = mb.astype(dtype).reshape(G // 2, 2 * GW, GW)

    ex_lane = _FRAME_OF_SLOT[:, lane_tau] + 1
    sel_out = jnp.asarray((np.arange(TCH + 1)[None, :, None] == ex_lane[:, None, :]).astype(np.float32))
    po_re = jnp.einsum('kjap,akl->japl', pw_re.reshape(TCH + 1, NJ, SLOTS, P), sel_out,
                       precision=_HI).reshape(G, P, GW)
    po_im = jnp.einsum('kjap,akl->japl', pw_im.reshape(TCH + 1, NJ, SLOTS, P), sel_out,
                       precision=_HI).reshape(G, P, GW)
    o_re = po_re * ct_re - po_im * ct_im
    o_im = po_re * ct_im + po_im * ct_re
    mc = jnp.concatenate([o_re * even, o_re * odd, -o_im * even, -o_im * odd], axis=1)

    dec_re = pw_re[TCH].reshape(G // 2, 1, 2 * P)
    dec_im = pw_im[TCH].reshape(G // 2, 1, 2 * P)
    return kpad, mb, mc.astype(dtype), dec_re, dec_im


IN_CT = 4


def _s5_in_kernel(x_ref, g_ref, w_ref, u_ref, gate_ref):
    rows = BATCH * TCH
    for c in range(IN_CT):
        xc = jnp.swapaxes(x_ref[:, c * TCH:(c + 1) * TCH, :], 0, 1).reshape(rows, D_MODEL)
        hp = _rms(xc, g_ref[...]).astype(_BF)
        gate = _dot(hp, w_ref[:, BRANCH:])
        gate_ref[c] = (gate * _sigmoid(gate)).astype(_BF).reshape(TCH, BATCH, BRANCH)
        u = _dot(hp, w_ref[:, :BRANCH])
        u_ref[c] = u.astype(_BF).reshape(TCH, BATCH, BRANCH)


def _s5_in(x, g, w_in):
    grid = (NCHUNK // IN_CT,)
    blk4 = (IN_CT, TCH, BATCH, BRANCH)
    return pl.pallas_call(
        _s5_in_kernel,
        grid=grid,
        in_specs=[
            pl.BlockSpec((BATCH, IN_CT * TCH, D_MODEL), lambda i: (0, i, 0)),
            pl.BlockSpec((1, D_MODEL), lambda i: (0, 0)),
            pl.BlockSpec((D_MODEL, 2 * BRANCH), lambda i: (0, 0)),
        ],
        out_specs=[
            pl.BlockSpec(blk4, lambda i: (i, 0, 0, 0)),
            pl.BlockSpec(blk4, lambda i: (i, 0, 0, 0)),
        ],
        out_shape=[
            jax.ShapeDtypeStruct((NCHUNK, TCH, BATCH, BRANCH), _BF),
            jax.ShapeDtypeStruct((NCHUNK, TCH, BATCH, BRANCH), _BF),
        ],
        compiler_params=pltpu.CompilerParams(
            dimension_semantics=("parallel",), vmem_limit_bytes=VMEM_LIMIT),
        name="s5_in",
    )(x, g, w_in)


MIX_CT = 32
MIX_R = MIX_CT * BATCH
MIX_CB = 4
MIX_RT = MIX_CB * BATCH

def _gather_slots(v, slot):
    rolled = [v[0]] + [pltpu.roll(v[k], k * S5_GROUP, axis=1) for k in range(1, SLOTS)]
    out = []
    for a in range(SLOTS):
        acc = rolled[0]
        for k in range(1, SLOTS):
            acc = jnp.where(slot == (a + k) % SLOTS, rolled[k], acc)
        out.append(acc)
    return out


def _scatter_slots(v, slot):
    out = []
    for k in range(SLOTS):
        acc = v[0]
        for a in range(1, SLOTS):
            acc = jnp.where(slot == (a + k) % SLOTS, v[a], acc)
        out.append(acc if k == 0 else pltpu.roll(acc, (SLOTS - k) * S5_GROUP, axis=1))
    return out


def _column_perms():
    p = np.zeros((SLOTS, GW, GW), np.float32)
    for a in range(SLOTS):
        for x in range(TCH):
            f = int(_FRAME_OF_SLOT[a, x])
            for c in range(S5_GROUP):
                p[a, f * S5_GROUP + c, x * S5_GROUP + c] = 1.0
    return p


_CPERM = _column_perms()


def _s5_mix_kernel(u_ref, kpad_ref, mb_ref, mc_ref, cperm_ref, dre_ref, dim_ref, y_ref,
                   op_ref, xs_ref, ys_ref, ss_ref, hs_ref, carry_ref):
    ct = pl.program_id(1)

    @pl.when(ct == 0)
    def _():
        carry_ref[...] = jnp.zeros_like(carry_ref)
        for a in range(SLOTS):
            kp = kpad_ref[a]
            rows = []
            for x in range(TCH):
                off = (TCH - 1 - int(_FRAME_OF_SLOT[a, x])) * S5_GROUP
                rows.append(kp[:, off:off + GW])
            toep = jnp.concatenate(rows, axis=0).astype(_BF)
            op_ref[a, 0:GW, :] = _dot(toep, cperm_ref[a]).astype(_BF)
            op_ref[a, GW:2 * GW, :] = mc_ref[a]

    slot = lax.broadcasted_iota(jnp.int32, (MIX_RT, LANES), 1) // S5_GROUP

    def relayout_in(rt, carry):
        c0 = pl.multiple_of(rt * MIX_CB, MIX_CB)
        r0 = pl.multiple_of(rt * MIX_RT, MIX_RT)
        for half in range(2):
            pieces = [u_ref[pl.ds(c0, MIX_CB), SLOTS * half + k, :, :].reshape(MIX_RT, LANES)
                      for k in range(SLOTS)]
            groups = _gather_slots(pieces, slot)
            for a in range(SLOTS):
                lo = (a % 2) * GW + half * LANES
                xs_ref[a // 2, pl.ds(r0, MIX_RT), lo:lo + LANES] = groups[a]
        return carry

    lax.fori_loop(0, MIX_CT // MIX_CB, relayout_in, 0, unroll=True)

    for q in range(PAIRS_PER_J):
        ss_ref[q] = _dot(xs_ref[q], mb_ref[q])

    dre = [dre_ref[q] for q in range(PAIRS_PER_J)]
    dim_ = [dim_ref[q] for q in range(PAIRS_PER_J)]

    def chunk_step(c, state):
        r0 = pl.multiple_of(c * BATCH, BATCH)
        new = []
        for q in range(PAIRS_PER_J):
            hre, him = state[2 * q], state[2 * q + 1]
            hs_ref[q, pl.ds(r0, BATCH), 0:LANES] = hre.astype(_BF)
            hs_ref[q, pl.ds(r0, BATCH), LANES:2 * LANES] = him.astype(_BF)
            sre = ss_ref[q, pl.ds(r0, BATCH), 0:LANES]
            sim = ss_ref[q, pl.ds(r0, BATCH), LANES:2 * LANES]
            new.append(dre[q] * hre - dim_[q] * him + sre)
            new.append(dre[q] * him + dim_[q] * hre + sim)
        return tuple(new)

    init = tuple(carry_ref[q, k] for q in range(PAIRS_PER_J) for k in range(2))
    final = lax.fori_loop(0, MIX_CT, chunk_step, init, unroll=2)
    for q in range(PAIRS_PER_J):
        carry_ref[q, 0] = final[2 * q]
        carry_ref[q, 1] = final[2 * q + 1]

    for a in range(SLOTS):
        q, m = a // 2, a % 2
        lhs = jnp.concatenate([xs_ref[q, :, m * GW:(m + 1) * GW], hs_ref[q]], axis=1)
        ys_ref[a] = _dot(lhs, op_ref[a])

    def relayout_out(rt, carry):
        c0 = pl.multiple_of(rt * MIX_CB, MIX_CB)
        r0 = pl.multiple_of(rt * MIX_RT, MIX_RT)
        for half in range(2):
            groups = [ys_ref[a, pl.ds(r0, MIX_RT), half * LANES:(half + 1) * LANES].astype(_BF)
                      for a in range(SLOTS)]
            frames = _scatter_slots(groups, slot)
            for k in range(SLOTS):
                y_ref[pl.ds(c0, MIX_CB), SLOTS * half + k, :, :] = frames[k].reshape(MIX_CB, BATCH, LANES)
        return carry

    lax.fori_loop(0, MIX_CT // MIX_CB, relayout_out, 0, unroll=True)


def _s5_mix(u, kpad, mb, mc, dec_re, dec_im):
    grid = (NJ, NCHUNK // MIX_CT)
    blk = (MIX_CT, TCH, BATCH, LANES)
    dspec = pl.BlockSpec((PAIRS_PER_J, 1, LANES), lambda j, c: (j, 0, 0))
    return pl.pallas_call(
        _s5_mix_kernel,
        grid=grid,
        in_specs=[
            pl.BlockSpec(blk, lambda j, c: (c, 0, 0, j)),
            pl.BlockSpec((SLOTS, S5_GROUP, KPAD), lambda j, c: (j, 0, 0)),
            pl.BlockSpec((PAIRS_PER_J, 2 * GW, GW), lambda j, c: (j, 0, 0)),
            pl.BlockSpec((SLOTS, GW, GW), lambda j, c: (j, 0, 0)),
            pl.BlockSpec((SLOTS, GW, GW), lambda j, c: (0, 0, 0)),
            dspec, dspec,
        ],
        out_specs=pl.BlockSpec(blk, lambda j, c: (c, 0, 0, j)),
        out_shape=jax.ShapeDtypeStruct((NCHUNK, TCH, BATCH, BRANCH), _BF),
        scratch_shapes=[
            pltpu.VMEM((SLOTS, 2 * GW, GW), _BF),
            pltpu.VMEM((PAIRS_PER_J, MIX_R, 2 * GW), _BF),
            pltpu.VMEM((SLOTS, MIX_R, GW), _F32),
            pltpu.VMEM((PAIRS_PER_J, MIX_R, GW), _F32),
            pltpu.VMEM((PAIRS_PER_J, MIX_R, GW), _BF),
            pltpu.VMEM((PAIRS_PER_J, 2, BATCH, LANES), _F32),
        ],
        compiler_params=pltpu.CompilerParams(
            dimension_semantics=("parallel", "arbitrary"), vmem_limit_bytes=VMEM_LIMIT),
        name="s5_mix",
    )(u, kpad, mb, mc, jnp.asarray(_CPERM, _BF), dec_re, dec_im)


def _s5_out_kernel(y_ref, gate_ref, x_ref, wglu_ref, bglu_ref, wout_ref, o_ref):
    rows = BATCH * TCH
    for c in range(IN_CT):
        y = _gelu(y_ref[c].reshape(rows, BRANCH).astype(_F32))
        glu = _dot(y.astype(_BF), wglu_ref[...]) + bglu_ref[...]
        y = y * _sigmoid(glu)
        z = (y * gate_ref[c].reshape(rows, BRANCH).astype(_F32)).astype(_BF)
        out = _dot(z, wout_ref[...]).reshape(TCH, BATCH, D_MODEL)
        o_ref[:, c * TCH:(c + 1) * TCH, :] = x_ref[:, c * TCH:(c + 1) * TCH, :] + jnp.swapaxes(out, 0, 1)


def _s5_out(y, gate, x, w_glu, b_glu, w_out):
    grid = (NCHUNK // IN_CT,)
    blk4 = (IN_CT, TCH, BATCH, BRANCH)
    xblk = pl.BlockSpec((BATCH, IN_CT * TCH, D_MODEL), lambda i: (0, i, 0))
    return pl.pallas_call(
        _s5_out_kernel,
        grid=grid,
        in_specs=[
            pl.BlockSpec(blk4, lambda i: (i, 0, 0, 0)),
            pl.BlockSpec(blk4, lambda i: (i, 0, 0, 0)),
            xblk,
            pl.BlockSpec((BRANCH, BRANCH), lambda i: (0, 0)),
            pl.BlockSpec((1, BRANCH), lambda i: (0, 0)),
            pl.BlockSpec((BRANCH, D_MODEL), lambda i: (0, 0)),
        ],
        out_specs=xblk,
        out_shape=jax.ShapeDtypeStruct((BATCH, SEQ, D_MODEL), _F32),
        compiler_params=pltpu.CompilerParams(
            dimension_semantics=("parallel",), vmem_limit_bytes=VMEM_LIMIT),
        name="s5_out",
    )(y, gate, x, w_glu, b_glu, w_out)


SGU_TB = 512
SGU_NB = SGU_TB // SGU_BLOCK


def _sgu_kernel(x_ref, g_ref, win_ref, lng_ref, lnb_ref, ws_ref, bs_ref, wout_ref, fg_ref, o_ref, mix_ref):
    x = x_ref[...]
    h = _rms(x, g_ref[...]).astype(_BF)
    v = _dot(h, win_ref[:, BRANCH:2 * BRANCH])
    u = _dot(h, win_ref[:, :BRANCH])
    gate = _dot(h, win_ref[:, 2 * BRANCH:])
    v = _gelu(v)
    mu = jnp.mean(v, axis=-1, keepdims=True)
    vc = v - mu
    var = jnp.mean(vc * vc, axis=-1, keepdims=True)
    vn = (vc * lax.rsqrt(var + LN_EPS) * lng_ref[...] + lnb_ref[...]).astype(_BF)
    for hd in range(SGU_HEADS):
        lo = hd * SGU_HEAD_DIM
        rhs = jnp.concatenate(
            [vn[b * SGU_BLOCK:(b + 1) * SGU_BLOCK, lo:lo + SGU_HEAD_DIM] for b in range(SGU_NB)], axis=1)
        res = _dot(ws_ref[hd], rhs)
        for b in range(SGU_NB):
            mix_ref[b * SGU_BLOCK:(b + 1) * SGU_BLOCK, lo:lo + SGU_HEAD_DIM] = (
                res[:, b * SGU_HEAD_DIM:(b + 1) * SGU_HEAD_DIM] + bs_ref[:, lo:lo + SGU_HEAD_DIM])
    sg = gate * _sigmoid(gate)
    y = _gelu(u) * mix_ref[...]
    z = (y * sg).astype(_BF)
    x2 = x + _dot(z, wout_ref[...])
    o_ref[...] = _rms(x2, fg_ref[...])


def _sgu_layer(x, g, w_in, ln_g, ln_b, ws, bs, w_out, final_g):
    n = BATCH * SEQ
    grid = (n // SGU_TB,)
    row = lambda i: (0, 0)
    xblk = pl.BlockSpec((SGU_TB, D_MODEL), lambda i: (i, 0))
    return pl.pallas_call(
        _sgu_kernel,
        grid=grid,
        in_specs=[
            xblk,
            pl.BlockSpec((1, D_MODEL), row),
            pl.BlockSpec((D_MODEL, 3 * BRANCH), row),
            pl.BlockSpec((1, BRANCH), row),
            pl.BlockSpec((1, BRANCH), row),
            pl.BlockSpec((SGU_HEADS, SGU_BLOCK, SGU_BLOCK), lambda i: (0, 0, 0)),
            pl.BlockSpec((SGU_BLOCK, BRANCH), row),
            pl.BlockSpec((BRANCH, D_MODEL), row),
            pl.BlockSpec((1, D_MODEL), row),
        ],
        out_specs=xblk,
        out_shape=jax.ShapeDtypeStruct((n, D_MODEL), _F32),
        scratch_shapes=[pltpu.VMEM((SGU_TB, BRANCH), _F32)],
        compiler_params=pltpu.CompilerParams(
            dimension_semantics=("parallel",), vmem_limit_bytes=VMEM_LIMIT),
        name="sgu_layer",
    )(x, g, w_in, ln_g, ln_b, ws, bs, w_out, final_g)


def kernel(x, norm_g, final_g, s5_w_in, s5_A_re, s5_A_im, s5_log_dt, s5_B_re, s5_B_im, s5_C_re, s5_C_im, s5_D, s5_w_glu, s5_b_glu, s5_w_out, sgu_w_in, sgu_ln_g, sgu_ln_b, sgu_w_s, sgu_b_s, sgu_w_out):
    kpad, mb, mc, dec_re, dec_im = _s5_operators(
        s5_A_re[0], s5_A_im[0], s5_log_dt[0], s5_B_re[0], s5_B_im[0], s5_C_re[0], s5_C_im[0], s5_D[0])
    u, gate = _s5_in(x, norm_g[0][None, :], s5_w_in[0].astype(_BF))
    y = _s5_mix(u, kpad, mb, mc, dec_re, dec_im)
    x1 = _s5_out(y, gate, x, s5_w_glu[0].astype(_BF), s5_b_glu[0][None, :], s5_w_out[0].astype(_BF))

    mask = jnp.tril(jnp.ones((SGU_BLOCK, SGU_BLOCK), dtype=bool))
    ws = jnp.where(mask[None], sgu_w_s[0], 0.0).astype(_BF)
    bs = jnp.repeat(jnp.transpose(sgu_b_s[0]), SGU_HEAD_DIM, axis=1)
    out = _sgu_layer(x1.reshape(BATCH * SEQ, D_MODEL), norm_g[1][None, :], sgu_w_in[0].astype(_BF),
                     sgu_ln_g[0][None, :], sgu_ln_b[0][None, :], ws, bs,
                     sgu_w_out[0].astype(_BF), final_g[None, :])
    return out.reshape(BATCH, SEQ, D_MODEL)
```

```python
import math

import numpy as np
import jax
import jax.numpy as jnp
from jax import lax
from jax.experimental import pallas as pl
from jax.experimental.pallas import tpu as pltpu

D_MODEL = 1024
BATCH = 16
SEQ = 2048
BRANCH = D_MODEL
S5_GROUP = 16
S5_GROUPS = BRANCH // S5_GROUP
S5_STATE = 64
SGU_BLOCK = 128
SGU_HEADS = 8
SGU_HEAD_DIM = BRANCH // SGU_HEADS
RMS_EPS = 1e-6
LN_EPS = 1e-5
RE_CLIP = -1e-4

LANES = 128
TCH = 16
NCHUNK = SEQ // TCH
SLOTS = LANES // S5_GROUP
NJ = BRANCH // LANES
GW = TCH * S5_GROUP
PAIRS_PER_J = SLOTS // 2
VMEM_LIMIT = 56 * 1024 * 1024

_HI = lax.Precision.HIGHEST
_BF = jnp.bfloat16
_F32 = jnp.float32


_GELU_C1 = math.sqrt(2.0 / math.pi)
_GELU_C2 = _GELU_C1 * 0.044715


def _gelu(x):
    inner = x * (_GELU_C1 + _GELU_C2 * (x * x))
    return (0.5 * x) * (1.0 + jnp.tanh(inner))


def _sigmoid(x):
    return jax.nn.sigmoid(x)


def _dot(a, b):
    return jnp.dot(a, b, preferred_element_type=_F32)


def _rms(x, g):
    return x * lax.rsqrt(jnp.mean(x * x, axis=-1, keepdims=True) + RMS_EPS) * g


KPAD = 2 * GW


def _slot_frame_table():
    a = np.arange(SLOTS)[:, None]
    x = np.arange(TCH)[None, :]
    half, k = x // SLOTS, x % SLOTS
    return SLOTS * half + (k - a) % SLOTS


_FRAME_OF_SLOT = _slot_frame_table()


def _s5_operators(A_re, A_im, log_dt, B_re, B_im, C_re, C_im, D, dtype=jnp.bfloat16):
    f32 = jnp.float32
    G, P, H = S5_GROUPS, S5_STATE, S5_GROUP
    lam_re = jnp.minimum(A_re.astype(f32), RE_CLIP)
    lam_im = A_im.astype(f32)
    dt = jnp.exp(log_dt.astype(f32))[:, None]
    ldr, ldi = lam_re * dt, lam_im * dt

    kk = jnp.arange(TCH + 1, dtype=f32)[:, None, None]
    mag = jnp.exp(kk * ldr)
    pw_re, pw_im = mag * jnp.cos(kk * ldi), mag * jnp.sin(kk * ldi)

    nr, ni = pw_re[1] - 1.0, pw_im[1]
    den = lam_re * lam_re + lam_im * lam_im
    cf_re = (nr * lam_re + ni * lam_im) / den
    cf_im = (ni * lam_re - nr * lam_im) / den
    bb_re = cf_re[..., None] * B_re - cf_im[..., None] * B_im
    bb_im = cf_re[..., None] * B_im + cf_im[..., None] * B_re
    c_re = jnp.swapaxes(C_re.astype(f32), 1, 2)
    c_im = jnp.swapaxes(C_im.astype(f32), 1, 2)

    lane_tau = np.arange(GW) // S5_GROUP
    sel_tau = jnp.asarray((np.arange(TCH)[:, None] == lane_tau[None, :]).astype(np.float32))
    pt_re = jnp.einsum('kgp,kl->gpl', pw_re[:TCH], sel_tau, precision=_HI)
    pt_im = jnp.einsum('kgp,kl->gpl', pw_im[:TCH], sel_tau, precision=_HI)
    expand = jnp.asarray(np.tile(np.eye(S5_GROUP, dtype=np.float32), (1, TCH)))
    ct_re = jnp.einsum('gpc,cl->gpl', c_re, expand, precision=_HI)
    ct_im = jnp.einsum('gpc,cl->gpl', c_im, expand, precision=_HI)
    lc_re = pt_re * ct_re - pt_im * ct_im
    lc_im = pt_re * ct_im + pt_im * ct_re
    taps = (jnp.einsum('gph,gpl->ghl', bb_re, lc_re, precision=_HI)
            - jnp.einsum('gph,gpl->ghl', bb_im, lc_im, precision=_HI))
    skip = D.astype(f32).reshape(G, H, 1) * jnp.eye(H, dtype=f32)
    taps = taps + jnp.pad(skip, ((0, 0), (0, 0), (0, GW - S5_GROUP)))
    kpad = jnp.pad(taps, ((0, 0), (0, 0), (GW - S5_GROUP, S5_GROUP)))

    even = (np.arange(G) % 2 == 0).astype(np.float32)[:, None, None]
    odd = 1.0 - even

    frame = jnp.asarray(_FRAME_OF_SLOT[np.arange(G) % SLOTS], f32)
    ex_in = (TCH - 1.0) - frame

    mag = jnp.exp(ex_in[:, :, None] * ldr[:, None])
    pi_re = mag * jnp.cos(ex_in[:, :, None] * ldi[:, None])
    pi_im = mag * jnp.sin(ex_in[:, :, None] * ldi[:, None])
    bt_re, bt_im = jnp.swapaxes(bb_re, 1, 2), jnp.swapaxes(bb_im, 1, 2)
    in_re = (pi_re[:, :, None] * bt_re[:, None] - pi_im[:, :, None] * bt_im[:, None]).reshape(G, GW, P)
    in_im = (pi_re[:, :, None] * bt_im[:, None] + pi_im[:, :, None] * bt_re[:, None]).reshape(G, GW, P)
    mb = jnp.concatenate([in_re * even, in_re * odd, in_im * even, in_im * odd], axis=-1)
    mb = mb.astype(dtype).reshape(G // 2, 2 * GW, GW)

    ex_lane = _FRAME_OF_SLOT[:, lane_tau] + 1
    sel_out = jnp.asarray((np.arange(TCH + 1)[None, :, None] == ex_lane[:, None, :]).astype(np.float32))
    po_re = jnp.einsum('kjap,akl->japl', pw_re.reshape(TCH + 1, NJ, SLOTS, P), sel_out,
                       precision=_HI).reshape(G, P, GW)
    po_im = jnp.einsum('kjap,akl->japl', pw_im.reshape(TCH + 1, NJ, SLOTS, P), sel_out,
                       precision=_HI).reshape(G, P, GW)
    o_re = po_re * ct_re - po_im * ct_im
    o_im = po_re * ct_im + po_im * ct_re
    mc = jnp.concatenate([o_re * even, o_re * odd, -o_im * even, -o_im * odd], axis=1)

    dec_re = pw_re[TCH].reshape(G // 2, 1, 2 * P)
    dec_im = pw_im[TCH].reshape(G // 2, 1, 2 * P)
    return kpad, mb, mc.astype(dtype), dec_re, dec_im


IN_CT = 4


def _s5_in_kernel(x_ref, g_ref, w_ref, u_ref, gate_ref):
    rows = BATCH * TCH
    for c in range(IN_CT):
        xc = jnp.swapaxes(x_ref[:, c * TCH:(c + 1) * TCH, :], 0, 1).reshape(rows, D_MODEL)
        hp = _rms(xc, g_ref[...]).astype(_BF)
        gate = _dot(hp, w_ref[:, BRANCH:])
        gate_ref[c] = (gate * _sigmoid(gate)).astype(_BF).reshape(TCH, BATCH, BRANCH)
        u = _dot(hp, w_ref[:, :BRANCH])
        u_ref[c] = u.astype(_BF).reshape(TCH, BATCH, BRANCH)


def _s5_in(x, g, w_in):
    grid = (NCHUNK // IN_CT,)
    blk4 = (IN_CT, TCH, BATCH, BRANCH)
    return pl.pallas_call(
        _s5_in_kernel,
        grid=grid,
        in_specs=[
            pl.BlockSpec((BATCH, IN_CT * TCH, D_MODEL), lambda i: (0, i, 0)),
            pl.BlockSpec((1, D_MODEL), lambda i: (0, 0)),
            pl.BlockSpec((D_MODEL, 2 * BRANCH), lambda i: (0, 0)),
        ],
        out_specs=[
            pl.BlockSpec(blk4, lambda i: (i, 0, 0, 0)),
            pl.BlockSpec(blk4, lambda i: (i, 0, 0, 0)),
        ],
        out_shape=[
            jax.ShapeDtypeStruct((NCHUNK, TCH, BATCH, BRANCH), _BF),
            jax.ShapeDtypeStruct((NCHUNK, TCH, BATCH, BRANCH), _BF),
        ],
        compiler_params=pltpu.CompilerParams(
            dimension_semantics=("parallel",), vmem_limit_bytes=VMEM_LIMIT),
        name="s5_in",
    )(x, g, w_in)


MIX_CT = 32
MIX_R = MIX_CT * BATCH
MIX_CB = 4
MIX_RT = MIX_CB * BATCH
MIX_NH = 2
MIX_HC = MIX_CT // MIX_NH
MIX_HR = MIX_HC * BATCH

def _gather_slots(v, slot):
    rolled = [v[0]] + [pltpu.roll(v[k], k * S5_GROUP, axis=1) for k in range(1, SLOTS)]
    out = []
    for a in range(SLOTS):
        acc = rolled[0]
        for k in range(1, SLOTS):
            acc = jnp.where(slot == (a + k) % SLOTS, rolled[k], acc)
        out.append(acc)
    return out


def _scatter_slots(v, slot):
    out = []
    for k in range(SLOTS):
        acc = v[0]
        for a in range(1, SLOTS):
            acc = jnp.where(slot == (a + k) % SLOTS, v[a], acc)
        out.append(acc if k == 0 else pltpu.roll(acc, (SLOTS - k) * S5_GROUP, axis=1))
    return out


def _column_perms():
    p = np.zeros((SLOTS, GW, GW), np.float32)
    for a in range(SLOTS):
        for x in range(TCH):
            f = int(_FRAME_OF_SLOT[a, x])
            for c in range(S5_GROUP):
                p[a, f * S5_GROUP + c, x * S5_GROUP + c] = 1.0
    return p


_CPERM = _column_perms()


def _s5_mix_kernel(u_ref, kpad_ref, mb_ref, mc_ref, cperm_ref, dre_ref, dim_ref, y_ref,
                   op_ref, xs_ref, ys_ref, ss_ref, hs_ref, carry_ref):
    ct = pl.program_id(1)

    @pl.when(ct == 0)
    def _():
        carry_ref[...] = jnp.zeros_like(carry_ref)
        for a in range(SLOTS):
            kp = kpad_ref[a]
            rows = []
            for x in range(TCH):
                off = (TCH - 1 - int(_FRAME_OF_SLOT[a, x])) * S5_GROUP
                rows.append(kp[:, off:off + GW])
            toep = jnp.concatenate(rows, axis=0).astype(_BF)
            op_ref[a, 0:GW, :] = _dot(toep, cperm_ref[a]).astype(_BF)
            op_ref[a, GW:2 * GW, :] = mc_ref[a]

    slot = lax.broadcasted_iota(jnp.int32, (MIX_RT, LANES), 1) // S5_GROUP
    dre = [dre_ref[q] for q in range(PAIRS_PER_J)]
    dim_ = [dim_ref[q] for q in range(PAIRS_PER_J)]

    def relayout_in(c0):
        r0 = c0 * BATCH
        for half in range(2):
            pieces = [u_ref[c0:c0 + MIX_CB, SLOTS * half + k, :, :].reshape(MIX_RT, LANES) for k in range(SLOTS)]
            groups = _gather_slots(pieces, slot)
            for a in range(SLOTS):
                lo = (a % 2) * GW + half * LANES
                xs_ref[a // 2, r0:r0 + MIX_RT, lo:lo + LANES] = groups[a]

    def relayout_out(c0):
        r0 = c0 * BATCH
        for half in range(2):
            groups = [ys_ref[a, r0:r0 + MIX_RT, half * LANES:(half + 1) * LANES].astype(_BF) for a in range(SLOTS)]
            frames = _scatter_slots(groups, slot)
            for k in range(SLOTS):
                y_ref[c0:c0 + MIX_CB, SLOTS * half + k, :, :] = frames[k].reshape(MIX_CB, BATCH, LANES)

    def recurrence(c0, state):
        for c in range(c0, c0 + MIX_HC):
            r0 = c * BATCH
            new = []
            for q in range(PAIRS_PER_J):
                hre, him = state[2 * q], state[2 * q + 1]
                hs_ref[q, r0:r0 + BATCH, 0:LANES] = hre.astype(_BF)
                hs_ref[q, r0:r0 + BATCH, LANES:2 * LANES] = him.astype(_BF)
                sre = ss_ref[q, r0:r0 + BATCH, 0:LANES]
                sim = ss_ref[q, r0:r0 + BATCH, LANES:2 * LANES]
                new.append(dre[q] * hre - dim_[q] * him + sre)
                new.append(dre[q] * him + dim_[q] * hre + sim)
            state = new
        return state

    blocks = [(h * MIX_HC, h * MIX_HC * BATCH) for h in range(MIX_NH)]
    for c0, r0 in blocks:
        for t in range(MIX_HC // MIX_CB):
            relayout_in(c0 + t * MIX_CB)
        for q in range(PAIRS_PER_J):
            ss_ref[q, r0:r0 + MIX_HR, :] = _dot(xs_ref[q, r0:r0 + MIX_HR, :], mb_ref[q])
    state = [carry_ref[q, k] for q in range(PAIRS_PER_J) for k in range(2)]
    for c0, r0 in blocks:
        state = recurrence(c0, state)
    for q in range(PAIRS_PER_J):
        carry_ref[q, 0] = state[2 * q]
        carry_ref[q, 1] = state[2 * q + 1]
    for c0, r0 in blocks:
        for a in range(SLOTS):
            q, m = a // 2, a % 2
            lhs = jnp.concatenate([xs_ref[q, r0:r0 + MIX_HR, m * GW:(m + 1) * GW], hs_ref[q, r0:r0 + MIX_HR, :]], axis=1)
            ys_ref[a, r0:r0 + MIX_HR, :] = _dot(lhs, op_ref[a])
    for c0, r0 in blocks:
        for t in range(MIX_HC // MIX_CB):
            relayout_out(c0 + t * MIX_CB)


def _s5_mix(u, kpad, mb, mc, dec_re, dec_im):
    grid = (NJ, NCHUNK // MIX_CT)
    blk = (MIX_CT, TCH, BATCH, LANES)
    dspec = pl.BlockSpec((PAIRS_PER_J, 1, LANES), lambda j, c: (j, 0, 0))
    return pl.pallas_call(
        _s5_mix_kernel,
        grid=grid,
        in_specs=[
            pl.BlockSpec(blk, lambda j, c: (c, 0, 0, j)),
            pl.BlockSpec((SLOTS, S5_GROUP, KPAD), lambda j, c: (j, 0, 0)),
            pl.BlockSpec((PAIRS_PER_J, 2 * GW, GW), lambda j, c: (j, 0, 0)),
            pl.BlockSpec((SLOTS, GW, GW), lambda j, c: (j, 0, 0)),
            pl.BlockSpec((SLOTS, GW, GW), lambda j, c: (0, 0, 0)),
            dspec, dspec,
        ],
        out_specs=pl.BlockSpec(blk, lambda j, c: (c, 0, 0, j)),
        out_shape=jax.ShapeDtypeStruct((NCHUNK, TCH, BATCH, BRANCH), _BF),
        scratch_shapes=[
            pltpu.VMEM((SLOTS, 2 * GW, GW), _BF),
            pltpu.VMEM((PAIRS_PER_J, MIX_R, 2 * GW), _BF),
            pltpu.VMEM((SLOTS, MIX_R, GW), _F32),
            pltpu.VMEM((PAIRS_PER_J, MIX_R, GW), _F32),
            pltpu.VMEM((PAIRS_PER_J, MIX_R, GW), _BF),
            pltpu.VMEM((PAIRS_PER_J, 2, BATCH, LANES), _F32),
        ],
        compiler_params=pltpu.CompilerParams(
            dimension_semantics=("parallel", "arbitrary"), vmem_limit_bytes=VMEM_LIMIT),
        name="s5_mix",
    )(u, kpad, mb, mc, jnp.asarray(_CPERM, _BF), dec_re, dec_im)


def _wavefront(stages, n):
    for t in range(len(stages) + n - 1):
        for i in range(n):
            k = t - i
            if 0 <= k < len(stages):
                stages[k](i)


def _s5_out_kernel(y_ref, gate_ref, x_ref, wglu_ref, bglu_ref, wout_ref, o_ref):
    rows = BATCH * TCH
    st = [dict() for _ in range(IN_CT)]

    def act(c):
        st[c]["y"] = _gelu(y_ref[c].reshape(rows, BRANCH).astype(_F32))

    def glu_mm(c):
        st[c]["glu"] = _dot(st[c]["y"].astype(_BF), wglu_ref[...])

    def gating(c):
        y = st[c].pop("y") * _sigmoid(st[c].pop("glu") + bglu_ref[...])
        st[c]["z"] = (y * gate_ref[c].reshape(rows, BRANCH).astype(_F32)).astype(_BF)

    def out_mm(c):
        st[c]["out"] = _dot(st[c].pop("z"), wout_ref[...])

    def residual(c):
        out = st[c].pop("out").reshape(TCH, BATCH, D_MODEL)
        o_ref[:, c * TCH:(c + 1) * TCH, :] = x_ref[:, c * TCH:(c + 1) * TCH, :] + jnp.swapaxes(out, 0, 1)

    _wavefront([act, glu_mm, gating, out_mm, residual], IN_CT)


def _s5_out(y, gate, x, w_glu, b_glu, w_out):
    grid = (NCHUNK // IN_CT,)
    blk4 = (IN_CT, TCH, BATCH, BRANCH)
    xblk = pl.BlockSpec((BATCH, IN_CT * TCH, D_MODEL), lambda i: (0, i, 0))
    return pl.pallas_call(
        _s5_out_kernel,
        grid=grid,
        in_specs=[
            pl.BlockSpec(blk4, lambda i: (i, 0, 0, 0)),
            pl.BlockSpec(blk4, lambda i: (i, 0, 0, 0)),
            xblk,
            pl.BlockSpec((BRANCH, BRANCH), lambda i: (0, 0)),
            pl.BlockSpec((1, BRANCH), lambda i: (0, 0)),
            pl.BlockSpec((BRANCH, D_MODEL), lambda i: (0, 0)),
        ],
        out_specs=xblk,
        out_shape=jax.ShapeDtypeStruct((BATCH, SEQ, D_MODEL), _F32),
        compiler_params=pltpu.CompilerParams(
            dimension_semantics=("parallel",), vmem_limit_bytes=VMEM_LIMIT),
        name="s5_out",
    )(y, gate, x, w_glu, b_glu, w_out)


SGU_TB = 1024
SGU_NB = SGU_TB // SGU_BLOCK


SGU_RB = 512
SGU_BPB = SGU_RB // SGU_BLOCK


def _sgu_kernel(x_ref, g_ref, win_ref, lng_ref, lnb_ref, ws_ref, bs_ref, wout_ref, fg_ref, o_ref, mix_ref):
    blocks = [slice(r0, r0 + SGU_RB) for r0 in range(0, SGU_TB, SGU_RB)]
    st = [dict() for _ in blocks]

    def norm_in(i):
        st[i]["h"] = _rms(x_ref[blocks[i], :], g_ref[...]).astype(_BF)

    def proj_v(i):
        st[i]["v"] = _dot(st[i]["h"], win_ref[:, BRANCH:2 * BRANCH])

    def proj_ug(i):
        st[i]["u"] = _dot(st[i]["h"], win_ref[:, :BRANCH])
        st[i]["gate"] = _dot(st[i]["h"], win_ref[:, 2 * BRANCH:])

    def layer_norm(i):
        v = _gelu(st[i].pop("v"))
        mu = jnp.mean(v, axis=-1, keepdims=True)
        vc = v - mu
        var = jnp.mean(vc * vc, axis=-1, keepdims=True)
        st[i]["vn"] = (vc * lax.rsqrt(var + LN_EPS) * lng_ref[...] + lnb_ref[...]).astype(_BF)

    def spatial(i):
        vn = st[i].pop("vn")
        r0 = blocks[i].start
        for hd in range(SGU_HEADS):
            lo = hd * SGU_HEAD_DIM
            rhs = jnp.concatenate(
                [vn[b * SGU_BLOCK:(b + 1) * SGU_BLOCK, lo:lo + SGU_HEAD_DIM] for b in range(SGU_BPB)], axis=1)
            res = _dot(ws_ref[hd], rhs)
            for b in range(SGU_BPB):
                mix_ref[r0 + b * SGU_BLOCK:r0 + (b + 1) * SGU_BLOCK, lo:lo + SGU_HEAD_DIM] = (
                    res[:, b * SGU_HEAD_DIM:(b + 1) * SGU_HEAD_DIM] + bs_ref[:, lo:lo + SGU_HEAD_DIM])

    def gating(i):
        gate = st[i].pop("gate")
        y = _gelu(st[i].pop("u")) * mix_ref[blocks[i], :]
        st[i]["z"] = (y * (gate * _sigmoid(gate))).astype(_BF)

    def proj_out(i):
        st[i]["out"] = _dot(st[i].pop("z"), wout_ref[...])

    def norm_out(i):
        x2 = x_ref[blocks[i], :] + st[i].pop("out")
        o_ref[blocks[i], :] = _rms(x2, fg_ref[...])

    _wavefront([norm_in, proj_v, layer_norm, proj_ug, spatial, gating, proj_out, norm_out], len(blocks))


def _sgu_layer(x, g, w_in, ln_g, ln_b, ws, bs, w_out, final_g):
    n = BATCH * SEQ
    grid = (n // SGU_TB,)
    row = lambda i: (0, 0)
    xblk = pl.BlockSpec((SGU_TB, D_MODEL), lambda i: (i, 0))
    return pl.pallas_call(
        _sgu_kernel,
        grid=grid,
        in_specs=[
            xblk,
            pl.BlockSpec((1, D_MODEL), row),
            pl.BlockSpec((D_MODEL, 3 * BRANCH), row),
            pl.BlockSpec((1, BRANCH), row),
            pl.BlockSpec((1, BRANCH), row),
            pl.BlockSpec((SGU_HEADS, SGU_BLOCK, SGU_BLOCK), lambda i: (0, 0, 0)),
            pl.BlockSpec((SGU_BLOCK, BRANCH), row),
            pl.BlockSpec((BRANCH, D_MODEL), row),
            pl.BlockSpec((1, D_MODEL), row),
        ],
        out_specs=xblk,
        out_shape=jax.ShapeDtypeStruct((n, D_MODEL), _F32),
        scratch_shapes=[pltpu.VMEM((SGU_TB, BRANCH), _F32)],
        compiler_params=pltpu.CompilerParams(
            dimension_semantics=("parallel",), vmem_limit_bytes=VMEM_LIMIT),
        name="sgu_layer",
    )(x, g, w_in, ln_g, ln_b, ws, bs, w_out, final_g)


def kernel(x, norm_g, final_g, s5_w_in, s5_A_re, s5_A_im, s5_log_dt, s5_B_re, s5_B_im, s5_C_re, s5_C_im, s5_D, s5_w_glu, s5_b_glu, s5_w_out, sgu_w_in, sgu_ln_g, sgu_ln_b, sgu_w_s, sgu_b_s, sgu_w_out):
    kpad, mb, mc, dec_re, dec_im = _s5_operators(
        s5_A_re[0], s5_A_im[0], s5_log_dt[0], s5_B_re[0], s5_B_im[0], s5_C_re[0], s5_C_im[0], s5_D[0])
    u, gate = _s5_in(x, norm_g[0][None, :], s5_w_in[0].astype(_BF))
    y = _s5_mix(u, kpad, mb, mc, dec_re, dec_im)
    x1 = _s5_out(y, gate, x, s5_w_glu[0].astype(_BF), s5_b_glu[0][None, :], s5_w_out[0].astype(_BF))

    mask = jnp.tril(jnp.ones((SGU_BLOCK, SGU_BLOCK), dtype=bool))
    ws = jnp.where(mask[None], sgu_w_s[0], 0.0).astype(_BF)
    bs = jnp.repeat(jnp.transpose(sgu_b_s[0]), SGU_HEAD_DIM, axis=1)
    out = _sgu_layer(x1.reshape(BATCH * SEQ, D_MODEL), norm_g[1][None, :], sgu_w_in[0].astype(_BF),
                     sgu_ln_g[0][None, :], sgu_ln_b[0][None, :], ws, bs,
                     sgu_w_out[0].astype(_BF), final_g[None, :])
    return out.reshape(BATCH, SEQ, D_MODEL)
```

```python
import math

import numpy as np
import jax
import jax.numpy as jnp
from jax import lax
from jax.experimental import pallas as pl
from jax.experimental.pallas import tpu as pltpu

D_MODEL = 1024
BATCH = 16
SEQ = 2048
BRANCH = D_MODEL
S5_GROUP = 16
S5_GROUPS = BRANCH // S5_GROUP
S5_STATE = 64
SGU_BLOCK = 128
SGU_HEADS = 8
SGU_HEAD_DIM = BRANCH // SGU_HEADS
RMS_EPS = 1e-6
LN_EPS = 1e-5
RE_CLIP = -1e-4

LANES = 128
TCH = 16
NCHUNK = SEQ // TCH
SLOTS = LANES // S5_GROUP
NJ = BRANCH // LANES
GW = TCH * S5_GROUP
PAIRS_PER_J = SLOTS // 2
VMEM_LIMIT = 56 * 1024 * 1024

_HI = lax.Precision.HIGHEST
_BF = jnp.bfloat16
_F32 = jnp.float32


_GELU_C1 = math.sqrt(2.0 / math.pi)
_GELU_C2 = _GELU_C1 * 0.044715


def _gelu(x):
    inner = x * (_GELU_C1 + _GELU_C2 * (x * x))
    return (0.5 * x) * (1.0 + jnp.tanh(inner))


def _sigmoid(x):
    return jax.nn.sigmoid(x)


def _dot(a, b):
    return jnp.dot(a, b, preferred_element_type=_F32)


def _rms(x, g):
    return x * lax.rsqrt(jnp.mean(x * x, axis=-1, keepdims=True) + RMS_EPS) * g


KPAD = 2 * GW


def _slot_frame_table():
    a = np.arange(SLOTS)[:, None]
    x = np.arange(TCH)[None, :]
    half, k = x // SLOTS, x % SLOTS
    return SLOTS * half + (k - a) % SLOTS


_FRAME_OF_SLOT = _slot_frame_table()


def _s5_operators(A_re, A_im, log_dt, B_re, B_im, C_re, C_im, D, dtype=jnp.bfloat16):
    f32 = jnp.float32
    G, P, H = S5_GROUPS, S5_STATE, S5_GROUP
    lam_re = jnp.minimum(A_re.astype(f32), RE_CLIP)
    lam_im = A_im.astype(f32)
    dt = jnp.exp(log_dt.astype(f32))[:, None]
    ldr, ldi = lam_re * dt, lam_im * dt

    def power(k):
        mag = jnp.exp(k[..., None] * ldr[:, None])
        return mag * jnp.cos(k[..., None] * ldi[:, None]), mag * jnp.sin(k[..., None] * ldi[:, None])

    frame = np.asarray(_FRAME_OF_SLOT[np.arange(G) % SLOTS], np.float32)
    pw_re, pw_im = power(jnp.broadcast_to(jnp.arange(TCH + 1, dtype=f32), (G, TCH + 1)))
    pi_re, pi_im = power(jnp.asarray(TCH - 1.0 - frame))
    po_re, po_im = power(jnp.asarray(frame + 1.0))

    nr, ni = pw_re[:, 1] - 1.0, pw_im[:, 1]
    den = lam_re * lam_re + lam_im * lam_im
    cf_re = ((nr * lam_re + ni * lam_im) / den)[:, None]
    cf_im = ((ni * lam_re - nr * lam_im) / den)[:, None]
    b_re, b_im = jnp.swapaxes(B_re.astype(f32), 1, 2), jnp.swapaxes(B_im.astype(f32), 1, 2)
    bb_re = cf_re * b_re - cf_im * b_im
    bb_im = cf_re * b_im + cf_im * b_re
    c_re, c_im = C_re.astype(f32), C_im.astype(f32)

    def outer(t_re, t_im, m_re, m_im):
        re = t_re[:, :, None] * m_re[:, None] - t_im[:, :, None] * m_im[:, None]
        im = t_re[:, :, None] * m_im[:, None] + t_im[:, :, None] * m_re[:, None]
        return re.reshape(G, GW, P), im.reshape(G, GW, P)

    lc_re, lc_im = outer(pw_re[:, :TCH], pw_im[:, :TCH], c_re, c_im)
    taps = (jnp.einsum('ghp,glp->ghl', bb_re, lc_re, precision=_HI)
            - jnp.einsum('ghp,glp->ghl', bb_im, lc_im, precision=_HI))
    skip = D.astype(f32).reshape(G, H, 1) * jnp.eye(H, dtype=f32)
    taps = taps + jnp.pad(skip, ((0, 0), (0, 0), (0, GW - S5_GROUP)))
    kpad = jnp.pad(taps, ((0, 0), (0, 0), (GW - S5_GROUP, S5_GROUP)))

    even = (np.arange(G) % 2 == 0).astype(np.float32)[:, None, None]
    odd = 1.0 - even

    in_re, in_im = outer(pi_re, pi_im, bb_re, bb_im)
    mb = jnp.concatenate([in_re * even, in_re * odd, in_im * even, in_im * odd], axis=-1)
    mb = mb.astype(dtype).reshape(G // 2, 2 * GW, GW)

    o_re, o_im = outer(po_re, po_im, c_re, c_im)
    o_re, o_im = jnp.swapaxes(o_re, 1, 2), jnp.swapaxes(o_im, 1, 2)
    mc = jnp.concatenate([o_re * even, o_re * odd, -o_im * even, -o_im * odd], axis=1)

    dec_re = pw_re[:, TCH].reshape(G // 2, 1, 2 * P)
    dec_im = pw_im[:, TCH].reshape(G // 2, 1, 2 * P)
    return kpad, mb, mc.astype(dtype), dec_re, dec_im


IN_CT = 4


def _s5_in_kernel(x_ref, g_ref, w_ref, u_ref, gate_ref):
    rows = BATCH * TCH
    for c in range(IN_CT):
        xc = jnp.swapaxes(x_ref[:, c * TCH:(c + 1) * TCH, :], 0, 1).reshape(rows, D_MODEL)
        hp = _rms(xc, g_ref[...]).astype(_BF)
        gate = _dot(hp, w_ref[:, BRANCH:])
        gate_ref[c] = (gate * _sigmoid(gate)).astype(_BF).reshape(TCH, BATCH, BRANCH)
        u = _dot(hp, w_ref[:, :BRANCH])
        u_ref[c] = u.astype(_BF).reshape(TCH, BATCH, BRANCH)


def _s5_in(x, g, w_in):
    grid = (NCHUNK // IN_CT,)
    blk4 = (IN_CT, TCH, BATCH, BRANCH)
    return pl.pallas_call(
        _s5_in_kernel,
        grid=grid,
        in_specs=[
            pl.BlockSpec((BATCH, IN_CT * TCH, D_MODEL), lambda i: (0, i, 0)),
            pl.BlockSpec((1, D_MODEL), lambda i: (0, 0)),
            pl.BlockSpec((D_MODEL, 2 * BRANCH), lambda i: (0, 0)),
        ],
        out_specs=[
            pl.BlockSpec(blk4, lambda i: (i, 0, 0, 0)),
            pl.BlockSpec(blk4, lambda i: (i, 0, 0, 0)),
        ],
        out_shape=[
            jax.ShapeDtypeStruct((NCHUNK, TCH, BATCH, BRANCH), _BF),
            jax.ShapeDtypeStruct((NCHUNK, TCH, BATCH, BRANCH), _BF),
        ],
        compiler_params=pltpu.CompilerParams(
            dimension_semantics=("parallel",), vmem_limit_bytes=VMEM_LIMIT),
        name="s5_in",
    )(x, g, w_in)


MIX_CT = 64
MIX_R = MIX_CT * BATCH
MIX_CB = 4
MIX_RT = MIX_CB * BATCH
MIX_NH = 4
MIX_HC = MIX_CT // MIX_NH
MIX_HR = MIX_HC * BATCH

def _gather_slots(v, slot):
    rolled = [v[0]] + [pltpu.roll(v[k], k * S5_GROUP, axis=1) for k in range(1, SLOTS)]
    out = []
    for a in range(SLOTS):
        acc = rolled[0]
        for k in range(1, SLOTS):
            acc = jnp.where(slot == (a + k) % SLOTS, rolled[k], acc)
        out.append(acc)
    return out


def _scatter_slots(v, slot):
    out = []
    for k in range(SLOTS):
        acc = v[0]
        for a in range(1, SLOTS):
            acc = jnp.where(slot == (a + k) % SLOTS, v[a], acc)
        out.append(acc if k == 0 else pltpu.roll(acc, (SLOTS - k) * S5_GROUP, axis=1))
    return out


def _column_perms():
    p = np.zeros((SLOTS, GW, GW), np.float32)
    for a in range(SLOTS):
        for x in range(TCH):
            f = int(_FRAME_OF_SLOT[a, x])
            for c in range(S5_GROUP):
                p[a, f * S5_GROUP + c, x * S5_GROUP + c] = 1.0
    return p


_CPERM = _column_perms()


def _s5_mix_kernel(u_ref, kpad_ref, mb_ref, mc_ref, cperm_ref, dre_ref, dim_ref, y_ref,
                   op_ref, xs_ref, ys_ref, ss_ref, hs_ref, carry_ref):
    ct = pl.program_id(1)

    @pl.when(ct == 0)
    def _():
        carry_ref[...] = jnp.zeros_like(carry_ref)
        for a in range(SLOTS):
            kp = kpad_ref[a]
            rows = []
            for x in range(TCH):
                off = (TCH - 1 - int(_FRAME_OF_SLOT[a, x])) * S5_GROUP
                rows.append(kp[:, off:off + GW])
            toep = jnp.concatenate(rows, axis=0).astype(_BF)
            op_ref[a, 0:GW, :] = _dot(toep, cperm_ref[a]).astype(_BF)
            op_ref[a, GW:2 * GW, :] = mc_ref[a]

    slot = lax.broadcasted_iota(jnp.int32, (MIX_RT, LANES), 1) // S5_GROUP
    dre = [dre_ref[q] for q in range(PAIRS_PER_J)]
    dim_ = [dim_ref[q] for q in range(PAIRS_PER_J)]

    def relayout_in(c0):
        r0 = c0 * BATCH
        for half in range(2):
            pieces = [u_ref[c0:c0 + MIX_CB, SLOTS * half + k, :, :].reshape(MIX_RT, LANES) for k in range(SLOTS)]
            groups = _gather_slots(pieces, slot)
            for a in range(SLOTS):
                lo = (a % 2) * GW + half * LANES
                xs_ref[a // 2, r0:r0 + MIX_RT, lo:lo + LANES] = groups[a]

    def relayout_out(c0):
        r0 = c0 * BATCH
        for half in range(2):
            groups = [ys_ref[a, r0:r0 + MIX_RT, half * LANES:(half + 1) * LANES].astype(_BF) for a in range(SLOTS)]
            frames = _scatter_slots(groups, slot)
            for k in range(SLOTS):
                y_ref[c0:c0 + MIX_CB, SLOTS * half + k, :, :] = frames[k].reshape(MIX_CB, BATCH, LANES)

    def recurrence(c0, state):
        for c in range(c0, c0 + MIX_HC):
            r0 = c * BATCH
            new = []
            for q in range(PAIRS_PER_J):
                hre, him = state[2 * q], state[2 * q + 1]
                hs_ref[q, r0:r0 + BATCH, 0:LANES] = hre.astype(_BF)
                hs_ref[q, r0:r0 + BATCH, LANES:2 * LANES] = him.astype(_BF)
                sre = ss_ref[q, r0:r0 + BATCH, 0:LANES]
                sim = ss_ref[q, r0:r0 + BATCH, LANES:2 * LANES]
                new.append(dre[q] * hre - dim_[q] * him + sre)
                new.append(dre[q] * him + dim_[q] * hre + sim)
            state = new
        return state

    blocks = [(h * MIX_HC, h * MIX_HC * BATCH) for h in range(MIX_NH)]
    for c0, r0 in blocks:
        for t in range(MIX_HC // MIX_CB):
            relayout_in(c0 + t * MIX_CB)
        for q in range(PAIRS_PER_J):
            ss_ref[q, r0:r0 + MIX_HR, :] = _dot(xs_ref[q, r0:r0 + MIX_HR, :], mb_ref[q])
    state = [carry_ref[q, k] for q in range(PAIRS_PER_J) for k in range(2)]
    for c0, r0 in blocks:
        state = recurrence(c0, state)
    for q in range(PAIRS_PER_J):
        carry_ref[q, 0] = state[2 * q]
        carry_ref[q, 1] = state[2 * q + 1]
    for c0, r0 in blocks:
        for a in range(SLOTS):
            q, m = a // 2, a % 2
            lhs = jnp.concatenate([xs_ref[q, r0:r0 + MIX_HR, m * GW:(m + 1) * GW], hs_ref[q, r0:r0 + MIX_HR, :]], axis=1)
            ys_ref[a, r0:r0 + MIX_HR, :] = _dot(lhs, op_ref[a])
    for c0, r0 in blocks:
        for t in range(MIX_HC // MIX_CB):
            relayout_out(c0 + t * MIX_CB)


def _s5_mix(u, kpad, mb, mc, dec_re, dec_im):
    grid = (NJ, NCHUNK // MIX_CT)
    blk = (MIX_CT, TCH, BATCH, LANES)
    dspec = pl.BlockSpec((PAIRS_PER_J, 1, LANES), lambda j, c: (j, 0, 0))
    return pl.pallas_call(
        _s5_mix_kernel,
        grid=grid,
        in_specs=[
            pl.BlockSpec(blk, lambda j, c: (c, 0, 0, j)),
            pl.BlockSpec((SLOTS, S5_GROUP, KPAD), lambda j, c: (j, 0, 0)),
            pl.BlockSpec((PAIRS_PER_J, 2 * GW, GW), lambda j, c: (j, 0, 0)),
            pl.BlockSpec((SLOTS, GW, GW), lambda j, c: (j, 0, 0)),
            pl.BlockSpec((SLOTS, GW, GW), lambda j, c: (0, 0, 0)),
            dspec, dspec,
        ],
        out_specs=pl.BlockSpec(blk, lambda j, c: (c, 0, 0, j)),
        out_shape=jax.ShapeDtypeStruct((NCHUNK, TCH, BATCH, BRANCH), _BF),
        scratch_shapes=[
            pltpu.VMEM((SLOTS, 2 * GW, GW), _BF),
            pltpu.VMEM((PAIRS_PER_J, MIX_R, 2 * GW), _BF),
            pltpu.VMEM((SLOTS, MIX_R, GW), _F32),
            pltpu.VMEM((PAIRS_PER_J, MIX_R, GW), _F32),
            pltpu.VMEM((PAIRS_PER_J, MIX_R, GW), _BF),
            pltpu.VMEM((PAIRS_PER_J, 2, BATCH, LANES), _F32),
        ],
        compiler_params=pltpu.CompilerParams(
            dimension_semantics=("parallel", "arbitrary"), vmem_limit_bytes=VMEM_LIMIT),
        name="s5_mix",
    )(u, kpad, mb, mc, jnp.asarray(_CPERM, _BF), dec_re, dec_im)


def _wavefront(stages, n):
    for t in range(len(stages) + n - 1):
        for i in range(n):
            k = t - i
            if 0 <= k < len(stages):
                stages[k](i)


def _s5_out_kernel(y_ref, gate_ref, x_ref, wglu_ref, bglu_ref, wout_ref, o_ref):
    rows = BATCH * TCH
    st = [dict() for _ in range(IN_CT)]

    def act(c):
        st[c]["y"] = _gelu(y_ref[c].reshape(rows, BRANCH).astype(_F32))

    def glu_mm(c):
        st[c]["glu"] = _dot(st[c]["y"].astype(_BF), wglu_ref[...])

    def gating(c):
        y = st[c].pop("y") * _sigmoid(st[c].pop("glu") + bglu_ref[...])
        st[c]["z"] = (y * gate_ref[c].reshape(rows, BRANCH).astype(_F32)).astype(_BF)

    def out_mm(c):
        st[c]["out"] = _dot(st[c].pop("z"), wout_ref[...])

    def residual(c):
        out = st[c].pop("out").reshape(TCH, BATCH, D_MODEL)
        o_ref[:, c * TCH:(c + 1) * TCH, :] = x_ref[:, c * TCH:(c + 1) * TCH, :] + jnp.swapaxes(out, 0, 1)

    _wavefront([act, glu_mm, gating, out_mm, residual], IN_CT)


def _s5_out(y, gate, x, w_glu, b_glu, w_out):
    grid = (NCHUNK // IN_CT,)
    blk4 = (IN_CT, TCH, BATCH, BRANCH)
    xblk = pl.BlockSpec((BATCH, IN_CT * TCH, D_MODEL), lambda i: (0, i, 0))
    return pl.pallas_call(
        _s5_out_kernel,
        grid=grid,
        in_specs=[
            pl.BlockSpec(blk4, lambda i: (i, 0, 0, 0)),
            pl.BlockSpec(blk4, lambda i: (i, 0, 0, 0)),
            xblk,
            pl.BlockSpec((BRANCH, BRANCH), lambda i: (0, 0)),
            pl.BlockSpec((1, BRANCH), lambda i: (0, 0)),
            pl.BlockSpec((BRANCH, D_MODEL), lambda i: (0, 0)),
        ],
        out_specs=xblk,
        out_shape=jax.ShapeDtypeStruct((BATCH, SEQ, D_MODEL), _F32),
        compiler_params=pltpu.CompilerParams(
            dimension_semantics=("parallel",), vmem_limit_bytes=VMEM_LIMIT),
        name="s5_out",
    )(y, gate, x, w_glu, b_glu, w_out)


SGU_TB = 1024
SGU_NB = SGU_TB // SGU_BLOCK


SGU_RB = 512
SGU_BPB = SGU_RB // SGU_BLOCK


def _sgu_kernel(x_ref, g_ref, win_ref, lng_ref, lnb_ref, ws_ref, bs_ref, wout_ref, fg_ref, o_ref, mix_ref):
    blocks = [slice(r0, r0 + SGU_RB) for r0 in range(0, SGU_TB, SGU_RB)]
    st = [dict() for _ in blocks]

    def norm_in(i):
        st[i]["h"] = _rms(x_ref[blocks[i], :], g_ref[...]).astype(_BF)

    def proj_v(i):
        st[i]["v"] = _dot(st[i]["h"], win_ref[:, BRANCH:2 * BRANCH])

    def proj_ug(i):
        st[i]["u"] = _dot(st[i]["h"], win_ref[:, :BRANCH])
        st[i]["gate"] = _dot(st[i]["h"], win_ref[:, 2 * BRANCH:])

    def layer_norm(i):
        v = _gelu(st[i].pop("v"))
        mu = jnp.mean(v, axis=-1, keepdims=True)
        vc = v - mu
        var = jnp.mean(vc * vc, axis=-1, keepdims=True)
        st[i]["vn"] = (vc * lax.rsqrt(var + LN_EPS) * lng_ref[...] + lnb_ref[...]).astype(_BF)

    def spatial(i):
        vn = st[i].pop("vn")
        r0 = blocks[i].start
        for hd in range(SGU_HEADS):
            lo = hd * SGU_HEAD_DIM
            rhs = jnp.concatenate(
                [vn[b * SGU_BLOCK:(b + 1) * SGU_BLOCK, lo:lo + SGU_HEAD_DIM] for b in range(SGU_BPB)], axis=1)
            res = _dot(ws_ref[hd], rhs)
            for b in range(SGU_BPB):
                mix_ref[r0 + b * SGU_BLOCK:r0 + (b + 1) * SGU_BLOCK, lo:lo + SGU_HEAD_DIM] = (
                    res[:, b * SGU_HEAD_DIM:(b + 1) * SGU_HEAD_DIM] + bs_ref[:, lo:lo + SGU_HEAD_DIM])

    def gating(i):
        gate = st[i].pop("gate")
        y = _gelu(st[i].pop("u")) * mix_ref[blocks[i], :]
        st[i]["z"] = (y * (gate * _sigmoid(gate))).astype(_BF)

    def proj_out(i):
        st[i]["out"] = _dot(st[i].pop("z"), wout_ref[...])

    def norm_out(i):
        x2 = x_ref[blocks[i], :] + st[i].pop("out")
        o_ref[blocks[i], :] = _rms(x2, fg_ref[...])

    _wavefront([norm_in, proj_v, layer_norm, proj_ug, spatial, gating, proj_out, norm_out], len(blocks))


def _sgu_layer(x, g, w_in, ln_g, ln_b, ws, bs, w_out, final_g):
    n = BATCH * SEQ
    grid = (n // SGU_TB,)
    row = lambda i: (0, 0)
    xblk = pl.BlockSpec((SGU_TB, D_MODEL), lambda i: (i, 0))
    return pl.pallas_call(
        _sgu_kernel,
        grid=grid,
        in_specs=[
            xblk,
            pl.BlockSpec((1, D_MODEL), row),
            pl.BlockSpec((D_MODEL, 3 * BRANCH), row),
            pl.BlockSpec((1, BRANCH), row),
            pl.BlockSpec((1, BRANCH), row),
            pl.BlockSpec((SGU_HEADS, SGU_BLOCK, SGU_BLOCK), lambda i: (0, 0, 0)),
            pl.BlockSpec((SGU_BLOCK, BRANCH), row),
            pl.BlockSpec((BRANCH, D_MODEL), row),
            pl.BlockSpec((1, D_MODEL), row),
        ],
        out_specs=xblk,
        out_shape=jax.ShapeDtypeStruct((n, D_MODEL), _F32),
        scratch_shapes=[pltpu.VMEM((SGU_TB, BRANCH), _F32)],
        compiler_params=pltpu.CompilerParams(
            dimension_semantics=("parallel",), vmem_limit_bytes=VMEM_LIMIT),
        name="sgu_layer",
    )(x, g, w_in, ln_g, ln_b, ws, bs, w_out, final_g)


def kernel(x, norm_g, final_g, s5_w_in, s5_A_re, s5_A_im, s5_log_dt, s5_B_re, s5_B_im, s5_C_re, s5_C_im, s5_D, s5_w_glu, s5_b_glu, s5_w_out, sgu_w_in, sgu_ln_g, sgu_ln_b, sgu_w_s, sgu_b_s, sgu_w_out):
    kpad, mb, mc, dec_re, dec_im = _s5_operators(
        s5_A_re[0], s5_A_im[0], s5_log_dt[0], s5_B_re[0], s5_B_im[0], s5_C_re[0], s5_C_im[0], s5_D[0])
    u, gate = _s5_in(x, norm_g[0][None, :], s5_w_in[0].astype(_BF))
    y = _s5_mix(u, kpad, mb, mc, dec_re, dec_im)
    x1 = _s5_out(y, gate, x, s5_w_glu[0].astype(_BF), s5_b_glu[0][None, :], s5_w_out[0].astype(_BF))

    mask = jnp.tril(jnp.ones((SGU_BLOCK, SGU_BLOCK), dtype=bool))
    ws = jnp.where(mask[None], sgu_w_s[0], 0.0).astype(_BF)
    bs = jnp.repeat(jnp.transpose(sgu_b_s[0]), SGU_HEAD_DIM, axis=1)
    out = _sgu_layer(x1.reshape(BATCH * SEQ, D_MODEL), norm_g[1][None, :], sgu_w_in[0].astype(_BF),
                     sgu_ln_g[0][None, :], sgu_ln_b[0][None, :], ws, bs,
                     sgu_w_out[0].astype(_BF), final_g[None, :])
    return out.reshape(BATCH, SEQ, D_MODEL)
```

```python
import math

import numpy as np
import jax
import jax.numpy as jnp
from jax import lax
from jax.experimental import pallas as pl
from jax.experimental.pallas import tpu as pltpu

D_MODEL = 1024
BATCH = 16
SEQ = 2048
BRANCH = D_MODEL
S5_GROUP = 16
S5_GROUPS = BRANCH // S5_GROUP
S5_STATE = 64
SGU_BLOCK = 128
SGU_HEADS = 8
SGU_HEAD_DIM = BRANCH // SGU_HEADS
RMS_EPS = 1e-6
LN_EPS = 1e-5
RE_CLIP = -1e-4

LANES = 128
TCH = 16
NCHUNK = SEQ // TCH
SLOTS = LANES // S5_GROUP
NJ = BRANCH // LANES
GW = TCH * S5_GROUP
PAIRS_PER_J = SLOTS // 2
VMEM_LIMIT = 56 * 1024 * 1024

_HI = lax.Precision.HIGHEST
_BF = jnp.bfloat16
_F32 = jnp.float32


_GELU_C1 = math.sqrt(2.0 / math.pi)
_GELU_C2 = _GELU_C1 * 0.044715


def _gelu(x):
    inner = x * (_GELU_C1 + _GELU_C2 * (x * x))
    return (0.5 * x) * (1.0 + jnp.tanh(inner))


def _sigmoid(x):
    return jax.nn.sigmoid(x)


def _dot(a, b):
    return jnp.dot(a, b, preferred_element_type=_F32)


def _rms(x, g):
    return x * lax.rsqrt(jnp.mean(x * x, axis=-1, keepdims=True) + RMS_EPS) * g


KPAD = 2 * GW


def _slot_frame_table():
    a = np.arange(SLOTS)[:, None]
    x = np.arange(TCH)[None, :]
    half, k = x // SLOTS, x % SLOTS
    return SLOTS * half + (k - a) % SLOTS


_FRAME_OF_SLOT = _slot_frame_table()


def _s5_ops_kernel(lre_ref, lim_ref, ldr_ref, ldi_ref, btr_ref, bti_ref, cr_ref, ci_ref, d_ref,
                   kpad_ref, mb_ref, mc_ref, dre_ref, dim_ref):
    P = S5_STATE
    lane = lax.broadcasted_iota(jnp.int32, (S5_GROUP, LANES), 1)
    kcol = lax.broadcasted_iota(jnp.int32, (TCH, 1), 0)
    lane256 = lax.broadcasted_iota(jnp.int32, (S5_GROUP, GW), 1)
    row256 = lax.broadcasted_iota(jnp.int32, (S5_GROUP, GW), 0)
    lane_gw = lax.broadcasted_iota(jnp.int32, (GW, LANES), 1)
    pick = (lax.broadcasted_iota(jnp.int32, (P, LANES), 1)
            == lax.broadcasted_iota(jnp.int32, (P, LANES), 0)).astype(_F32)
    nt = (((1,), (1,)), ((), ()))
    zeros64 = jnp.zeros((P, GW), _F32)
    dec_prev = None
    for a in range(SLOTS):
        lam_re, lam_im = lre_ref[a:a + 1, :], lim_ref[a:a + 1, :]
        ldr, ldi = ldr_ref[a:a + 1, :], ldi_ref[a:a + 1, :]

        def power(k):
            mag = jnp.exp(k * ldr)
            return mag * jnp.cos(k * ldi), mag * jnp.sin(k * ldi)

        def outer(t_re, t_im, m_re, m_im):
            re = [t_re[k:k + 1] * m_re - t_im[k:k + 1] * m_im for k in range(TCH)]
            im = [t_re[k:k + 1] * m_im + t_im[k:k + 1] * m_re for k in range(TCH)]
            return jnp.concatenate(re, axis=0), jnp.concatenate(im, axis=0)

        frame = (kcol & SLOTS) + (((kcol & (SLOTS - 1)) + (SLOTS - a)) & (SLOTS - 1))
        pw_re, pw_im = power(kcol.astype(_F32))
        pi_re, pi_im = power((TCH - 1 - frame).astype(_F32))
        po_re, po_im = power((frame + 1).astype(_F32))
        p16_re, p16_im = power(jnp.full((1, 1), float(TCH), _F32))

        nr, ni = pw_re[1:2] - 1.0, pw_im[1:2]
        den = lam_re * lam_re + lam_im * lam_im
        cf_re = (nr * lam_re + ni * lam_im) / den
        cf_im = (ni * lam_re - nr * lam_im) / den
        b_re, b_im = btr_ref[a], bti_ref[a]
        bb_re = cf_re * b_re - cf_im * b_im
        bb_im = cf_re * b_im + cf_im * b_re
        c_re, c_im = cr_ref[a], ci_ref[a]

        lc_re, lc_im = outer(pw_re, pw_im, c_re, c_im)
        once = lane < P
        taps = (lax.dot_general(jnp.where(once, bb_re, 0.0), lc_re, nt, precision=_HI, preferred_element_type=_F32)
                - lax.dot_general(jnp.where(once, bb_im, 0.0), lc_im, nt, precision=_HI, preferred_element_type=_F32))
        taps = taps + jnp.where(lane256 == row256, d_ref[a], 0.0)
        kpad_ref[a, :, 0:GW] = jnp.zeros((S5_GROUP, GW), _F32)
        kpad_ref[a, :, GW:2 * GW] = taps

        q, m = a // 2, a % 2
        mine = (lane_gw >= P) if m else (lane_gw < P)

        in_re, in_im = outer(pi_re, pi_im, bb_re, bb_im)
        mb_ref[q, m * GW:(m + 1) * GW, 0:LANES] = jnp.where(mine, in_re, 0.0).astype(_BF)
        mb_ref[q, m * GW:(m + 1) * GW, LANES:2 * LANES] = jnp.where(mine, in_im, 0.0).astype(_BF)

        ot_re, ot_im = outer(po_re, po_im, c_re, c_im)
        o_re = lax.dot_general(pick, ot_re, nt, precision=_HI, preferred_element_type=_F32)
        o_im = lax.dot_general(pick, ot_im, nt, precision=_HI, preferred_element_type=_F32)
        blocks = [zeros64, zeros64, zeros64, zeros64]
        blocks[m], blocks[2 + m] = o_re, -o_im
        mc_ref[a] = jnp.concatenate(blocks, axis=0).astype(_BF)

        if m == 0:
            dec_prev = (p16_re, p16_im)
        else:
            first = lax.broadcasted_iota(jnp.int32, (1, LANES), 1) < P
            dre_ref[q] = jnp.where(first, dec_prev[0], p16_re)
            dim_ref[q] = jnp.where(first, dec_prev[1], p16_im)


def _s5_operators(A_re, A_im, log_dt, B_re, B_im, C_re, C_im, D):
    f32 = jnp.float32
    G, P, H = S5_GROUPS, S5_STATE, S5_GROUP
    lam_re = jnp.minimum(A_re.astype(f32), RE_CLIP)
    lam_im = A_im.astype(f32)
    dt = jnp.exp(log_dt.astype(f32))[:, None]

    twice = lambda t: jnp.concatenate([t, t], axis=-1)
    tables = [twice(lam_re), twice(lam_im), twice(lam_re * dt), twice(lam_im * dt)]
    mats = [twice(jnp.swapaxes(B_re.astype(f32), 1, 2)), twice(jnp.swapaxes(B_im.astype(f32), 1, 2)),
            twice(C_re.astype(f32)), twice(C_im.astype(f32))]
    tspec = pl.BlockSpec((SLOTS, LANES), lambda j: (j, 0))
    mspec = pl.BlockSpec((SLOTS, H, LANES), lambda j: (j, 0, 0))
    dspec = pl.BlockSpec((PAIRS_PER_J, 1, LANES), lambda j: (j, 0, 0))
    return pl.pallas_call(
        _s5_ops_kernel,
        grid=(NJ,),
        in_specs=[tspec] * 4 + [mspec] * 4 + [pl.BlockSpec((SLOTS, H, 1), lambda j: (j, 0, 0))],
        out_specs=[
            pl.BlockSpec((SLOTS, H, KPAD), lambda j: (j, 0, 0)),
            pl.BlockSpec((PAIRS_PER_J, 2 * GW, GW), lambda j: (j, 0, 0)),
            pl.BlockSpec((SLOTS, GW, GW), lambda j: (j, 0, 0)),
            dspec, dspec,
        ],
        out_shape=[
            jax.ShapeDtypeStruct((G, H, KPAD), f32),
            jax.ShapeDtypeStruct((G // 2, 2 * GW, GW), _BF),
            jax.ShapeDtypeStruct((G, GW, GW), _BF),
            jax.ShapeDtypeStruct((G // 2, 1, LANES), f32),
            jax.ShapeDtypeStruct((G // 2, 1, LANES), f32),
        ],
        compiler_params=pltpu.CompilerParams(dimension_semantics=("parallel",), vmem_limit_bytes=VMEM_LIMIT),
        name="s5_ops",
    )(*tables, *mats, D.astype(f32).reshape(G, H, 1))


IN_CT = 4


def _s5_in_kernel(x_ref, g_ref, w_ref, u_ref, gate_ref):
    rows = BATCH * TCH
    for c in range(IN_CT):
        xc = jnp.swapaxes(x_ref[:, c * TCH:(c + 1) * TCH, :], 0, 1).reshape(rows, D_MODEL)
        hp = _rms(xc, g_ref[...]).astype(_BF)
        gate = _dot(hp, w_ref[:, BRANCH:])
        gate_ref[c] = (gate * _sigmoid(gate)).astype(_BF).reshape(TCH, BATCH, BRANCH)
        u = _dot(hp, w_ref[:, :BRANCH])
        u_ref[c] = u.astype(_BF).reshape(TCH, BATCH, BRANCH)


def _s5_in(x, g, w_in):
    grid = (NCHUNK // IN_CT,)
    blk4 = (IN_CT, TCH, BATCH, BRANCH)
    return pl.pallas_call(
        _s5_in_kernel,
        grid=grid,
        in_specs=[
            pl.BlockSpec((BATCH, IN_CT * TCH, D_MODEL), lambda i: (0, i, 0)),
            pl.BlockSpec((1, D_MODEL), lambda i: (0, 0)),
            pl.BlockSpec((D_MODEL, 2 * BRANCH), lambda i: (0, 0)),
        ],
        out_specs=[
            pl.BlockSpec(blk4, lambda i: (i, 0, 0, 0)),
            pl.BlockSpec(blk4, lambda i: (i, 0, 0, 0)),
        ],
        out_shape=[
            jax.ShapeDtypeStruct((NCHUNK, TCH, BATCH, BRANCH), _BF),
            jax.ShapeDtypeStruct((NCHUNK, TCH, BATCH, BRANCH), _BF),
        ],
        compiler_params=pltpu.CompilerParams(
            dimension_semantics=("parallel",), vmem_limit_bytes=VMEM_LIMIT),
        name="s5_in",
    )(x, g, w_in)


MIX_CT = 64
MIX_R = MIX_CT * BATCH
MIX_CB = 4
MIX_RT = MIX_CB * BATCH
MIX_NH = 4
MIX_HC = MIX_CT // MIX_NH
MIX_HR = MIX_HC * BATCH

def _gather_slots(v, slot):
    rolled = [v[0]] + [pltpu.roll(v[k], k * S5_GROUP, axis=1) for k in range(1, SLOTS)]
    out = []
    for a in range(SLOTS):
        acc = rolled[0]
        for k in range(1, SLOTS):
            acc = jnp.where(slot == (a + k) % SLOTS, rolled[k], acc)
        out.append(acc)
    return out


def _scatter_slots(v, slot):
    out = []
    for k in range(SLOTS):
        acc = v[0]
        for a in range(1, SLOTS):
            acc = jnp.where(slot == (a + k) % SLOTS, v[a], acc)
        out.append(acc if k == 0 else pltpu.roll(acc, (SLOTS - k) * S5_GROUP, axis=1))
    return out


def _column_perms():
    p = np.zeros((SLOTS, GW, GW), np.float32)
    for a in range(SLOTS):
        for x in range(TCH):
            f = int(_FRAME_OF_SLOT[a, x])
            for c in range(S5_GROUP):
                p[a, f * S5_GROUP + c, x * S5_GROUP + c] = 1.0
    return p


_CPERM = _column_perms()


def _s5_mix_kernel(u_ref, kpad_ref, mb_ref, mc_ref, cperm_ref, dre_ref, dim_ref, y_ref,
                   op_ref, xs_ref, ys_ref, ss_ref, hs_ref, carry_ref):
    ct = pl.program_id(1)

    @pl.when(ct == 0)
    def _():
        carry_ref[...] = jnp.zeros_like(carry_ref)
        for a in range(SLOTS):
            kp = kpad_ref[a]
            rows = []
            for x in range(TCH):
                off = (TCH - int(_FRAME_OF_SLOT[a, x])) * S5_GROUP
                rows.append(kp[:, off:off + GW])
            toep = jnp.concatenate(rows, axis=0).astype(_BF)
            op_ref[a, 0:GW, :] = _dot(toep, cperm_ref[a]).astype(_BF)
            op_ref[a, GW:2 * GW, :] = mc_ref[a]

    slot = lax.broadcasted_iota(jnp.int32, (MIX_RT, LANES), 1) // S5_GROUP
    dre = [dre_ref[q] for q in range(PAIRS_PER_J)]
    dim_ = [dim_ref[q] for q in range(PAIRS_PER_J)]

    def relayout_in(c0):
        r0 = c0 * BATCH
        for half in range(2):
            pieces = [u_ref[c0:c0 + MIX_CB, SLOTS * half + k, :, :].reshape(MIX_RT, LANES) for k in range(SLOTS)]
            groups = _gather_slots(pieces, slot)
            for a in range(SLOTS):
                lo = (a % 2) * GW + half * LANES
                xs_ref[a // 2, r0:r0 + MIX_RT, lo:lo + LANES] = groups[a]

    def relayout_out(c0):
        r0 = c0 * BATCH
        for half in range(2):
            groups = [ys_ref[a, r0:r0 + MIX_RT, half * LANES:(half + 1) * LANES].astype(_BF) for a in range(SLOTS)]
            frames = _scatter_slots(groups, slot)
            for k in range(SLOTS):
                y_ref[c0:c0 + MIX_CB, SLOTS * half + k, :, :] = frames[k].reshape(MIX_CB, BATCH, LANES)

    def recurrence(c0, state):
        for c in range(c0, c0 + MIX_HC):
            r0 = c * BATCH
            new = []
            for q in range(PAIRS_PER_J):
                hre, him = state[2 * q], state[2 * q + 1]
                hs_ref[q, r0:r0 + BATCH, 0:LANES] = hre.astype(_BF)
                hs_ref[q, r0:r0 + BATCH, LANES:2 * LANES] = him.astype(_BF)
                sre = ss_ref[q, r0:r0 + BATCH, 0:LANES]
                sim = ss_ref[q, r0:r0 + BATCH, LANES:2 * LANES]
                new.append(dre[q] * hre - dim_[q] * him + sre)
                new.append(dre[q] * him + dim_[q] * hre + sim)
            state = new
        return state

    blocks = [(h * MIX_HC, h * MIX_HC * BATCH) for h in range(MIX_NH)]
    for c0, r0 in blocks:
        for t in range(MIX_HC // MIX_CB):
            relayout_in(c0 + t * MIX_CB)
        for q in range(PAIRS_PER_J):
            ss_ref[q, r0:r0 + MIX_HR, :] = _dot(xs_ref[q, r0:r0 + MIX_HR, :], mb_ref[q])
    state = [carry_ref[q, k] for q in range(PAIRS_PER_J) for k in range(2)]
    for c0, r0 in blocks:
        state = recurrence(c0, state)
    for q in range(PAIRS_PER_J):
        carry_ref[q, 0] = state[2 * q]
        carry_ref[q, 1] = state[2 * q + 1]
    for c0, r0 in blocks:
        for a in range(SLOTS):
            q, m = a // 2, a % 2
            lhs = jnp.concatenate([xs_ref[q, r0:r0 + MIX_HR, m * GW:(m + 1) * GW], hs_ref[q, r0:r0 + MIX_HR, :]], axis=1)
            ys_ref[a, r0:r0 + MIX_HR, :] = _dot(lhs, op_ref[a])
    for c0, r0 in blocks:
        for t in range(MIX_HC // MIX_CB):
            relayout_out(c0 + t * MIX_CB)


def _s5_mix(u, kpad, mb, mc, dec_re, dec_im):
    grid = (NJ, NCHUNK // MIX_CT)
    blk = (MIX_CT, TCH, BATCH, LANES)
    dspec = pl.BlockSpec((PAIRS_PER_J, 1, LANES), lambda j, c: (j, 0, 0))
    return pl.pallas_call(
        _s5_mix_kernel,
        grid=grid,
        in_specs=[
            pl.BlockSpec(blk, lambda j, c: (c, 0, 0, j)),
            pl.BlockSpec((SLOTS, S5_GROUP, KPAD), lambda j, c: (j, 0, 0)),
            pl.BlockSpec((PAIRS_PER_J, 2 * GW, GW), lambda j, c: (j, 0, 0)),
            pl.BlockSpec((SLOTS, GW, GW), lambda j, c: (j, 0, 0)),
            pl.BlockSpec((SLOTS, GW, GW), lambda j, c: (0, 0, 0)),
            dspec, dspec,
        ],
        out_specs=pl.BlockSpec(blk, lambda j, c: (c, 0, 0, j)),
        out_shape=jax.ShapeDtypeStruct((NCHUNK, TCH, BATCH, BRANCH), _BF),
        scratch_shapes=[
            pltpu.VMEM((SLOTS, 2 * GW, GW), _BF),
            pltpu.VMEM((PAIRS_PER_J, MIX_R, 2 * GW), _BF),
            pltpu.VMEM((SLOTS, MIX_R, GW), _F32),
            pltpu.VMEM((PAIRS_PER_J, MIX_R, GW), _F32),
            pltpu.VMEM((PAIRS_PER_J, MIX_R, GW), _BF),
            pltpu.VMEM((PAIRS_PER_J, 2, BATCH, LANES), _F32),
        ],
        compiler_params=pltpu.CompilerParams(
            dimension_semantics=("parallel", "arbitrary"), vmem_limit_bytes=VMEM_LIMIT),
        name="s5_mix",
    )(u, kpad, mb, mc, jnp.asarray(_CPERM, _BF), dec_re, dec_im)


def _wavefront(stages, n):
    for t in range(len(stages) + n - 1):
        for i in range(n):
            k = t - i
            if 0 <= k < len(stages):
                stages[k](i)


def _s5_out_kernel(y_ref, gate_ref, x_ref, wglu_ref, bglu_ref, wout_ref, o_ref):
    rows = BATCH * TCH
    st = [dict() for _ in range(IN_CT)]

    def act(c):
        st[c]["y"] = _gelu(y_ref[c].reshape(rows, BRANCH).astype(_F32))

    def glu_mm(c):
        st[c]["glu"] = _dot(st[c]["y"].astype(_BF), wglu_ref[...])

    def gating(c):
        y = st[c].pop("y") * _sigmoid(st[c].pop("glu") + bglu_ref[...])
        st[c]["z"] = (y * gate_ref[c].reshape(rows, BRANCH).astype(_F32)).astype(_BF)

    def out_mm(c):
        st[c]["out"] = _dot(st[c].pop("z"), wout_ref[...])

    def residual(c):
        out = st[c].pop("out").reshape(TCH, BATCH, D_MODEL)
        o_ref[:, c * TCH:(c + 1) * TCH, :] = x_ref[:, c * TCH:(c + 1) * TCH, :] + jnp.swapaxes(out, 0, 1)

    _wavefront([act, glu_mm, gating, out_mm, residual], IN_CT)


def _s5_out(y, gate, x, w_glu, b_glu, w_out):
    grid = (NCHUNK // IN_CT,)
    blk4 = (IN_CT, TCH, BATCH, BRANCH)
    xblk = pl.BlockSpec((BATCH, IN_CT * TCH, D_MODEL), lambda i: (0, i, 0))
    return pl.pallas_call(
        _s5_out_kernel,
        grid=grid,
        in_specs=[
            pl.BlockSpec(blk4, lambda i: (i, 0, 0, 0)),
            pl.BlockSpec(blk4, lambda i: (i, 0, 0, 0)),
            xblk,
            pl.BlockSpec((BRANCH, BRANCH), lambda i: (0, 0)),
            pl.BlockSpec((1, BRANCH), lambda i: (0, 0)),
            pl.BlockSpec((BRANCH, D_MODEL), lambda i: (0, 0)),
        ],
        out_specs=xblk,
        out_shape=jax.ShapeDtypeStruct((BATCH, SEQ, D_MODEL), _F32),
        compiler_params=pltpu.CompilerParams(
            dimension_semantics=("parallel",), vmem_limit_bytes=VMEM_LIMIT),
        name="s5_out",
    )(y, gate, x, w_glu, b_glu, w_out)


SGU_TB = 1024
SGU_NB = SGU_TB // SGU_BLOCK


SGU_RB = 512
SGU_BPB = SGU_RB // SGU_BLOCK


def _sgu_kernel(x_ref, g_ref, win_ref, lng_ref, lnb_ref, ws_ref, bs_ref, wout_ref, fg_ref, o_ref, mix_ref):
    blocks = [slice(r0, r0 + SGU_RB) for r0 in range(0, SGU_TB, SGU_RB)]
    st = [dict() for _ in blocks]

    def norm_in(i):
        st[i]["h"] = _rms(x_ref[blocks[i], :], g_ref[...]).astype(_BF)

    def proj_v(i):
        st[i]["v"] = _dot(st[i]["h"], win_ref[:, BRANCH:2 * BRANCH])

    def proj_ug(i):
        st[i]["u"] = _dot(st[i]["h"], win_ref[:, :BRANCH])
        st[i]["gate"] = _dot(st[i]["h"], win_ref[:, 2 * BRANCH:])

    def layer_norm(i):
        v = _gelu(st[i].pop("v"))
        mu = jnp.mean(v, axis=-1, keepdims=True)
        vc = v - mu
        var = jnp.mean(vc * vc, axis=-1, keepdims=True)
        st[i]["vn"] = (vc * lax.rsqrt(var + LN_EPS) * lng_ref[...] + lnb_ref[...]).astype(_BF)

    def spatial(i):
        vn = st[i].pop("vn")
        r0 = blocks[i].start
        for hd in range(SGU_HEADS):
            lo = hd * SGU_HEAD_DIM
            rhs = jnp.concatenate(
                [vn[b * SGU_BLOCK:(b + 1) * SGU_BLOCK, lo:lo + SGU_HEAD_DIM] for b in range(SGU_BPB)], axis=1)
            res = _dot(ws_ref[hd], rhs)
            for b in range(SGU_BPB):
                mix_ref[r0 + b * SGU_BLOCK:r0 + (b + 1) * SGU_BLOCK, lo:lo + SGU_HEAD_DIM] = (
                    res[:, b * SGU_HEAD_DIM:(b + 1) * SGU_HEAD_DIM] + bs_ref[:, lo:lo + SGU_HEAD_DIM])

    def gating(i):
        gate = st[i].pop("gate")
        y = _gelu(st[i].pop("u")) * mix_ref[blocks[i], :]
        st[i]["z"] = (y * (gate * _sigmoid(gate))).astype(_BF)

    def proj_out(i):
        st[i]["out"] = _dot(st[i].pop("z"), wout_ref[...])

    def norm_out(i):
        x2 = x_ref[blocks[i], :] + st[i].pop("out")
        o_ref[blocks[i], :] = _rms(x2, fg_ref[...])

    _wavefront([norm_in, proj_v, layer_norm, proj_ug, spatial, gating, proj_out, norm_out], len(blocks))


def _sgu_layer(x, g, w_in, ln_g, ln_b, ws, bs, w_out, final_g):
    n = BATCH * SEQ
    grid = (n // SGU_TB,)
    row = lambda i: (0, 0)
    xblk = pl.BlockSpec((SGU_TB, D_MODEL), lambda i: (i, 0))
    return pl.pallas_call(
        _sgu_kernel,
        grid=grid,
        in_specs=[
            xblk,
            pl.BlockSpec((1, D_MODEL), row),
            pl.BlockSpec((D_MODEL, 3 * BRANCH), row),
            pl.BlockSpec((1, BRANCH), row),
            pl.BlockSpec((1, BRANCH), row),
            pl.BlockSpec((SGU_HEADS, SGU_BLOCK, SGU_BLOCK), lambda i: (0, 0, 0)),
            pl.BlockSpec((SGU_BLOCK, BRANCH), row),
            pl.BlockSpec((BRANCH, D_MODEL), row),
            pl.BlockSpec((1, D_MODEL), row),
        ],
        out_specs=xblk,
        out_shape=jax.ShapeDtypeStruct((n, D_MODEL), _F32),
        scratch_shapes=[pltpu.VMEM((SGU_TB, BRANCH), _F32)],
        compiler_params=pltpu.CompilerParams(
            dimension_semantics=("parallel",), vmem_limit_bytes=VMEM_LIMIT),
        name="sgu_layer",
    )(x, g, w_in, ln_g, ln_b, ws, bs, w_out, final_g)


def kernel(x, norm_g, final_g, s5_w_in, s5_A_re, s5_A_im, s5_log_dt, s5_B_re, s5_B_im, s5_C_re, s5_C_im, s5_D, s5_w_glu, s5_b_glu, s5_w_out, sgu_w_in, sgu_ln_g, sgu_ln_b, sgu_w_s, sgu_b_s, sgu_w_out):
    kpad, mb, mc, dec_re, dec_im = _s5_operators(
        s5_A_re[0], s5_A_im[0], s5_log_dt[0], s5_B_re[0], s5_B_im[0], s5_C_re[0], s5_C_im[0], s5_D[0])
    u, gate = _s5_in(x, norm_g[0][None, :], s5_w_in[0].astype(_BF))
    y = _s5_mix(u, kpad, mb, mc, dec_re, dec_im)
    x1 = _s5_out(y, gate, x, s5_w_glu[0].astype(_BF), s5_b_glu[0][None, :], s5_w_out[0].astype(_BF))

    mask = jnp.tril(jnp.ones((SGU_BLOCK, SGU_BLOCK), dtype=bool))
    ws = jnp.where(mask[None], sgu_w_s[0], 0.0).astype(_BF)
    bs = jnp.repeat(jnp.transpose(sgu_b_s[0]), SGU_HEAD_DIM, axis=1)
    out = _sgu_layer(x1.reshape(BATCH * SEQ, D_MODEL), norm_g[1][None, :], sgu_w_in[0].astype(_BF),
                     sgu_ln_g[0][None, :], sgu_ln_b[0][None, :], ws, bs,
                     sgu_w_out[0].astype(_BF), final_g[None, :])
    return out.reshape(BATCH, SEQ, D_MODEL)
```

```python
import math

import numpy as np
import jax
import jax.numpy as jnp
from jax import lax
from jax.experimental import pallas as pl
from jax.experimental.pallas import tpu as pltpu

D_MODEL = 1024
BATCH = 16
SEQ = 2048
BRANCH = D_MODEL
S5_GROUP = 16
S5_GROUPS = BRANCH // S5_GROUP
S5_STATE = 64
SGU_BLOCK = 128
SGU_HEADS = 8
SGU_HEAD_DIM = BRANCH // SGU_HEADS
RMS_EPS = 1e-6
LN_EPS = 1e-5
RE_CLIP = -1e-4

LANES = 128
TCH = 16
NCHUNK = SEQ // TCH
SLOTS = LANES // S5_GROUP
NJ = BRANCH // LANES
GW = TCH * S5_GROUP
PAIRS_PER_J = SLOTS // 2
VMEM_LIMIT = 56 * 1024 * 1024

_HI = lax.Precision.HIGHEST
_BF = jnp.bfloat16
_F32 = jnp.float32


_GELU_C1 = math.sqrt(2.0 / math.pi)
_GELU_C2 = _GELU_C1 * 0.044715


def _gelu(x):
    inner = x * (_GELU_C1 + _GELU_C2 * (x * x))
    return (0.5 * x) * (1.0 + jnp.tanh(inner))


def _sigmoid(x):
    return jax.nn.sigmoid(x)


def _dot(a, b):
    return jnp.dot(a, b, preferred_element_type=_F32)


def _wavefront(stages, n):
    for t in range(len(stages) + n - 1):
        for i in range(n):
            k = t - i
            if 0 <= k < len(stages):
                stages[k](i)


def _rms(x, g):
    return x * lax.rsqrt(jnp.mean(x * x, axis=-1, keepdims=True) + RMS_EPS) * g


KPAD = 2 * GW


def _slot_frame_table():
    a = np.arange(SLOTS)[:, None]
    x = np.arange(TCH)[None, :]
    half, k = x // SLOTS, x % SLOTS
    return SLOTS * half + (k - a) % SLOTS


_FRAME_OF_SLOT = _slot_frame_table()


def _s5_ops_kernel(lre_ref, lim_ref, ldr_ref, ldi_ref, btr_ref, bti_ref, cr_ref, ci_ref, d_ref,
                   kpad_ref, mb_ref, mc_ref, dre_ref, dim_ref):
    P = S5_STATE
    lane = lax.broadcasted_iota(jnp.int32, (S5_GROUP, LANES), 1)
    kcol = lax.broadcasted_iota(jnp.int32, (TCH, 1), 0)
    lane256 = lax.broadcasted_iota(jnp.int32, (S5_GROUP, GW), 1)
    row256 = lax.broadcasted_iota(jnp.int32, (S5_GROUP, GW), 0)
    lane_gw = lax.broadcasted_iota(jnp.int32, (GW, LANES), 1)
    pick = (lax.broadcasted_iota(jnp.int32, (P, LANES), 1)
            == lax.broadcasted_iota(jnp.int32, (P, LANES), 0)).astype(_F32)
    nt = (((1,), (1,)), ((), ()))
    zeros64 = jnp.zeros((P, GW), _F32)
    dec_prev = None
    for a in range(SLOTS):
        lam_re, lam_im = lre_ref[a:a + 1, :], lim_ref[a:a + 1, :]
        ldr, ldi = ldr_ref[a:a + 1, :], ldi_ref[a:a + 1, :]

        def power(k):
            mag = jnp.exp(k * ldr)
            return mag * jnp.cos(k * ldi), mag * jnp.sin(k * ldi)

        def outer(t_re, t_im, m_re, m_im):
            re = [t_re[k:k + 1] * m_re - t_im[k:k + 1] * m_im for k in range(TCH)]
            im = [t_re[k:k + 1] * m_im + t_im[k:k + 1] * m_re for k in range(TCH)]
            return jnp.concatenate(re, axis=0), jnp.concatenate(im, axis=0)

        frame = (kcol & SLOTS) + (((kcol & (SLOTS - 1)) + (SLOTS - a)) & (SLOTS - 1))
        pw_re, pw_im = power(kcol.astype(_F32))
        pi_re, pi_im = power((TCH - 1 - frame).astype(_F32))
        po_re, po_im = power((frame + 1).astype(_F32))
        p16_re, p16_im = power(jnp.full((1, 1), float(TCH), _F32))

        nr, ni = pw_re[1:2] - 1.0, pw_im[1:2]
        den = lam_re * lam_re + lam_im * lam_im
        cf_re = (nr * lam_re + ni * lam_im) / den
        cf_im = (ni * lam_re - nr * lam_im) / den
        b_re, b_im = btr_ref[a], bti_ref[a]
        bb_re = cf_re * b_re - cf_im * b_im
        bb_im = cf_re * b_im + cf_im * b_re
        c_re, c_im = cr_ref[a], ci_ref[a]

        lc_re, lc_im = outer(pw_re, pw_im, c_re, c_im)
        once = lane < P
        taps = (lax.dot_general(jnp.where(once, bb_re, 0.0), lc_re, nt, precision=_HI, preferred_element_type=_F32)
                - lax.dot_general(jnp.where(once, bb_im, 0.0), lc_im, nt, precision=_HI, preferred_element_type=_F32))
        taps = taps + jnp.where(lane256 == row256, d_ref[a], 0.0)
        kpad_ref[a, :, 0:GW] = jnp.zeros((S5_GROUP, GW), _F32)
        kpad_ref[a, :, GW:2 * GW] = taps

        q, m = a // 2, a % 2
        mine = (lane_gw >= P) if m else (lane_gw < P)

        in_re, in_im = outer(pi_re, pi_im, bb_re, bb_im)
        mb_ref[q, m * GW:(m + 1) * GW, 0:LANES] = jnp.where(mine, in_re, 0.0).astype(_BF)
        mb_ref[q, m * GW:(m + 1) * GW, LANES:2 * LANES] = jnp.where(mine, in_im, 0.0).astype(_BF)

        ot_re, ot_im = outer(po_re, po_im, c_re, c_im)
        o_re = lax.dot_general(pick, ot_re, nt, precision=_HI, preferred_element_type=_F32)
        o_im = lax.dot_general(pick, ot_im, nt, precision=_HI, preferred_element_type=_F32)
        blocks = [zeros64, zeros64, zeros64, zeros64]
        blocks[m], blocks[2 + m] = o_re, -o_im
        mc_ref[a] = jnp.concatenate(blocks, axis=0).astype(_BF)

        if m == 0:
            dec_prev = (p16_re, p16_im)
        else:
            first = lax.broadcasted_iota(jnp.int32, (1, LANES), 1) < P
            dre_ref[q] = jnp.where(first, dec_prev[0], p16_re)
            dim_ref[q] = jnp.where(first, dec_prev[1], p16_im)


def _s5_operators(A_re, A_im, log_dt, B_re, B_im, C_re, C_im, D):
    f32 = jnp.float32
    G, P, H = S5_GROUPS, S5_STATE, S5_GROUP
    lam_re = jnp.minimum(A_re.astype(f32), RE_CLIP)
    lam_im = A_im.astype(f32)
    dt = jnp.exp(log_dt.astype(f32))[:, None]

    twice = lambda t: jnp.concatenate([t, t], axis=-1)
    tables = [twice(lam_re), twice(lam_im), twice(lam_re * dt), twice(lam_im * dt)]
    mats = [twice(jnp.swapaxes(B_re.astype(f32), 1, 2)), twice(jnp.swapaxes(B_im.astype(f32), 1, 2)),
            twice(C_re.astype(f32)), twice(C_im.astype(f32))]
    tspec = pl.BlockSpec((SLOTS, LANES), lambda j: (j, 0))
    mspec = pl.BlockSpec((SLOTS, H, LANES), lambda j: (j, 0, 0))
    dspec = pl.BlockSpec((PAIRS_PER_J, 1, LANES), lambda j: (j, 0, 0))
    return pl.pallas_call(
        _s5_ops_kernel,
        grid=(NJ,),
        in_specs=[tspec] * 4 + [mspec] * 4 + [pl.BlockSpec((SLOTS, H, 1), lambda j: (j, 0, 0))],
        out_specs=[
            pl.BlockSpec((SLOTS, H, KPAD), lambda j: (j, 0, 0)),
            pl.BlockSpec((PAIRS_PER_J, 2 * GW, GW), lambda j: (j, 0, 0)),
            pl.BlockSpec((SLOTS, GW, GW), lambda j: (j, 0, 0)),
            dspec, dspec,
        ],
        out_shape=[
            jax.ShapeDtypeStruct((G, H, KPAD), f32),
            jax.ShapeDtypeStruct((G // 2, 2 * GW, GW), _BF),
            jax.ShapeDtypeStruct((G, GW, GW), _BF),
            jax.ShapeDtypeStruct((G // 2, 1, LANES), f32),
            jax.ShapeDtypeStruct((G // 2, 1, LANES), f32),
        ],
        compiler_params=pltpu.CompilerParams(dimension_semantics=("parallel",), vmem_limit_bytes=VMEM_LIMIT),
        name="s5_ops",
    )(*tables, *mats, D.astype(f32).reshape(G, H, 1))


IN_CT = 8


IN_MB = 1


def _s5_in_kernel(x_ref, g_ref, w_ref, u_ref, gate_ref):
    rows = BATCH * TCH
    nb = IN_CT // IN_MB
    st = [dict() for _ in range(nb)]

    def norm(i):
        parts = []
        for c in range(i * IN_MB, (i + 1) * IN_MB):
            xc = jnp.swapaxes(x_ref[:, c * TCH:(c + 1) * TCH, :], 0, 1).reshape(rows, D_MODEL)
            parts.append(_rms(xc, g_ref[...]).astype(_BF))
        st[i]["hp"] = jnp.concatenate(parts, axis=0)

    def gate_mm(i):
        gate = _dot(st[i]["hp"], w_ref[:, BRANCH:])
        gate_ref[i * IN_MB:(i + 1) * IN_MB] = (
            (gate * _sigmoid(gate)).astype(_BF).reshape(IN_MB, TCH, BATCH, BRANCH))

    def u_mm(i):
        u = _dot(st[i].pop("hp"), w_ref[:, :BRANCH])
        u_ref[i * IN_MB:(i + 1) * IN_MB] = u.astype(_BF).reshape(IN_MB, TCH, BATCH, BRANCH)

    _wavefront([norm, gate_mm, u_mm], nb)


def _s5_in(x, g, w_in):
    grid = (NCHUNK // IN_CT,)
    blk4 = (IN_CT, TCH, BATCH, BRANCH)
    return pl.pallas_call(
        _s5_in_kernel,
        grid=grid,
        in_specs=[
            pl.BlockSpec((BATCH, IN_CT * TCH, D_MODEL), lambda i: (0, i, 0)),
            pl.BlockSpec((1, D_MODEL), lambda i: (0, 0)),
            pl.BlockSpec((D_MODEL, 2 * BRANCH), lambda i: (0, 0)),
        ],
        out_specs=[
            pl.BlockSpec(blk4, lambda i: (i, 0, 0, 0)),
            pl.BlockSpec(blk4, lambda i: (i, 0, 0, 0)),
        ],
        out_shape=[
            jax.ShapeDtypeStruct((NCHUNK, TCH, BATCH, BRANCH), _BF),
            jax.ShapeDtypeStruct((NCHUNK, TCH, BATCH, BRANCH), _BF),
        ],
        compiler_params=pltpu.CompilerParams(
            dimension_semantics=("parallel",), vmem_limit_bytes=VMEM_LIMIT),
        name="s5_in",
    )(x, g, w_in)


MIX_CT = 64
MIX_R = MIX_CT * BATCH
MIX_CB = 4
MIX_RT = MIX_CB * BATCH
MIX_NH = 4
MIX_HC = MIX_CT // MIX_NH
MIX_HR = MIX_HC * BATCH

def _gather_slots(v, slot):
    rolled = [v[0]] + [pltpu.roll(v[k], k * S5_GROUP, axis=1) for k in range(1, SLOTS)]
    out = []
    for a in range(SLOTS):
        acc = rolled[0]
        for k in range(1, SLOTS):
            acc = jnp.where(slot == (a + k) % SLOTS, rolled[k], acc)
        out.append(acc)
    return out


def _scatter_slots(v, slot):
    out = []
    for k in range(SLOTS):
        acc = v[0]
        for a in range(1, SLOTS):
            acc = jnp.where(slot == (a + k) % SLOTS, v[a], acc)
        out.append(acc if k == 0 else pltpu.roll(acc, (SLOTS - k) * S5_GROUP, axis=1))
    return out


def _column_perms():
    p = np.zeros((SLOTS, GW, GW), np.float32)
    for a in range(SLOTS):
        for x in range(TCH):
            f = int(_FRAME_OF_SLOT[a, x])
            for c in range(S5_GROUP):
                p[a, f * S5_GROUP + c, x * S5_GROUP + c] = 1.0
    return p


_CPERM = _column_perms()


def _s5_mix_kernel(u_ref, kpad_ref, mb_ref, mc_ref, cperm_ref, dre_ref, dim_ref, y_ref,
                   op_ref, xs_ref, ys_ref, ss_ref, hs_ref, carry_ref):
    ct = pl.program_id(1)

    @pl.when(ct == 0)
    def _():
        carry_ref[...] = jnp.zeros_like(carry_ref)
        for a in range(SLOTS):
            kp = kpad_ref[a]
            rows = []
            for x in range(TCH):
                off = (TCH - int(_FRAME_OF_SLOT[a, x])) * S5_GROUP
                rows.append(kp[:, off:off + GW])
            toep = jnp.concatenate(rows, axis=0).astype(_BF)
            op_ref[a, 0:GW, :] = _dot(toep, cperm_ref[a]).astype(_BF)
            op_ref[a, GW:2 * GW, :] = mc_ref[a]

    slot = lax.broadcasted_iota(jnp.int32, (MIX_RT, LANES), 1) // S5_GROUP
    dre = [dre_ref[q] for q in range(PAIRS_PER_J)]
    dim_ = [dim_ref[q] for q in range(PAIRS_PER_J)]

    def relayout_in(c0):
        r0 = c0 * BATCH
        for half in range(2):
            pieces = [u_ref[c0:c0 + MIX_CB, SLOTS * half + k, :, :].reshape(MIX_RT, LANES) for k in range(SLOTS)]
            groups = _gather_slots(pieces, slot)
            for a in range(SLOTS):
                lo = (a % 2) * GW + half * LANES
                xs_ref[a // 2, r0:r0 + MIX_RT, lo:lo + LANES] = groups[a]

    def relayout_out(c0):
        r0 = c0 * BATCH
        for half in range(2):
            groups = [ys_ref[a, r0:r0 + MIX_RT, half * LANES:(half + 1) * LANES].astype(_BF) for a in range(SLOTS)]
            frames = _scatter_slots(groups, slot)
            for k in range(SLOTS):
                y_ref[c0:c0 + MIX_CB, SLOTS * half + k, :, :] = frames[k].reshape(MIX_CB, BATCH, LANES)

    def recurrence(c0, state):
        for c in range(c0, c0 + MIX_HC):
            r0 = c * BATCH
            new = []
            for q in range(PAIRS_PER_J):
                hre, him = state[2 * q], state[2 * q + 1]
                hs_ref[q, r0:r0 + BATCH, 0:LANES] = hre.astype(_BF)
                hs_ref[q, r0:r0 + BATCH, LANES:2 * LANES] = him.astype(_BF)
                sre = ss_ref[q, r0:r0 + BATCH, 0:LANES]
                sim = ss_ref[q, r0:r0 + BATCH, LANES:2 * LANES]
                new.append(dre[q] * hre - dim_[q] * him + sre)
                new.append(dre[q] * him + dim_[q] * hre + sim)
            state = new
        return state

    blocks = [(h * MIX_HC, h * MIX_HC * BATCH) for h in range(MIX_NH)]
    for c0, r0 in blocks:
        for t in range(MIX_HC // MIX_CB):
            relayout_in(c0 + t * MIX_CB)
        for q in range(PAIRS_PER_J):
            ss_ref[q, r0:r0 + MIX_HR, :] = _dot(xs_ref[q, r0:r0 + MIX_HR, :], mb_ref[q])
    state = [carry_ref[q, k] for q in range(PAIRS_PER_J) for k in range(2)]
    for c0, r0 in blocks:
        state = recurrence(c0, state)
    for q in range(PAIRS_PER_J):
        carry_ref[q, 0] = state[2 * q]
        carry_ref[q, 1] = state[2 * q + 1]
    for c0, r0 in blocks:
        for a in range(SLOTS):
            q, m = a // 2, a % 2
            lhs = jnp.concatenate([xs_ref[q, r0:r0 + MIX_HR, m * GW:(m + 1) * GW], hs_ref[q, r0:r0 + MIX_HR, :]], axis=1)
            ys_ref[a, r0:r0 + MIX_HR, :] = _dot(lhs, op_ref[a])
    for c0, r0 in blocks:
        for t in range(MIX_HC // MIX_CB):
            relayout_out(c0 + t * MIX_CB)


def _s5_mix(u, kpad, mb, mc, dec_re, dec_im):
    grid = (NJ, NCHUNK // MIX_CT)
    blk = (MIX_CT, TCH, BATCH, LANES)
    dspec = pl.BlockSpec((PAIRS_PER_J, 1, LANES), lambda j, c: (j, 0, 0))
    return pl.pallas_call(
        _s5_mix_kernel,
        grid=grid,
        in_specs=[
            pl.BlockSpec(blk, lambda j, c: (c, 0, 0, j)),
            pl.BlockSpec((SLOTS, S5_GROUP, KPAD), lambda j, c: (j, 0, 0)),
            pl.BlockSpec((PAIRS_PER_J, 2 * GW, GW), lambda j, c: (j, 0, 0)),
            pl.BlockSpec((SLOTS, GW, GW), lambda j, c: (j, 0, 0)),
            pl.BlockSpec((SLOTS, GW, GW), lambda j, c: (0, 0, 0)),
            dspec, dspec,
        ],
        out_specs=pl.BlockSpec(blk, lambda j, c: (c, 0, 0, j)),
        out_shape=jax.ShapeDtypeStruct((NCHUNK, TCH, BATCH, BRANCH), _BF),
        scratch_shapes=[
            pltpu.VMEM((SLOTS, 2 * GW, GW), _BF),
            pltpu.VMEM((PAIRS_PER_J, MIX_R, 2 * GW), _BF),
            pltpu.VMEM((SLOTS, MIX_R, GW), _F32),
            pltpu.VMEM((PAIRS_PER_J, MIX_R, GW), _F32),
            pltpu.VMEM((PAIRS_PER_J, MIX_R, GW), _BF),
            pltpu.VMEM((PAIRS_PER_J, 2, BATCH, LANES), _F32),
        ],
        compiler_params=pltpu.CompilerParams(
            dimension_semantics=("parallel", "arbitrary"), vmem_limit_bytes=VMEM_LIMIT),
        name="s5_mix",
    )(u, kpad, mb, mc, jnp.asarray(_CPERM, _BF), dec_re, dec_im)


def _s5_out_kernel(y_ref, gate_ref, wglu_ref, bglu_ref, wout_ref, o_ref):
    rows = IN_MB * BATCH * TCH
    nb = IN_CT // IN_MB
    st = [dict() for _ in range(nb)]

    def act(i):
        st[i]["y"] = _gelu(y_ref[i * IN_MB:(i + 1) * IN_MB].reshape(rows, BRANCH).astype(_F32))

    def glu_mm(i):
        st[i]["glu"] = _dot(st[i]["y"].astype(_BF), wglu_ref[...])

    def gating(i):
        y = st[i].pop("y") * _sigmoid(st[i].pop("glu") + bglu_ref[...])
        sg = gate_ref[i * IN_MB:(i + 1) * IN_MB].reshape(rows, BRANCH).astype(_F32)
        st[i]["z"] = (y * sg).astype(_BF)

    def out_mm(i):
        st[i]["out"] = _dot(st[i].pop("z"), wout_ref[...])

    def residual(i):
        out = st[i].pop("out").reshape(IN_MB, TCH, BATCH, D_MODEL)
        for k in range(IN_MB):
            c = i * IN_MB + k
            o_ref[:, c * TCH:(c + 1) * TCH, :] = jnp.swapaxes(out[k], 0, 1)

    _wavefront([act, glu_mm, gating, out_mm, residual], nb)


def _s5_out(y, gate, w_glu, b_glu, w_out):
    grid = (NCHUNK // IN_CT,)
    blk4 = (IN_CT, TCH, BATCH, BRANCH)
    xblk = pl.BlockSpec((BATCH, IN_CT * TCH, D_MODEL), lambda i: (0, i, 0))
    return pl.pallas_call(
        _s5_out_kernel,
        grid=grid,
        in_specs=[
            pl.BlockSpec(blk4, lambda i: (i, 0, 0, 0)),
            pl.BlockSpec(blk4, lambda i: (i, 0, 0, 0)),
            pl.BlockSpec((BRANCH, BRANCH), lambda i: (0, 0)),
            pl.BlockSpec((1, BRANCH), lambda i: (0, 0)),
            pl.BlockSpec((BRANCH, D_MODEL), lambda i: (0, 0)),
        ],
        out_specs=xblk,
        out_shape=jax.ShapeDtypeStruct((BATCH, SEQ, D_MODEL), _F32),
        compiler_params=pltpu.CompilerParams(
            dimension_semantics=("parallel",), vmem_limit_bytes=VMEM_LIMIT),
        name="s5_out",
    )(y, gate, w_glu, b_glu, w_out)


SGU_TB = 1024
SGU_NB = SGU_TB // SGU_BLOCK


SGU_RB = 512
SGU_BPB = SGU_RB // SGU_BLOCK


def _sgu_kernel(x_ref, d_ref, g_ref, win_ref, lng_ref, lnb_ref, ws_ref, bs_ref, wout_ref, fg_ref, o_ref, mix_ref):
    blocks = [slice(r0, r0 + SGU_RB) for r0 in range(0, SGU_TB, SGU_RB)]
    st = [dict() for _ in blocks]

    def norm_in(i):
        st[i]["h"] = _rms(x_ref[blocks[i], :] + d_ref[blocks[i], :], g_ref[...]).astype(_BF)

    def proj_v(i):
        st[i]["v"] = _dot(st[i]["h"], win_ref[:, BRANCH:2 * BRANCH])

    def proj_ug(i):
        st[i]["u"] = _dot(st[i]["h"], win_ref[:, :BRANCH])
        st[i]["gate"] = _dot(st[i]["h"], win_ref[:, 2 * BRANCH:])

    def layer_norm(i):
        v = _gelu(st[i].pop("v"))
        mu = jnp.mean(v, axis=-1, keepdims=True)
        vc = v - mu
        var = jnp.mean(vc * vc, axis=-1, keepdims=True)
        st[i]["vn"] = (vc * lax.rsqrt(var + LN_EPS) * lng_ref[...] + lnb_ref[...]).astype(_BF)

    def spatial(i):
        vn = st[i].pop("vn")
        r0 = blocks[i].start
        for hd in range(SGU_HEADS):
            lo = hd * SGU_HEAD_DIM
            rhs = jnp.concatenate(
                [vn[b * SGU_BLOCK:(b + 1) * SGU_BLOCK, lo:lo + SGU_HEAD_DIM] for b in range(SGU_BPB)], axis=1)
            res = _dot(ws_ref[hd], rhs)
            for b in range(SGU_BPB):
                mix_ref[r0 + b * SGU_BLOCK:r0 + (b + 1) * SGU_BLOCK, lo:lo + SGU_HEAD_DIM] = (
                    res[:, b * SGU_HEAD_DIM:(b + 1) * SGU_HEAD_DIM] + bs_ref[:, lo:lo + SGU_HEAD_DIM])

    def gating(i):
        gate = st[i].pop("gate")
        y = _gelu(st[i].pop("u")) * mix_ref[blocks[i], :]
        st[i]["z"] = (y * (gate * _sigmoid(gate))).astype(_BF)

    def proj_out(i):
        st[i]["out"] = _dot(st[i].pop("z"), wout_ref[...])

    def norm_out(i):
        x2 = (x_ref[blocks[i], :] + d_ref[blocks[i], :]) + st[i].pop("out")
        o_ref[blocks[i], :] = _rms(x2, fg_ref[...])

    _wavefront([norm_in, proj_v, layer_norm, proj_ug, spatial, gating, proj_out, norm_out], len(blocks))


def _sgu_layer(x, delta, g, w_in, ln_g, ln_b, ws, bs, w_out, final_g):
    n = BATCH * SEQ
    grid = (n // SGU_TB,)
    row = lambda i: (0, 0)
    xblk = pl.BlockSpec((SGU_TB, D_MODEL), lambda i: (i, 0))
    return pl.pallas_call(
        _sgu_kernel,
        grid=grid,
        in_specs=[
            xblk,
            xblk,
            pl.BlockSpec((1, D_MODEL), row),
            pl.BlockSpec((D_MODEL, 3 * BRANCH), row),
            pl.BlockSpec((1, BRANCH), row),
            pl.BlockSpec((1, BRANCH), row),
            pl.BlockSpec((SGU_HEADS, SGU_BLOCK, SGU_BLOCK), lambda i: (0, 0, 0)),
            pl.BlockSpec((SGU_BLOCK, BRANCH), row),
            pl.BlockSpec((BRANCH, D_MODEL), row),
            pl.BlockSpec((1, D_MODEL), row),
        ],
        out_specs=xblk,
        out_shape=jax.ShapeDtypeStruct((n, D_MODEL), _F32),
        scratch_shapes=[pltpu.VMEM((SGU_TB, BRANCH), _F32)],
        compiler_params=pltpu.CompilerParams(
            dimension_semantics=("parallel",), vmem_limit_bytes=VMEM_LIMIT),
        name="sgu_layer",
    )(x, delta, g, w_in, ln_g, ln_b, ws, bs, w_out, final_g)


def kernel(x, norm_g, final_g, s5_w_in, s5_A_re, s5_A_im, s5_log_dt, s5_B_re, s5_B_im, s5_C_re, s5_C_im, s5_D, s5_w_glu, s5_b_glu, s5_w_out, sgu_w_in, sgu_ln_g, sgu_ln_b, sgu_w_s, sgu_b_s, sgu_w_out):
    kpad, mb, mc, dec_re, dec_im = _s5_operators(
        s5_A_re[0], s5_A_im[0], s5_log_dt[0], s5_B_re[0], s5_B_im[0], s5_C_re[0], s5_C_im[0], s5_D[0])
    u, gate = _s5_in(x, norm_g[0][None, :], s5_w_in[0].astype(_BF))
    y = _s5_mix(u, kpad, mb, mc, dec_re, dec_im)
    delta = _s5_out(y, gate, s5_w_glu[0].astype(_BF), s5_b_glu[0][None, :], s5_w_out[0].astype(_BF))

    mask = jnp.tril(jnp.ones((SGU_BLOCK, SGU_BLOCK), dtype=bool))
    ws = jnp.where(mask[None], sgu_w_s[0], 0.0).astype(_BF)
    bs = jnp.repeat(jnp.transpose(sgu_b_s[0]), SGU_HEAD_DIM, axis=1)
    out = _sgu_layer(x.reshape(BATCH * SEQ, D_MODEL), delta.reshape(BATCH * SEQ, D_MODEL),
                     norm_g[1][None, :], sgu_w_in[0].astype(_BF),
                     sgu_ln_g[0][None, :], sgu_ln_b[0][None, :], ws, bs,
                     sgu_w_out[0].astype(_BF), final_g[None, :])
    return out.reshape(BATCH, SEQ, D_MODEL)
```

```python
import math

import numpy as np
import jax
import jax.numpy as jnp
from jax import lax
from jax.experimental import pallas as pl
from jax.experimental.pallas import tpu as pltpu

D_MODEL = 1024
BATCH = 16
SEQ = 2048
BRANCH = D_MODEL
S5_GROUP = 16
S5_GROUPS = BRANCH // S5_GROUP
S5_STATE = 64
SGU_BLOCK = 128
SGU_HEADS = 8
SGU_HEAD_DIM = BRANCH // SGU_HEADS
RMS_EPS = 1e-6
LN_EPS = 1e-5
RE_CLIP = -1e-4

LANES = 128
TCH = 16
NCHUNK = SEQ // TCH
SLOTS = LANES // S5_GROUP
NJ = BRANCH // LANES
GW = TCH * S5_GROUP
PAIRS_PER_J = SLOTS // 2
VMEM_LIMIT = 56 * 1024 * 1024

_HI = lax.Precision.HIGHEST
_BF = jnp.bfloat16
_F32 = jnp.float32


_GELU_C1 = math.sqrt(2.0 / math.pi)
_GELU_C2 = _GELU_C1 * 0.044715


def _gelu(x):
    inner = x * (_GELU_C1 + _GELU_C2 * (x * x))
    return (0.5 * x) * (1.0 + jnp.tanh(inner))


def _sigmoid(x):
    return jax.nn.sigmoid(x)


def _dot(a, b):
    return jnp.dot(a, b, preferred_element_type=_F32)


def _wavefront(stages, n):
    for t in range(len(stages) + n - 1):
        for i in range(n):
            k = t - i
            if 0 <= k < len(stages):
                stages[k](i)


def _rms(x, g):
    return x * lax.rsqrt(jnp.mean(x * x, axis=-1, keepdims=True) + RMS_EPS) * g


KPAD = 2 * GW


def _slot_frame_table():
    a = np.arange(SLOTS)[:, None]
    x = np.arange(TCH)[None, :]
    half, k = x // SLOTS, x % SLOTS
    return SLOTS * half + (k - a) % SLOTS


_FRAME_OF_SLOT = _slot_frame_table()


def _s5_ops_kernel(lre_ref, lim_ref, ldr_ref, ldi_ref, btr_ref, bti_ref, cr_ref, ci_ref, d_ref,
                   kpad_ref, mb_ref, mc_ref, dre_ref, dim_ref):
    P = S5_STATE
    lane = lax.broadcasted_iota(jnp.int32, (S5_GROUP, LANES), 1)
    kcol = lax.broadcasted_iota(jnp.int32, (TCH, 1), 0)
    lane256 = lax.broadcasted_iota(jnp.int32, (S5_GROUP, GW), 1)
    row256 = lax.broadcasted_iota(jnp.int32, (S5_GROUP, GW), 0)
    lane_gw = lax.broadcasted_iota(jnp.int32, (GW, LANES), 1)
    nt = (((1,), (1,)), ((), ()))
    zeros64 = jnp.zeros((P, GW), _F32)
    st = [dict() for _ in range(SLOTS)]

    def outer(t_re, t_im, m_re, m_im):
        re = [t_re[k:k + 1] * m_re - t_im[k:k + 1] * m_im for k in range(TCH)]
        im = [t_re[k:k + 1] * m_im + t_im[k:k + 1] * m_re for k in range(TCH)]
        return jnp.concatenate(re, axis=0), jnp.concatenate(im, axis=0)

    def tables(a):
        lam_re, lam_im = lre_ref[a:a + 1, :], lim_ref[a:a + 1, :]
        ldr, ldi = ldr_ref[a:a + 1, :], ldi_ref[a:a + 1, :]

        def power(k):
            mag = jnp.exp(k * ldr)
            return mag * jnp.cos(k * ldi), mag * jnp.sin(k * ldi)

        frame = (kcol & SLOTS) + (((kcol & (SLOTS - 1)) + (SLOTS - a)) & (SLOTS - 1))
        pw_re, pw_im = power(kcol.astype(_F32))
        pi_re, pi_im = power((TCH - 1 - frame).astype(_F32))
        po_re, po_im = power((frame + 1).astype(_F32))
        st[a]["p16"] = power(jnp.full((1, 1), float(TCH), _F32))

        nr, ni = pw_re[1:2] - 1.0, pw_im[1:2]
        den = lam_re * lam_re + lam_im * lam_im
        cf_re = (nr * lam_re + ni * lam_im) / den
        cf_im = (ni * lam_re - nr * lam_im) / den
        b_re, b_im = btr_ref[a], bti_ref[a]
        bb_re = cf_re * b_re - cf_im * b_im
        bb_im = cf_re * b_im + cf_im * b_re
        c_re, c_im = cr_ref[a], ci_ref[a]
        st[a]["bb"] = (bb_re, bb_im)
        st[a]["lc"] = outer(pw_re, pw_im, c_re, c_im)
        st[a]["in"] = outer(pi_re, pi_im, bb_re, bb_im)
        st[a]["ot"] = outer(po_re, po_im, c_re, c_im)

    def taps(a):
        bb_re, bb_im = st[a].pop("bb")
        lc_re, lc_im = st[a].pop("lc")
        once = lane < P
        k = (lax.dot_general(jnp.where(once, bb_re, 0.0), lc_re, nt, precision=_HI, preferred_element_type=_F32)
             - lax.dot_general(jnp.where(once, bb_im, 0.0), lc_im, nt, precision=_HI, preferred_element_type=_F32))
        kpad_ref[a, :, 0:GW] = jnp.zeros((S5_GROUP, GW), _F32)
        kpad_ref[a, :, GW:2 * GW] = k + jnp.where(lane256 == row256, d_ref[a], 0.0)

    def state_in(a):
        q, m = a // 2, a % 2
        mine = (lane_gw >= P) if m else (lane_gw < P)
        in_re, in_im = st[a].pop("in")
        mb_ref[q, m * GW:(m + 1) * GW, 0:LANES] = jnp.where(mine, in_re, 0.0).astype(_BF)
        mb_ref[q, m * GW:(m + 1) * GW, LANES:2 * LANES] = jnp.where(mine, in_im, 0.0).astype(_BF)

    def state_out(a):
        q, m = a // 2, a % 2
        ot_re, ot_im = st[a].pop("ot")
        o_re, o_im = ot_re.T[0:P], ot_im.T[0:P]
        blocks = [zeros64, zeros64, zeros64, zeros64]
        blocks[m], blocks[2 + m] = o_re, -o_im
        mc_ref[a] = jnp.concatenate(blocks, axis=0).astype(_BF)
        if m:
            first = lax.broadcasted_iota(jnp.int32, (1, LANES), 1) < P
            dre_ref[q] = jnp.where(first, st[a - 1]["p16"][0], st[a]["p16"][0])
            dim_ref[q] = jnp.where(first, st[a - 1]["p16"][1], st[a]["p16"][1])

    _wavefront([tables, taps, state_in, state_out], SLOTS)


def _s5_operators(A_re, A_im, log_dt, B_re, B_im, C_re, C_im, D):
    f32 = jnp.float32
    G, P, H = S5_GROUPS, S5_STATE, S5_GROUP
    lam_re = jnp.minimum(A_re.astype(f32), RE_CLIP)
    lam_im = A_im.astype(f32)
    dt = jnp.exp(log_dt.astype(f32))[:, None]

    twice = lambda t: jnp.concatenate([t, t], axis=-1)
    tables = [twice(lam_re), twice(lam_im), twice(lam_re * dt), twice(lam_im * dt)]
    mats = [twice(jnp.swapaxes(B_re.astype(f32), 1, 2)), twice(jnp.swapaxes(B_im.astype(f32), 1, 2)),
            twice(C_re.astype(f32)), twice(C_im.astype(f32))]
    tspec = pl.BlockSpec((SLOTS, LANES), lambda j: (j, 0))
    mspec = pl.BlockSpec((SLOTS, H, LANES), lambda j: (j, 0, 0))
    dspec = pl.BlockSpec((PAIRS_PER_J, 1, LANES), lambda j: (j, 0, 0))
    return pl.pallas_call(
        _s5_ops_kernel,
        grid=(NJ,),
        in_specs=[tspec] * 4 + [mspec] * 4 + [pl.BlockSpec((SLOTS, H, 1), lambda j: (j, 0, 0))],
        out_specs=[
            pl.BlockSpec((SLOTS, H, KPAD), lambda j: (j, 0, 0)),
            pl.BlockSpec((PAIRS_PER_J, 2 * GW, GW), lambda j: (j, 0, 0)),
            pl.BlockSpec((SLOTS, GW, GW), lambda j: (j, 0, 0)),
            dspec, dspec,
        ],
        out_shape=[
            jax.ShapeDtypeStruct((G, H, KPAD), f32),
            jax.ShapeDtypeStruct((G // 2, 2 * GW, GW), _BF),
            jax.ShapeDtypeStruct((G, GW, GW), _BF),
            jax.ShapeDtypeStruct((G // 2, 1, LANES), f32),
            jax.ShapeDtypeStruct((G // 2, 1, LANES), f32),
        ],
        compiler_params=pltpu.CompilerParams(dimension_semantics=("parallel",), vmem_limit_bytes=VMEM_LIMIT),
        name="s5_ops",
    )(*tables, *mats, D.astype(f32).reshape(G, H, 1))


IN_CT = 8


IN_MB = 1


def _s5_in_kernel(x_ref, g_ref, w_ref, u_ref, gate_ref):
    rows = BATCH * TCH
    nb = IN_CT // IN_MB
    st = [dict() for _ in range(nb)]

    def norm(i):
        parts = []
        for c in range(i * IN_MB, (i + 1) * IN_MB):
            xc = jnp.swapaxes(x_ref[:, c * TCH:(c + 1) * TCH, :], 0, 1).reshape(rows, D_MODEL)
            parts.append(_rms(xc, g_ref[...]).astype(_BF))
        st[i]["hp"] = jnp.concatenate(parts, axis=0)

    def gate_mm(i):
        gate = _dot(st[i]["hp"], w_ref[:, BRANCH:])
        gate_ref[i * IN_MB:(i + 1) * IN_MB] = (
            (gate * _sigmoid(gate)).astype(_BF).reshape(IN_MB, TCH, BATCH, BRANCH))

    def u_mm(i):
        u = _dot(st[i].pop("hp"), w_ref[:, :BRANCH])
        u_ref[i * IN_MB:(i + 1) * IN_MB] = u.astype(_BF).reshape(IN_MB, TCH, BATCH, BRANCH)

    _wavefront([norm, gate_mm, u_mm], nb)


def _s5_in(x, g, w_in):
    grid = (NCHUNK // IN_CT,)
    blk4 = (IN_CT, TCH, BATCH, BRANCH)
    return pl.pallas_call(
        _s5_in_kernel,
        grid=grid,
        in_specs=[
            pl.BlockSpec((BATCH, IN_CT * TCH, D_MODEL), lambda i: (0, i, 0)),
            pl.BlockSpec((1, D_MODEL), lambda i: (0, 0)),
            pl.BlockSpec((D_MODEL, 2 * BRANCH), lambda i: (0, 0)),
        ],
        out_specs=[
            pl.BlockSpec(blk4, lambda i: (i, 0, 0, 0)),
            pl.BlockSpec(blk4, lambda i: (i, 0, 0, 0)),
        ],
        out_shape=[
            jax.ShapeDtypeStruct((NCHUNK, TCH, BATCH, BRANCH), _BF),
            jax.ShapeDtypeStruct((NCHUNK, TCH, BATCH, BRANCH), _BF),
        ],
        compiler_params=pltpu.CompilerParams(
            dimension_semantics=("parallel",), vmem_limit_bytes=VMEM_LIMIT),
        name="s5_in",
    )(x, g, w_in)


MIX_CT = 64
MIX_R = MIX_CT * BATCH
MIX_CB = 4
MIX_RT = MIX_CB * BATCH
MIX_NH = 4
MIX_HC = MIX_CT // MIX_NH
MIX_HR = MIX_HC * BATCH

def _gather_slots(v, slot):
    rolled = [v[0]] + [pltpu.roll(v[k], k * S5_GROUP, axis=1) for k in range(1, SLOTS)]
    out = []
    for a in range(SLOTS):
        acc = rolled[0]
        for k in range(1, SLOTS):
            acc = jnp.where(slot == (a + k) % SLOTS, rolled[k], acc)
        out.append(acc)
    return out


def _scatter_slots(v, slot):
    out = []
    for k in range(SLOTS):
        acc = v[0]
        for a in range(1, SLOTS):
            acc = jnp.where(slot == (a + k) % SLOTS, v[a], acc)
        out.append(acc if k == 0 else pltpu.roll(acc, (SLOTS - k) * S5_GROUP, axis=1))
    return out


def _column_perms():
    p = np.zeros((SLOTS, GW, GW), np.float32)
    for a in range(SLOTS):
        for x in range(TCH):
            f = int(_FRAME_OF_SLOT[a, x])
            for c in range(S5_GROUP):
                p[a, f * S5_GROUP + c, x * S5_GROUP + c] = 1.0
    return p


_CPERM = _column_perms()


def _s5_mix_kernel(u_ref, kpad_ref, mb_ref, mc_ref, cperm_ref, dre_ref, dim_ref, y_ref,
                   op_ref, xs_ref, ys_ref, ss_ref, hs_ref, carry_ref):
    ct = pl.program_id(1)

    @pl.when(ct == 0)
    def _():
        carry_ref[...] = jnp.zeros_like(carry_ref)
        for a in range(SLOTS):
            kp = kpad_ref[a]
            rows = []
            for x in range(TCH):
                off = (TCH - int(_FRAME_OF_SLOT[a, x])) * S5_GROUP
                rows.append(kp[:, off:off + GW])
            toep = jnp.concatenate(rows, axis=0).astype(_BF)
            op_ref[a, 0:GW, :] = _dot(toep, cperm_ref[a]).astype(_BF)
            op_ref[a, GW:2 * GW, :] = mc_ref[a]

    slot = lax.broadcasted_iota(jnp.int32, (MIX_RT, LANES), 1) // S5_GROUP
    dre = [dre_ref[q] for q in range(PAIRS_PER_J)]
    dim_ = [dim_ref[q] for q in range(PAIRS_PER_J)]

    def relayout_in(c0):
        r0 = c0 * BATCH
        for half in range(2):
            pieces = [u_ref[c0:c0 + MIX_CB, SLOTS * half + k, :, :].reshape(MIX_RT, LANES) for k in range(SLOTS)]
            groups = _gather_slots(pieces, slot)
            for a in range(SLOTS):
                lo = (a % 2) * GW + half * LANES
                xs_ref[a // 2, r0:r0 + MIX_RT, lo:lo + LANES] = groups[a]

    def relayout_out(c0):
        r0 = c0 * BATCH
        for half in range(2):
            groups = [ys_ref[a, r0:r0 + MIX_RT, half * LANES:(half + 1) * LANES].astype(_BF) for a in range(SLOTS)]
            frames = _scatter_slots(groups, slot)
            for k in range(SLOTS):
                y_ref[c0:c0 + MIX_CB, SLOTS * half + k, :, :] = frames[k].reshape(MIX_CB, BATCH, LANES)

    def recurrence(c0, state):
        for c in range(c0, c0 + MIX_HC):
            r0 = c * BATCH
            new = []
            for q in range(PAIRS_PER_J):
                hre, him = state[2 * q], state[2 * q + 1]
                hs_ref[q, r0:r0 + BATCH, 0:LANES] = hre.astype(_BF)
                hs_ref[q, r0:r0 + BATCH, LANES:2 * LANES] = him.astype(_BF)
                sre = ss_ref[q, r0:r0 + BATCH, 0:LANES]
                sim = ss_ref[q, r0:r0 + BATCH, LANES:2 * LANES]
                new.append(dre[q] * hre - dim_[q] * him + sre)
                new.append(dre[q] * him + dim_[q] * hre + sim)
            state = new
        return state

    blocks = [(h * MIX_HC, h * MIX_HC * BATCH) for h in range(MIX_NH)]
    for c0, r0 in blocks:
        for t in range(MIX_HC // MIX_CB):
            relayout_in(c0 + t * MIX_CB)
        for q in range(PAIRS_PER_J):
            ss_ref[q, r0:r0 + MIX_HR, :] = _dot(xs_ref[q, r0:r0 + MIX_HR, :], mb_ref[q])
    state = [carry_ref[q, k] for q in range(PAIRS_PER_J) for k in range(2)]
    for c0, r0 in blocks:
        state = recurrence(c0, state)
    for q in range(PAIRS_PER_J):
        carry_ref[q, 0] = state[2 * q]
        carry_ref[q, 1] = state[2 * q + 1]
    for c0, r0 in blocks:
        for a in range(SLOTS):
            q, m = a // 2, a % 2
            lhs = jnp.concatenate([xs_ref[q, r0:r0 + MIX_HR, m * GW:(m + 1) * GW], hs_ref[q, r0:r0 + MIX_HR, :]], axis=1)
            ys_ref[a, r0:r0 + MIX_HR, :] = _dot(lhs, op_ref[a])
    for c0, r0 in blocks:
        for t in range(MIX_HC // MIX_CB):
            relayout_out(c0 + t * MIX_CB)


def _s5_mix(u, kpad, mb, mc, dec_re, dec_im):
    grid = (NJ, NCHUNK // MIX_CT)
    blk = (MIX_CT, TCH, BATCH, LANES)
    dspec = pl.BlockSpec((PAIRS_PER_J, 1, LANES), lambda j, c: (j, 0, 0))
    return pl.pallas_call(
        _s5_mix_kernel,
        grid=grid,
        in_specs=[
            pl.BlockSpec(blk, lambda j, c: (c, 0, 0, j)),
            pl.BlockSpec((SLOTS, S5_GROUP, KPAD), lambda j, c: (j, 0, 0)),
            pl.BlockSpec((PAIRS_PER_J, 2 * GW, GW), lambda j, c: (j, 0, 0)),
            pl.BlockSpec((SLOTS, GW, GW), lambda j, c: (j, 0, 0)),
            pl.BlockSpec((SLOTS, GW, GW), lambda j, c: (0, 0, 0)),
            dspec, dspec,
        ],
        out_specs=pl.BlockSpec(blk, lambda j, c: (c, 0, 0, j)),
        out_shape=jax.ShapeDtypeStruct((NCHUNK, TCH, BATCH, BRANCH), _BF),
        scratch_shapes=[
            pltpu.VMEM((SLOTS, 2 * GW, GW), _BF),
            pltpu.VMEM((PAIRS_PER_J, MIX_R, 2 * GW), _BF),
            pltpu.VMEM((SLOTS, MIX_R, GW), _F32),
            pltpu.VMEM((PAIRS_PER_J, MIX_R, GW), _F32),
            pltpu.VMEM((PAIRS_PER_J, MIX_R, GW), _BF),
            pltpu.VMEM((PAIRS_PER_J, 2, BATCH, LANES), _F32),
        ],
        compiler_params=pltpu.CompilerParams(
            dimension_semantics=("parallel", "arbitrary"), vmem_limit_bytes=VMEM_LIMIT),
        name="s5_mix",
    )(u, kpad, mb, mc, jnp.asarray(_CPERM, _BF), dec_re, dec_im)


def _s5_out_kernel(y_ref, gate_ref, wglu_ref, bglu_ref, wout_ref, o_ref):
    rows = IN_MB * BATCH * TCH
    nb = IN_CT // IN_MB
    st = [dict() for _ in range(nb)]

    def act(i):
        st[i]["y"] = _gelu(y_ref[i * IN_MB:(i + 1) * IN_MB].reshape(rows, BRANCH).astype(_F32))

    def glu_mm(i):
        st[i]["glu"] = _dot(st[i]["y"].astype(_BF), wglu_ref[...])

    def gating(i):
        y = st[i].pop("y") * _sigmoid(st[i].pop("glu") + bglu_ref[...])
        sg = gate_ref[i * IN_MB:(i + 1) * IN_MB].reshape(rows, BRANCH).astype(_F32)
        st[i]["z"] = (y * sg).astype(_BF)

    def out_mm(i):
        st[i]["out"] = _dot(st[i].pop("z"), wout_ref[...])

    def residual(i):
        out = st[i].pop("out").reshape(IN_MB, TCH, BATCH, D_MODEL)
        for k in range(IN_MB):
            c = i * IN_MB + k
            o_ref[:, c * TCH:(c + 1) * TCH, :] = jnp.swapaxes(out[k], 0, 1)

    _wavefront([act, glu_mm, gating, out_mm, residual], nb)


def _s5_out(y, gate, w_glu, b_glu, w_out):
    grid = (NCHUNK // IN_CT,)
    blk4 = (IN_CT, TCH, BATCH, BRANCH)
    xblk = pl.BlockSpec((BATCH, IN_CT * TCH, D_MODEL), lambda i: (0, i, 0))
    return pl.pallas_call(
        _s5_out_kernel,
        grid=grid,
        in_specs=[
            pl.BlockSpec(blk4, lambda i: (i, 0, 0, 0)),
            pl.BlockSpec(blk4, lambda i: (i, 0, 0, 0)),
            pl.BlockSpec((BRANCH, BRANCH), lambda i: (0, 0)),
            pl.BlockSpec((1, BRANCH), lambda i: (0, 0)),
            pl.BlockSpec((BRANCH, D_MODEL), lambda i: (0, 0)),
        ],
        out_specs=xblk,
        out_shape=jax.ShapeDtypeStruct((BATCH, SEQ, D_MODEL), _F32),
        compiler_params=pltpu.CompilerParams(
            dimension_semantics=("parallel",), vmem_limit_bytes=VMEM_LIMIT),
        name="s5_out",
    )(y, gate, w_glu, b_glu, w_out)


SGU_TB = 1024
SGU_NB = SGU_TB // SGU_BLOCK


SGU_RB = 512
SGU_BPB = SGU_RB // SGU_BLOCK


def _sgu_kernel(x_ref, d_ref, g_ref, win_ref, lng_ref, lnb_ref, ws_ref, bs_ref, wout_ref, fg_ref, o_ref, mix_ref):
    blocks = [slice(r0, r0 + SGU_RB) for r0 in range(0, SGU_TB, SGU_RB)]
    st = [dict() for _ in blocks]

    def norm_in(i):
        st[i]["h"] = _rms(x_ref[blocks[i], :] + d_ref[blocks[i], :], g_ref[...]).astype(_BF)

    def proj_v(i):
        st[i]["v"] = _dot(st[i]["h"], win_ref[:, BRANCH:2 * BRANCH])

    def proj_ug(i):
        st[i]["u"] = _dot(st[i]["h"], win_ref[:, :BRANCH])
        st[i]["gate"] = _dot(st[i]["h"], win_ref[:, 2 * BRANCH:])

    def layer_norm(i):
        v = _gelu(st[i].pop("v"))
        mu = jnp.mean(v, axis=-1, keepdims=True)
        vc = v - mu
        var = jnp.mean(vc * vc, axis=-1, keepdims=True)
        st[i]["vn"] = (vc * lax.rsqrt(var + LN_EPS) * lng_ref[...] + lnb_ref[...]).astype(_BF)

    def spatial(i):
        vn = st[i].pop("vn")
        r0 = blocks[i].start
        for hd in range(SGU_HEADS):
            lo = hd * SGU_HEAD_DIM
            rhs = jnp.concatenate(
                [vn[b * SGU_BLOCK:(b + 1) * SGU_BLOCK, lo:lo + SGU_HEAD_DIM] for b in range(SGU_BPB)], axis=1)
            res = _dot(ws_ref[hd], rhs)
            for b in range(SGU_BPB):
                mix_ref[r0 + b * SGU_BLOCK:r0 + (b + 1) * SGU_BLOCK, lo:lo + SGU_HEAD_DIM] = (
                    res[:, b * SGU_HEAD_DIM:(b + 1) * SGU_HEAD_DIM] + bs_ref[:, lo:lo + SGU_HEAD_DIM])

    def gating(i):
        gate = st[i].pop("gate")
        y = _gelu(st[i].pop("u")) * mix_ref[blocks[i], :]
        st[i]["z"] = (y * (gate * _sigmoid(gate))).astype(_BF)

    def proj_out(i):
        st[i]["out"] = _dot(st[i].pop("z"), wout_ref[...])

    def norm_out(i):
        x2 = (x_ref[blocks[i], :] + d_ref[blocks[i], :]) + st[i].pop("out")
        o_ref[blocks[i], :] = _rms(x2, fg_ref[...])

    _wavefront([norm_in, proj_v, layer_norm, proj_ug, spatial, gating, proj_out, norm_out], len(blocks))


def _sgu_layer(x, delta, g, w_in, ln_g, ln_b, ws, bs, w_out, final_g):
    n = BATCH * SEQ
    grid = (n // SGU_TB,)
    row = lambda i: (0, 0)
    xblk = pl.BlockSpec((SGU_TB, D_MODEL), lambda i: (i, 0))
    return pl.pallas_call(
        _sgu_kernel,
        grid=grid,
        in_specs=[
            xblk,
            xblk,
            pl.BlockSpec((1, D_MODEL), row),
            pl.BlockSpec((D_MODEL, 3 * BRANCH), row),
            pl.BlockSpec((1, BRANCH), row),
            pl.BlockSpec((1, BRANCH), row),
            pl.BlockSpec((SGU_HEADS, SGU_BLOCK, SGU_BLOCK), lambda i: (0, 0, 0)),
            pl.BlockSpec((SGU_BLOCK, BRANCH), row),
            pl.BlockSpec((BRANCH, D_MODEL), row),
            pl.BlockSpec((1, D_MODEL), row),
        ],
        out_specs=xblk,
        out_shape=jax.ShapeDtypeStruct((n, D_MODEL), _F32),
        scratch_shapes=[pltpu.VMEM((SGU_TB, BRANCH), _F32)],
        compiler_params=pltpu.CompilerParams(
            dimension_semantics=("parallel",), vmem_limit_bytes=VMEM_LIMIT),
        name="sgu_layer",
    )(x, delta, g, w_in, ln_g, ln_b, ws, bs, w_out, final_g)


def kernel(x, norm_g, final_g, s5_w_in, s5_A_re, s5_A_im, s5_log_dt, s5_B_re, s5_B_im, s5_C_re, s5_C_im, s5_D, s5_w_glu, s5_b_glu, s5_w_out, sgu_w_in, sgu_ln_g, sgu_ln_b, sgu_w_s, sgu_b_s, sgu_w_out):
    kpad, mb, mc, dec_re, dec_im = _s5_operators(
        s5_A_re[0], s5_A_im[0], s5_log_dt[0], s5_B_re[0], s5_B_im[0], s5_C_re[0], s5_C_im[0], s5_D[0])
    u, gate = _s5_in(x, norm_g[0][None, :], s5_w_in[0].astype(_BF))
    y = _s5_mix(u, kpad, mb, mc, dec_re, dec_im)
    delta = _s5_out(y, gate, s5_w_glu[0].astype(_BF), s5_b_glu[0][None, :], s5_w_out[0].astype(_BF))

    mask = jnp.tril(jnp.ones((SGU_BLOCK, SGU_BLOCK), dtype=bool))
    ws = jnp.where(mask[None], sgu_w_s[0], 0.0).astype(_BF)
    bs = jnp.repeat(jnp.transpose(sgu_b_s[0]), SGU_HEAD_DIM, axis=1)
    out = _sgu_layer(x.reshape(BATCH * SEQ, D_MODEL), delta.reshape(BATCH * SEQ, D_MODEL),
                     norm_g[1][None, :], sgu_w_in[0].astype(_BF),
                     sgu_ln_g[0][None, :], sgu_ln_b[0][None, :], ws, bs,
                     sgu_w_out[0].astype(_BF), final_g[None, :])
    return out.reshape(BATCH, SEQ, D_MODEL)
```

```python
import math

import numpy as np
import jax
import jax.numpy as jnp
from jax import lax
from jax.experimental import pallas as pl
from jax.experimental.pallas import tpu as pltpu

D_MODEL = 1024
BATCH = 16
SEQ = 2048
BRANCH = D_MODEL
S5_GROUP = 16
S5_GROUPS = BRANCH // S5_GROUP
S5_STATE = 64
SGU_BLOCK = 128
SGU_HEADS = 8
SGU_HEAD_DIM = BRANCH // SGU_HEADS
RMS_EPS = 1e-6
LN_EPS = 1e-5
RE_CLIP = -1e-4

LANES = 128
TCH = 16
NCHUNK = SEQ // TCH
SLOTS = LANES // S5_GROUP
NJ = BRANCH // LANES
GW = TCH * S5_GROUP
PAIRS_PER_J = SLOTS // 2
VMEM_LIMIT = 56 * 1024 * 1024

_HI = lax.Precision.HIGHEST
_BF = jnp.bfloat16
_F32 = jnp.float32


_GELU_C1 = math.sqrt(2.0 / math.pi)
_GELU_C2 = _GELU_C1 * 0.044715


def _gelu(x):
    inner = x * (_GELU_C1 + _GELU_C2 * (x * x))
    return (0.5 * x) * (1.0 + jnp.tanh(inner))


def _sigmoid(x):
    return jax.nn.sigmoid(x)


def _dot(a, b):
    return jnp.dot(a, b, preferred_element_type=_F32)


def _wavefront(stages, n):
    for t in range(len(stages) + n - 1):
        for i in range(n):
            k = t - i
            if 0 <= k < len(stages):
                stages[k](i)


def _rms(x, g):
    return x * lax.rsqrt(jnp.mean(x * x, axis=-1, keepdims=True) + RMS_EPS) * g


KPAD = 2 * GW


def _slot_frame_table():
    a = np.arange(SLOTS)[:, None]
    x = np.arange(TCH)[None, :]
    half, k = x // SLOTS, x % SLOTS
    return SLOTS * half + (k - a) % SLOTS


_FRAME_OF_SLOT = _slot_frame_table()


def _s5_ops_kernel(lre_ref, lim_ref, ldr_ref, ldi_ref, btr_ref, bti_ref, cr_ref, ci_ref, d_ref,
                   kpad_ref, mb_ref, mc_ref, dre_ref, dim_ref):
    P = S5_STATE
    lane = lax.broadcasted_iota(jnp.int32, (S5_GROUP, LANES), 1)
    kcol = lax.broadcasted_iota(jnp.int32, (TCH, 1), 0)
    lane256 = lax.broadcasted_iota(jnp.int32, (S5_GROUP, GW), 1)
    row256 = lax.broadcasted_iota(jnp.int32, (S5_GROUP, GW), 0)
    lane_gw = lax.broadcasted_iota(jnp.int32, (GW, LANES), 1)
    nt = (((1,), (1,)), ((), ()))
    zeros64 = jnp.zeros((P, GW), _F32)
    st = [dict() for _ in range(SLOTS)]

    def outer(t_re, t_im, m_re, m_im):
        re = [t_re[k:k + 1] * m_re - t_im[k:k + 1] * m_im for k in range(TCH)]
        im = [t_re[k:k + 1] * m_im + t_im[k:k + 1] * m_re for k in range(TCH)]
        return jnp.concatenate(re, axis=0), jnp.concatenate(im, axis=0)

    def tables(a):
        lam_re, lam_im = lre_ref[a:a + 1, :], lim_ref[a:a + 1, :]
        ldr, ldi = ldr_ref[a:a + 1, :], ldi_ref[a:a + 1, :]

        def power(k):
            mag = jnp.exp(k * ldr)
            return mag * jnp.cos(k * ldi), mag * jnp.sin(k * ldi)

        frame = (kcol & SLOTS) + (((kcol & (SLOTS - 1)) + (SLOTS - a)) & (SLOTS - 1))
        pw_re, pw_im = power(kcol.astype(_F32))
        pi_re, pi_im = power((TCH - 1 - frame).astype(_F32))
        po_re, po_im = power((frame + 1).astype(_F32))
        st[a]["p16"] = power(jnp.full((1, 1), float(TCH), _F32))

        nr, ni = pw_re[1:2] - 1.0, pw_im[1:2]
        den = lam_re * lam_re + lam_im * lam_im
        cf_re = (nr * lam_re + ni * lam_im) / den
        cf_im = (ni * lam_re - nr * lam_im) / den
        b_re, b_im = btr_ref[a], bti_ref[a]
        bb_re = cf_re * b_re - cf_im * b_im
        bb_im = cf_re * b_im + cf_im * b_re
        c_re, c_im = cr_ref[a], ci_ref[a]
        st[a]["bb"] = (bb_re, bb_im)
        st[a]["lc"] = outer(pw_re, pw_im, c_re, c_im)
        st[a]["in"] = outer(pi_re, pi_im, bb_re, bb_im)
        st[a]["ot"] = outer(po_re, po_im, c_re, c_im)

    def taps(a):
        bb_re, bb_im = st[a].pop("bb")
        lc_re, lc_im = st[a].pop("lc")
        once = lane < P
        k = (lax.dot_general(jnp.where(once, bb_re, 0.0), lc_re, nt, precision=_HI, preferred_element_type=_F32)
             - lax.dot_general(jnp.where(once, bb_im, 0.0), lc_im, nt, precision=_HI, preferred_element_type=_F32))
        kpad_ref[a, :, 0:GW] = jnp.zeros((S5_GROUP, GW), _F32)
        kpad_ref[a, :, GW:2 * GW] = k + jnp.where(lane256 == row256, d_ref[a], 0.0)

    def state_in(a):
        q, m = a // 2, a % 2
        mine = (lane_gw >= P) if m else (lane_gw < P)
        in_re, in_im = st[a].pop("in")
        mb_ref[q, m * GW:(m + 1) * GW, 0:LANES] = jnp.where(mine, in_re, 0.0).astype(_BF)
        mb_ref[q, m * GW:(m + 1) * GW, LANES:2 * LANES] = jnp.where(mine, in_im, 0.0).astype(_BF)

    def state_out(a):
        q, m = a // 2, a % 2
        ot_re, ot_im = st[a].pop("ot")
        o_re, o_im = ot_re.T[0:P], ot_im.T[0:P]
        blocks = [zeros64, zeros64, zeros64, zeros64]
        blocks[m], blocks[2 + m] = o_re, -o_im
        mc_ref[a] = jnp.concatenate(blocks, axis=0).astype(_BF)
        if m:
            first = lax.broadcasted_iota(jnp.int32, (1, LANES), 1) < P
            dre_ref[q] = jnp.where(first, st[a - 1]["p16"][0], st[a]["p16"][0])
            dim_ref[q] = jnp.where(first, st[a - 1]["p16"][1], st[a]["p16"][1])

    _wavefront([tables, taps, state_in, state_out], SLOTS)


def _s5_operators(A_re, A_im, log_dt, B_re, B_im, C_re, C_im, D):
    f32 = jnp.float32
    G, P, H = S5_GROUPS, S5_STATE, S5_GROUP
    lam_re = jnp.minimum(A_re.astype(f32), RE_CLIP)
    lam_im = A_im.astype(f32)
    dt = jnp.exp(log_dt.astype(f32))[:, None]

    twice = lambda t: jnp.concatenate([t, t], axis=-1)
    tables = [twice(lam_re), twice(lam_im), twice(lam_re * dt), twice(lam_im * dt)]
    mats = [twice(jnp.swapaxes(B_re.astype(f32), 1, 2)), twice(jnp.swapaxes(B_im.astype(f32), 1, 2)),
            twice(C_re.astype(f32)), twice(C_im.astype(f32))]
    tspec = pl.BlockSpec((SLOTS, LANES), lambda j: (j, 0))
    mspec = pl.BlockSpec((SLOTS, H, LANES), lambda j: (j, 0, 0))
    dspec = pl.BlockSpec((PAIRS_PER_J, 1, LANES), lambda j: (j, 0, 0))
    return pl.pallas_call(
        _s5_ops_kernel,
        grid=(NJ,),
        in_specs=[tspec] * 4 + [mspec] * 4 + [pl.BlockSpec((SLOTS, H, 1), lambda j: (j, 0, 0))],
        out_specs=[
            pl.BlockSpec((SLOTS, H, KPAD), lambda j: (j, 0, 0)),
            pl.BlockSpec((PAIRS_PER_J, 2 * GW, GW), lambda j: (j, 0, 0)),
            pl.BlockSpec((SLOTS, GW, GW), lambda j: (j, 0, 0)),
            dspec, dspec,
        ],
        out_shape=[
            jax.ShapeDtypeStruct((G, H, KPAD), f32),
            jax.ShapeDtypeStruct((G // 2, 2 * GW, GW), _BF),
            jax.ShapeDtypeStruct((G, GW, GW), _BF),
            jax.ShapeDtypeStruct((G // 2, 1, LANES), f32),
            jax.ShapeDtypeStruct((G // 2, 1, LANES), f32),
        ],
        compiler_params=pltpu.CompilerParams(dimension_semantics=("parallel",), vmem_limit_bytes=VMEM_LIMIT),
        name="s5_ops",
    )(*tables, *mats, D.astype(f32).reshape(G, H, 1))


IN_CT = 8


IN_MB = 1


def _s5_in_kernel(x_ref, g_ref, wf_ref, u_ref, gate_ref, w_ref):
    @pl.when(pl.program_id(0) == 0)
    def _():
        w_ref[...] = wf_ref[...].astype(_BF)

    rows = BATCH * TCH
    nb = IN_CT // IN_MB
    st = [dict() for _ in range(nb)]

    def norm(i):
        parts = []
        for c in range(i * IN_MB, (i + 1) * IN_MB):
            xc = jnp.swapaxes(x_ref[:, c * TCH:(c + 1) * TCH, :], 0, 1).reshape(rows, D_MODEL)
            parts.append(_rms(xc, g_ref[...]).astype(_BF))
        st[i]["hp"] = jnp.concatenate(parts, axis=0)

    def gate_mm(i):
        gate = _dot(st[i]["hp"], w_ref[:, BRANCH:])
        gate_ref[i * IN_MB:(i + 1) * IN_MB] = (
            (gate * _sigmoid(gate)).astype(_BF).reshape(IN_MB, TCH, BATCH, BRANCH))

    def u_mm(i):
        u = _dot(st[i].pop("hp"), w_ref[:, :BRANCH])
        u_ref[i * IN_MB:(i + 1) * IN_MB] = u.astype(_BF).reshape(IN_MB, TCH, BATCH, BRANCH)

    _wavefront([norm, gate_mm, u_mm], nb)


def _s5_in(x, g, w_in):
    grid = (NCHUNK // IN_CT,)
    blk4 = (IN_CT, TCH, BATCH, BRANCH)
    return pl.pallas_call(
        _s5_in_kernel,
        grid=grid,
        in_specs=[
            pl.BlockSpec((BATCH, IN_CT * TCH, D_MODEL), lambda i: (0, i, 0)),
            pl.BlockSpec((1, D_MODEL), lambda i: (0, 0)),
            pl.BlockSpec((D_MODEL, 2 * BRANCH), lambda i: (0, 0)),
        ],
        out_specs=[
            pl.BlockSpec(blk4, lambda i: (i, 0, 0, 0)),
            pl.BlockSpec(blk4, lambda i: (i, 0, 0, 0)),
        ],
        out_shape=[
            jax.ShapeDtypeStruct((NCHUNK, TCH, BATCH, BRANCH), _BF),
            jax.ShapeDtypeStruct((NCHUNK, TCH, BATCH, BRANCH), _BF),
        ],
        scratch_shapes=[pltpu.VMEM((D_MODEL, 2 * BRANCH), _BF)],
        compiler_params=pltpu.CompilerParams(
            dimension_semantics=("arbitrary",), vmem_limit_bytes=VMEM_LIMIT),
        name="s5_in",
    )(x, g, w_in)


MIX_CT = 64
MIX_R = MIX_CT * BATCH
MIX_CB = 4
MIX_RT = MIX_CB * BATCH
MIX_NH = 4
MIX_HC = MIX_CT // MIX_NH
MIX_HR = MIX_HC * BATCH

def _gather_slots(v, slot):
    rolled = [v[0]] + [pltpu.roll(v[k], k * S5_GROUP, axis=1) for k in range(1, SLOTS)]
    out = []
    for a in range(SLOTS):
        acc = rolled[0]
        for k in range(1, SLOTS):
            acc = jnp.where(slot == (a + k) % SLOTS, rolled[k], acc)
        out.append(acc)
    return out


def _scatter_slots(v, slot):
    out = []
    for k in range(SLOTS):
        acc = v[0]
        for a in range(1, SLOTS):
            acc = jnp.where(slot == (a + k) % SLOTS, v[a], acc)
        out.append(acc if k == 0 else pltpu.roll(acc, (SLOTS - k) * S5_GROUP, axis=1))
    return out


def _column_perms():
    p = np.zeros((SLOTS, GW, GW), np.float32)
    for a in range(SLOTS):
        for x in range(TCH):
            f = int(_FRAME_OF_SLOT[a, x])
            for c in range(S5_GROUP):
                p[a, f * S5_GROUP + c, x * S5_GROUP + c] = 1.0
    return p


_CPERM = _column_perms()


def _s5_mix_kernel(u_ref, kpad_ref, mb_ref, mc_ref, cperm_ref, dre_ref, dim_ref, y_ref,
                   op_ref, xs_ref, ys_ref, ss_ref, hs_ref, carry_ref):
    ct = pl.program_id(1)

    @pl.when(ct == 0)
    def _():
        carry_ref[...] = jnp.zeros_like(carry_ref)
        for a in range(SLOTS):
            kp = kpad_ref[a]
            rows = []
            for x in range(TCH):
                off = (TCH - int(_FRAME_OF_SLOT[a, x])) * S5_GROUP
                rows.append(kp[:, off:off + GW])
            toep = jnp.concatenate(rows, axis=0).astype(_BF)
            op_ref[a, 0:GW, :] = _dot(toep, cperm_ref[a]).astype(_BF)
            op_ref[a, GW:2 * GW, :] = mc_ref[a]

    slot = lax.broadcasted_iota(jnp.int32, (MIX_RT, LANES), 1) // S5_GROUP
    dre = [dre_ref[q] for q in range(PAIRS_PER_J)]
    dim_ = [dim_ref[q] for q in range(PAIRS_PER_J)]

    def relayout_in(c0):
        r0 = c0 * BATCH
        for half in range(2):
            pieces = [u_ref[c0:c0 + MIX_CB, SLOTS * half + k, :, :].reshape(MIX_RT, LANES) for k in range(SLOTS)]
            groups = _gather_slots(pieces, slot)
            for a in range(SLOTS):
                lo = (a % 2) * GW + half * LANES
                xs_ref[a // 2, r0:r0 + MIX_RT, lo:lo + LANES] = groups[a]

    def relayout_out(c0):
        r0 = c0 * BATCH
        for half in range(2):
            groups = [ys_ref[a, r0:r0 + MIX_RT, half * LANES:(half + 1) * LANES].astype(_BF) for a in range(SLOTS)]
            frames = _scatter_slots(groups, slot)
            for k in range(SLOTS):
                y_ref[c0:c0 + MIX_CB, SLOTS * half + k, :, :] = frames[k].reshape(MIX_CB, BATCH, LANES)

    def recurrence(c0, state):
        for c in range(c0, c0 + MIX_HC):
            r0 = c * BATCH
            new = []
            for q in range(PAIRS_PER_J):
                hre, him = state[2 * q], state[2 * q + 1]
                hs_ref[q, r0:r0 + BATCH, 0:LANES] = hre.astype(_BF)
                hs_ref[q, r0:r0 + BATCH, LANES:2 * LANES] = him.astype(_BF)
                sre = ss_ref[q, r0:r0 + BATCH, 0:LANES]
                sim = ss_ref[q, r0:r0 + BATCH, LANES:2 * LANES]
                new.append(dre[q] * hre - dim_[q] * him + sre)
                new.append(dre[q] * him + dim_[q] * hre + sim)
            state = new
        return state

    blocks = [(h * MIX_HC, h * MIX_HC * BATCH) for h in range(MIX_NH)]
    for c0, r0 in blocks:
        for t in range(MIX_HC // MIX_CB):
            relayout_in(c0 + t * MIX_CB)
        for q in range(PAIRS_PER_J):
            ss_ref[q, r0:r0 + MIX_HR, :] = _dot(xs_ref[q, r0:r0 + MIX_HR, :], mb_ref[q])
    state = [carry_ref[q, k] for q in range(PAIRS_PER_J) for k in range(2)]
    for c0, r0 in blocks:
        state = recurrence(c0, state)
    for q in range(PAIRS_PER_J):
        carry_ref[q, 0] = state[2 * q]
        carry_ref[q, 1] = state[2 * q + 1]
    for c0, r0 in blocks:
        for a in range(SLOTS):
            q, m = a // 2, a % 2
            lhs = jnp.concatenate([xs_ref[q, r0:r0 + MIX_HR, m * GW:(m + 1) * GW], hs_ref[q, r0:r0 + MIX_HR, :]], axis=1)
            ys_ref[a, r0:r0 + MIX_HR, :] = _dot(lhs, op_ref[a])
    for c0, r0 in blocks:
        for t in range(MIX_HC // MIX_CB):
            relayout_out(c0 + t * MIX_CB)


def _s5_mix(u, kpad, mb, mc, dec_re, dec_im):
    grid = (NJ, NCHUNK // MIX_CT)
    blk = (MIX_CT, TCH, BATCH, LANES)
    dspec = pl.BlockSpec((PAIRS_PER_J, 1, LANES), lambda j, c: (j, 0, 0))
    return pl.pallas_call(
        _s5_mix_kernel,
        grid=grid,
        in_specs=[
            pl.BlockSpec(blk, lambda j, c: (c, 0, 0, j)),
            pl.BlockSpec((SLOTS, S5_GROUP, KPAD), lambda j, c: (j, 0, 0)),
            pl.BlockSpec((PAIRS_PER_J, 2 * GW, GW), lambda j, c: (j, 0, 0)),
            pl.BlockSpec((SLOTS, GW, GW), lambda j, c: (j, 0, 0)),
            pl.BlockSpec((SLOTS, GW, GW), lambda j, c: (0, 0, 0)),
            dspec, dspec,
        ],
        out_specs=pl.BlockSpec(blk, lambda j, c: (c, 0, 0, j)),
        out_shape=jax.ShapeDtypeStruct((NCHUNK, TCH, BATCH, BRANCH), _BF),
        scratch_shapes=[
            pltpu.VMEM((SLOTS, 2 * GW, GW), _BF),
            pltpu.VMEM((PAIRS_PER_J, MIX_R, 2 * GW), _BF),
            pltpu.VMEM((SLOTS, MIX_R, GW), _F32),
            pltpu.VMEM((PAIRS_PER_J, MIX_R, GW), _F32),
            pltpu.VMEM((PAIRS_PER_J, MIX_R, GW), _BF),
            pltpu.VMEM((PAIRS_PER_J, 2, BATCH, LANES), _F32),
        ],
        compiler_params=pltpu.CompilerParams(
            dimension_semantics=("parallel", "arbitrary"), vmem_limit_bytes=VMEM_LIMIT),
        name="s5_mix",
    )(u, kpad, mb, mc, jnp.asarray(_CPERM, _BF), dec_re, dec_im)


def _s5_out_kernel(y_ref, gate_ref, wgluf_ref, bglu_ref, woutf_ref, o_ref, wglu_ref, wout_ref):
    @pl.when(pl.program_id(0) == 0)
    def _():
        wglu_ref[...] = wgluf_ref[...].astype(_BF)
        wout_ref[...] = woutf_ref[...].astype(_BF)

    rows = IN_MB * BATCH * TCH
    nb = IN_CT // IN_MB
    st = [dict() for _ in range(nb)]

    def act(i):
        st[i]["y"] = _gelu(y_ref[i * IN_MB:(i + 1) * IN_MB].reshape(rows, BRANCH).astype(_F32))

    def glu_mm(i):
        st[i]["glu"] = _dot(st[i]["y"].astype(_BF), wglu_ref[...])

    def gating(i):
        y = st[i].pop("y") * _sigmoid(st[i].pop("glu") + bglu_ref[...])
        sg = gate_ref[i * IN_MB:(i + 1) * IN_MB].reshape(rows, BRANCH).astype(_F32)
        st[i]["z"] = (y * sg).astype(_BF)

    def out_mm(i):
        st[i]["out"] = _dot(st[i].pop("z"), wout_ref[...])

    def residual(i):
        out = st[i].pop("out").reshape(IN_MB, TCH, BATCH, D_MODEL)
        for k in range(IN_MB):
            c = i * IN_MB + k
            o_ref[:, c * TCH:(c + 1) * TCH, :] = jnp.swapaxes(out[k], 0, 1).astype(_BF)

    _wavefront([act, glu_mm, gating, out_mm, residual], nb)


def _s5_out(y, gate, w_glu, b_glu, w_out):
    grid = (NCHUNK // IN_CT,)
    blk4 = (IN_CT, TCH, BATCH, BRANCH)
    xblk = pl.BlockSpec((BATCH, IN_CT * TCH, D_MODEL), lambda i: (0, i, 0))
    return pl.pallas_call(
        _s5_out_kernel,
        grid=grid,
        in_specs=[
            pl.BlockSpec(blk4, lambda i: (i, 0, 0, 0)),
            pl.BlockSpec(blk4, lambda i: (i, 0, 0, 0)),
            pl.BlockSpec((BRANCH, BRANCH), lambda i: (0, 0)),
            pl.BlockSpec((1, BRANCH), lambda i: (0, 0)),
            pl.BlockSpec((BRANCH, D_MODEL), lambda i: (0, 0)),
        ],
        out_specs=xblk,
        out_shape=jax.ShapeDtypeStruct((BATCH, SEQ, D_MODEL), _BF),
        scratch_shapes=[pltpu.VMEM((BRANCH, BRANCH), _BF), pltpu.VMEM((BRANCH, D_MODEL), _BF)],
        compiler_params=pltpu.CompilerParams(
            dimension_semantics=("arbitrary",), vmem_limit_bytes=VMEM_LIMIT),
        name="s5_out",
    )(y, gate, w_glu, b_glu, w_out)


SGU_TB = 1024
SGU_NB = SGU_TB // SGU_BLOCK


SGU_RB = 512
SGU_BPB = SGU_RB // SGU_BLOCK


def _sgu_kernel(x_ref, d_ref, g_ref, win_ref, lng_ref, lnb_ref, ws_ref, bs_ref, wout_ref, fg_ref, o_ref, mix_ref):
    blocks = [slice(r0, r0 + SGU_RB) for r0 in range(0, SGU_TB, SGU_RB)]
    st = [dict() for _ in blocks]

    def norm_in(i):
        st[i]["h"] = _rms(x_ref[blocks[i], :] + d_ref[blocks[i], :].astype(_F32), g_ref[...]).astype(_BF)

    def proj_v(i):
        st[i]["v"] = _dot(st[i]["h"], win_ref[:, BRANCH:2 * BRANCH])

    def proj_ug(i):
        st[i]["u"] = _dot(st[i]["h"], win_ref[:, :BRANCH])
        st[i]["gate"] = _dot(st[i]["h"], win_ref[:, 2 * BRANCH:])

    def layer_norm(i):
        v = _gelu(st[i].pop("v"))
        mu = jnp.mean(v, axis=-1, keepdims=True)
        vc = v - mu
        var = jnp.mean(vc * vc, axis=-1, keepdims=True)
        st[i]["vn"] = (vc * lax.rsqrt(var + LN_EPS) * lng_ref[...] + lnb_ref[...]).astype(_BF)

    def spatial(i):
        vn = st[i].pop("vn")
        r0 = blocks[i].start
        for hd in range(SGU_HEADS):
            lo = hd * SGU_HEAD_DIM
            rhs = jnp.concatenate(
                [vn[b * SGU_BLOCK:(b + 1) * SGU_BLOCK, lo:lo + SGU_HEAD_DIM] for b in range(SGU_BPB)], axis=1)
            res = _dot(ws_ref[hd], rhs)
            for b in range(SGU_BPB):
                mix_ref[r0 + b * SGU_BLOCK:r0 + (b + 1) * SGU_BLOCK, lo:lo + SGU_HEAD_DIM] = (
                    res[:, b * SGU_HEAD_DIM:(b + 1) * SGU_HEAD_DIM] + bs_ref[:, lo:lo + SGU_HEAD_DIM])

    def gating(i):
        gate = st[i].pop("gate")
        y = _gelu(st[i].pop("u")) * mix_ref[blocks[i], :]
        st[i]["z"] = (y * (gate * _sigmoid(gate))).astype(_BF)

    def proj_out(i):
        st[i]["out"] = _dot(st[i].pop("z"), wout_ref[...])

    def norm_out(i):
        x2 = (x_ref[blocks[i], :] + d_ref[blocks[i], :].astype(_F32)) + st[i].pop("out")
        o_ref[blocks[i], :] = _rms(x2, fg_ref[...])

    _wavefront([norm_in, proj_v, layer_norm, proj_ug, spatial, gating, proj_out, norm_out], len(blocks))


def _sgu_layer(x, delta, g, w_in, ln_g, ln_b, ws, bs, w_out, final_g):
    n = BATCH * SEQ
    grid = (n // SGU_TB,)
    row = lambda i: (0, 0)
    xblk = pl.BlockSpec((SGU_TB, D_MODEL), lambda i: (i, 0))
    return pl.pallas_call(
        _sgu_kernel,
        grid=grid,
        in_specs=[
            xblk,
            xblk,
            pl.BlockSpec((1, D_MODEL), row),
            pl.BlockSpec((D_MODEL, 3 * BRANCH), row),
            pl.BlockSpec((1, BRANCH), row),
            pl.BlockSpec((1, BRANCH), row),
            pl.BlockSpec((SGU_HEADS, SGU_BLOCK, SGU_BLOCK), lambda i: (0, 0, 0)),
            pl.BlockSpec((SGU_BLOCK, BRANCH), row),
            pl.BlockSpec((BRANCH, D_MODEL), row),
            pl.BlockSpec((1, D_MODEL), row),
        ],
        out_specs=xblk,
        out_shape=jax.ShapeDtypeStruct((n, D_MODEL), _F32),
        scratch_shapes=[pltpu.VMEM((SGU_TB, BRANCH), _F32)],
        compiler_params=pltpu.CompilerParams(
            dimension_semantics=("parallel",), vmem_limit_bytes=VMEM_LIMIT),
        name="sgu_layer",
    )(x, delta, g, w_in, ln_g, ln_b, ws, bs, w_out, final_g)


def kernel(x, norm_g, final_g, s5_w_in, s5_A_re, s5_A_im, s5_log_dt, s5_B_re, s5_B_im, s5_C_re, s5_C_im, s5_D, s5_w_glu, s5_b_glu, s5_w_out, sgu_w_in, sgu_ln_g, sgu_ln_b, sgu_w_s, sgu_b_s, sgu_w_out):
    kpad, mb, mc, dec_re, dec_im = _s5_operators(
        s5_A_re[0], s5_A_im[0], s5_log_dt[0], s5_B_re[0], s5_B_im[0], s5_C_re[0], s5_C_im[0], s5_D[0])
    u, gate = _s5_in(x, norm_g[0][None, :], s5_w_in[0])
    y = _s5_mix(u, kpad, mb, mc, dec_re, dec_im)
    delta = _s5_out(y, gate, s5_w_glu[0], s5_b_glu[0][None, :], s5_w_out[0])

    mask = jnp.tril(jnp.ones((SGU_BLOCK, SGU_BLOCK), dtype=bool))
    ws = jnp.where(mask[None], sgu_w_s[0], 0.0).astype(_BF)
    bs = jnp.repeat(jnp.transpose(sgu_b_s[0]), SGU_HEAD_DIM, axis=1)
    out = _sgu_layer(x.reshape(BATCH * SEQ, D_MODEL), delta.reshape(BATCH * SEQ, D_MODEL),
                     norm_g[1][None, :], sgu_w_in[0].astype(_BF),
                     sgu_ln_g[0][None, :], sgu_ln_b[0][None, :], ws, bs,
                     sgu_w_out[0].astype(_BF), final_g[None, :])
    return out.reshape(BATCH, SEQ, D_MODEL)
```

```python
import math

import numpy as np
import jax
import jax.numpy as jnp
from jax import lax
from jax.experimental import pallas as pl
from jax.experimental.pallas import tpu as pltpu

D_MODEL = 1024
BATCH = 16
SEQ = 2048
BRANCH = D_MODEL
S5_GROUP = 16
S5_GROUPS = BRANCH // S5_GROUP
S5_STATE = 64
SGU_BLOCK = 128
SGU_HEADS = 8
SGU_HEAD_DIM = BRANCH // SGU_HEADS
RMS_EPS = 1e-6
LN_EPS = 1e-5
RE_CLIP = -1e-4

LANES = 128
TCH = 16
NCHUNK = SEQ // TCH
SLOTS = LANES // S5_GROUP
NJ = BRANCH // LANES
GW = TCH * S5_GROUP
PAIRS_PER_J = SLOTS // 2
VMEM_LIMIT = 56 * 1024 * 1024

_HI = lax.Precision.HIGHEST
_BF = jnp.bfloat16
_F32 = jnp.float32


_GELU_C1 = math.sqrt(2.0 / math.pi)
_GELU_C2 = _GELU_C1 * 0.044715


def _gelu(x):
    inner = x * (_GELU_C1 + _GELU_C2 * (x * x))
    return (0.5 * x) * (1.0 + jnp.tanh(inner))


def _sigmoid(x):
    return 0.5 + 0.5 * jnp.tanh(0.5 * x)


def _silu(x):
    h = 0.5 * x
    return h + h * jnp.tanh(h)


def _dot(a, b):
    return jnp.dot(a, b, preferred_element_type=_F32)


def _wavefront(stages, n):
    for t in range(len(stages) + n - 1):
        for i in range(n):
            k = t - i
            if 0 <= k < len(stages):
                stages[k](i)


def _rms(x, g):
    return x * lax.rsqrt(jnp.mean(x * x, axis=-1, keepdims=True) + RMS_EPS) * g


KPAD = 2 * GW


def _slot_frame_table():
    a = np.arange(SLOTS)[:, None]
    x = np.arange(TCH)[None, :]
    half, k = x // SLOTS, x % SLOTS
    return SLOTS * half + (k - a) % SLOTS


_FRAME_OF_SLOT = _slot_frame_table()


def _s5_ops_kernel(lre_ref, lim_ref, ldr_ref, ldi_ref, btr_ref, bti_ref, cr_ref, ci_ref, d_ref,
                   kpad_ref, mb_ref, mc_ref, dre_ref, dim_ref):
    P = S5_STATE
    lane = lax.broadcasted_iota(jnp.int32, (S5_GROUP, LANES), 1)
    kcol = lax.broadcasted_iota(jnp.int32, (TCH, 1), 0)
    lane256 = lax.broadcasted_iota(jnp.int32, (S5_GROUP, GW), 1)
    row256 = lax.broadcasted_iota(jnp.int32, (S5_GROUP, GW), 0)
    lane_gw = lax.broadcasted_iota(jnp.int32, (GW, LANES), 1)
    nt = (((1,), (1,)), ((), ()))
    zeros64 = jnp.zeros((P, GW), _F32)
    st = [dict() for _ in range(SLOTS)]

    def outer(t_re, t_im, m_re, m_im):
        re = [t_re[k:k + 1] * m_re - t_im[k:k + 1] * m_im for k in range(TCH)]
        im = [t_re[k:k + 1] * m_im + t_im[k:k + 1] * m_re for k in range(TCH)]
        return jnp.concatenate(re, axis=0), jnp.concatenate(im, axis=0)

    def tables(a):
        lam_re, lam_im = lre_ref[a:a + 1, :], lim_ref[a:a + 1, :]
        ldr, ldi = ldr_ref[a:a + 1, :], ldi_ref[a:a + 1, :]

        def power(k):
            mag = jnp.exp(k * ldr)
            return mag * jnp.cos(k * ldi), mag * jnp.sin(k * ldi)

        frame = (kcol & SLOTS) + (((kcol & (SLOTS - 1)) + (SLOTS - a)) & (SLOTS - 1))
        pw_re, pw_im = power(kcol.astype(_F32))
        pi_re, pi_im = power((TCH - 1 - frame).astype(_F32))
        po_re, po_im = power((frame + 1).astype(_F32))
        st[a]["p16"] = power(jnp.full((1, 1), float(TCH), _F32))

        nr, ni = pw_re[1:2] - 1.0, pw_im[1:2]
        den = lam_re * lam_re + lam_im * lam_im
        cf_re = (nr * lam_re + ni * lam_im) / den
        cf_im = (ni * lam_re - nr * lam_im) / den
        b_re, b_im = btr_ref[a], bti_ref[a]
        bb_re = cf_re * b_re - cf_im * b_im
        bb_im = cf_re * b_im + cf_im * b_re
        c_re, c_im = cr_ref[a], ci_ref[a]
        st[a]["bb"] = (bb_re, bb_im)
        st[a]["lc"] = outer(pw_re, pw_im, c_re, c_im)
        st[a]["in"] = outer(pi_re, pi_im, bb_re, bb_im)
        st[a]["ot"] = outer(po_re, po_im, c_re, c_im)

    def taps(a):
        bb_re, bb_im = st[a].pop("bb")
        lc_re, lc_im = st[a].pop("lc")
        once = lane < P
        k = (lax.dot_general(jnp.where(once, bb_re, 0.0), lc_re, nt, precision=_HI, preferred_element_type=_F32)
             - lax.dot_general(jnp.where(once, bb_im, 0.0), lc_im, nt, precision=_HI, preferred_element_type=_F32))
        kpad_ref[a, :, 0:GW] = jnp.zeros((S5_GROUP, GW), _F32)
        kpad_ref[a, :, GW:2 * GW] = k + jnp.where(lane256 == row256, d_ref[a], 0.0)

    def state_in(a):
        q, m = a // 2, a % 2
        mine = (lane_gw >= P) if m else (lane_gw < P)
        in_re, in_im = st[a].pop("in")
        mb_ref[q, m * GW:(m + 1) * GW, 0:LANES] = jnp.where(mine, in_re, 0.0).astype(_BF)
        mb_ref[q, m * GW:(m + 1) * GW, LANES:2 * LANES] = jnp.where(mine, in_im, 0.0).astype(_BF)

    def state_out(a):
        q, m = a // 2, a % 2
        ot_re, ot_im = st[a].pop("ot")
        o_re, o_im = ot_re.T[0:P], ot_im.T[0:P]
        blocks = [zeros64, zeros64, zeros64, zeros64]
        blocks[m], blocks[2 + m] = o_re, -o_im
        mc_ref[a] = jnp.concatenate(blocks, axis=0).astype(_BF)
        if m:
            first = lax.broadcasted_iota(jnp.int32, (1, LANES), 1) < P
            dre_ref[q] = jnp.where(first, st[a - 1]["p16"][0], st[a]["p16"][0])
            dim_ref[q] = jnp.where(first, st[a - 1]["p16"][1], st[a]["p16"][1])

    _wavefront([tables, taps, state_in, state_out], SLOTS)


def _s5_operators(A_re, A_im, log_dt, B_re, B_im, C_re, C_im, D):
    f32 = jnp.float32
    G, P, H = S5_GROUPS, S5_STATE, S5_GROUP
    lam_re = jnp.minimum(A_re.astype(f32), RE_CLIP)
    lam_im = A_im.astype(f32)
    dt = jnp.exp(log_dt.astype(f32))[:, None]

    twice = lambda t: jnp.concatenate([t, t], axis=-1)
    tables = [twice(lam_re), twice(lam_im), twice(lam_re * dt), twice(lam_im * dt)]
    mats = [twice(jnp.swapaxes(B_re.astype(f32), 1, 2)), twice(jnp.swapaxes(B_im.astype(f32), 1, 2)),
            twice(C_re.astype(f32)), twice(C_im.astype(f32))]
    tspec = pl.BlockSpec((SLOTS, LANES), lambda j: (j, 0))
    mspec = pl.BlockSpec((SLOTS, H, LANES), lambda j: (j, 0, 0))
    dspec = pl.BlockSpec((PAIRS_PER_J, 1, LANES), lambda j: (j, 0, 0))
    return pl.pallas_call(
        _s5_ops_kernel,
        grid=(NJ,),
        in_specs=[tspec] * 4 + [mspec] * 4 + [pl.BlockSpec((SLOTS, H, 1), lambda j: (j, 0, 0))],
        out_specs=[
            pl.BlockSpec((SLOTS, H, KPAD), lambda j: (j, 0, 0)),
            pl.BlockSpec((PAIRS_PER_J, 2 * GW, GW), lambda j: (j, 0, 0)),
            pl.BlockSpec((SLOTS, GW, GW), lambda j: (j, 0, 0)),
            dspec, dspec,
        ],
        out_shape=[
            jax.ShapeDtypeStruct((G, H, KPAD), f32),
            jax.ShapeDtypeStruct((G // 2, 2 * GW, GW), _BF),
            jax.ShapeDtypeStruct((G, GW, GW), _BF),
            jax.ShapeDtypeStruct((G // 2, 1, LANES), f32),
            jax.ShapeDtypeStruct((G // 2, 1, LANES), f32),
        ],
        compiler_params=pltpu.CompilerParams(dimension_semantics=("parallel",), vmem_limit_bytes=VMEM_LIMIT),
        name="s5_ops",
    )(*tables, *mats, D.astype(f32).reshape(G, H, 1))


IN_CT = 8


IN_MB = 1


def _s5_in_kernel(x_ref, g_ref, w_ref, u_ref, gate_ref):
    rows = BATCH * TCH
    nb = IN_CT // IN_MB
    st = [dict() for _ in range(nb)]

    def norm(i):
        parts = []
        for c in range(i * IN_MB, (i + 1) * IN_MB):
            xc = jnp.swapaxes(x_ref[:, c * TCH:(c + 1) * TCH, :], 0, 1).reshape(rows, D_MODEL)
            parts.append(_rms(xc, g_ref[...]).astype(_BF))
        st[i]["hp"] = jnp.concatenate(parts, axis=0)

    def gate_mm(i):
        gate = _dot(st[i]["hp"], w_ref[:, BRANCH:])
        gate_ref[i * IN_MB:(i + 1) * IN_MB] = (
            _silu(gate).astype(_BF).reshape(IN_MB, TCH, BATCH, BRANCH))

    def u_mm(i):
        u = _dot(st[i].pop("hp"), w_ref[:, :BRANCH])
        u_ref[i * IN_MB:(i + 1) * IN_MB] = u.astype(_BF).reshape(IN_MB, TCH, BATCH, BRANCH)

    _wavefront([norm, gate_mm, u_mm], nb)


def _s5_in(x, g, w_in):
    grid = (NCHUNK // IN_CT,)
    blk4 = (IN_CT, TCH, BATCH, BRANCH)
    return pl.pallas_call(
        _s5_in_kernel,
        grid=grid,
        in_specs=[
            pl.BlockSpec((BATCH, IN_CT * TCH, D_MODEL), lambda i: (0, i, 0)),
            pl.BlockSpec((1, D_MODEL), lambda i: (0, 0)),
            pl.BlockSpec((D_MODEL, 2 * BRANCH), lambda i: (0, 0)),
        ],
        out_specs=[
            pl.BlockSpec(blk4, lambda i: (i, 0, 0, 0)),
            pl.BlockSpec(blk4, lambda i: (i, 0, 0, 0)),
        ],
        out_shape=[
            jax.ShapeDtypeStruct((NCHUNK, TCH, BATCH, BRANCH), _BF),
            jax.ShapeDtypeStruct((NCHUNK, TCH, BATCH, BRANCH), _BF),
        ],
        compiler_params=pltpu.CompilerParams(
            dimension_semantics=("parallel",), vmem_limit_bytes=VMEM_LIMIT),
        name="s5_in",
    )(x, g, w_in)


MIX_CT = 64
MIX_R = MIX_CT * BATCH
MIX_CB = 4
MIX_RT = MIX_CB * BATCH
MIX_NH = 4
MIX_HC = MIX_CT // MIX_NH
MIX_HR = MIX_HC * BATCH

def _gather_slots(v, slot):
    rolled = [v[0]] + [pltpu.roll(v[k], k * S5_GROUP, axis=1) for k in range(1, SLOTS)]
    out = []
    for a in range(SLOTS):
        acc = rolled[0]
        for k in range(1, SLOTS):
            acc = jnp.where(slot == (a + k) % SLOTS, rolled[k], acc)
        out.append(acc)
    return out


def _scatter_slots(v, slot):
    out = []
    for k in range(SLOTS):
        acc = v[0]
        for a in range(1, SLOTS):
            acc = jnp.where(slot == (a + k) % SLOTS, v[a], acc)
        out.append(acc if k == 0 else pltpu.roll(acc, (SLOTS - k) * S5_GROUP, axis=1))
    return out


def _column_perms():
    p = np.zeros((SLOTS, GW, GW), np.float32)
    for a in range(SLOTS):
        for x in range(TCH):
            f = int(_FRAME_OF_SLOT[a, x])
            for c in range(S5_GROUP):
                p[a, f * S5_GROUP + c, x * S5_GROUP + c] = 1.0
    return p


_CPERM = _column_perms()


def _s5_mix_kernel(u_ref, kpad_ref, mb_ref, mc_ref, cperm_ref, dre_ref, dim_ref, y_ref,
                   op_ref, xs_ref, ys_ref, ss_ref, hs_ref, carry_ref):
    ct = pl.program_id(1)

    @pl.when(ct == 0)
    def _():
        carry_ref[...] = jnp.zeros_like(carry_ref)
        for a in range(SLOTS):
            kp = kpad_ref[a]
            rows = []
            for x in range(TCH):
                off = (TCH - int(_FRAME_OF_SLOT[a, x])) * S5_GROUP
                rows.append(kp[:, off:off + GW])
            toep = jnp.concatenate(rows, axis=0).astype(_BF)
            op_ref[a, 0:GW, :] = _dot(toep, cperm_ref[a]).astype(_BF)
            op_ref[a, GW:2 * GW, :] = mc_ref[a]

    slot = lax.broadcasted_iota(jnp.int32, (MIX_RT, LANES), 1) // S5_GROUP
    dre = [dre_ref[q] for q in range(PAIRS_PER_J)]
    dim_ = [dim_ref[q] for q in range(PAIRS_PER_J)]

    def relayout_in(c0):
        r0 = c0 * BATCH
        for half in range(2):
            pieces = [u_ref[c0:c0 + MIX_CB, SLOTS * half + k, :, :].reshape(MIX_RT, LANES) for k in range(SLOTS)]
            groups = _gather_slots(pieces, slot)
            for a in range(SLOTS):
                lo = (a % 2) * GW + half * LANES
                xs_ref[a // 2, r0:r0 + MIX_RT, lo:lo + LANES] = groups[a]

    def relayout_out(c0):
        r0 = c0 * BATCH
        for half in range(2):
            groups = [ys_ref[a, r0:r0 + MIX_RT, half * LANES:(half + 1) * LANES].astype(_BF) for a in range(SLOTS)]
            frames = _scatter_slots(groups, slot)
            for k in range(SLOTS):
                y_ref[c0:c0 + MIX_CB, SLOTS * half + k, :, :] = frames[k].reshape(MIX_CB, BATCH, LANES)

    def recurrence(c0, state):
        for c in range(c0, c0 + MIX_HC):
            r0 = c * BATCH
            new = []
            for q in range(PAIRS_PER_J):
                hre, him = state[2 * q], state[2 * q + 1]
                hs_ref[q, r0:r0 + BATCH, 0:LANES] = hre.astype(_BF)
                hs_ref[q, r0:r0 + BATCH, LANES:2 * LANES] = him.astype(_BF)
                sre = ss_ref[q, r0:r0 + BATCH, 0:LANES]
                sim = ss_ref[q, r0:r0 + BATCH, LANES:2 * LANES]
                new.append(dre[q] * hre - dim_[q] * him + sre)
                new.append(dre[q] * him + dim_[q] * hre + sim)
            state = new
        return state

    blocks = [(h * MIX_HC, h * MIX_HC * BATCH) for h in range(MIX_NH)]
    for c0, r0 in blocks:
        for t in range(MIX_HC // MIX_CB):
            relayout_in(c0 + t * MIX_CB)
        for q in range(PAIRS_PER_J):
            ss_ref[q, r0:r0 + MIX_HR, :] = _dot(xs_ref[q, r0:r0 + MIX_HR, :], mb_ref[q])
    state = [carry_ref[q, k] for q in range(PAIRS_PER_J) for k in range(2)]
    for c0, r0 in blocks:
        state = recurrence(c0, state)
    for q in range(PAIRS_PER_J):
        carry_ref[q, 0] = state[2 * q]
        carry_ref[q, 1] = state[2 * q + 1]
    for c0, r0 in blocks:
        for a in range(SLOTS):
            q, m = a // 2, a % 2
            lhs = jnp.concatenate([xs_ref[q, r0:r0 + MIX_HR, m * GW:(m + 1) * GW], hs_ref[q, r0:r0 + MIX_HR, :]], axis=1)
            ys_ref[a, r0:r0 + MIX_HR, :] = _dot(lhs, op_ref[a])
    for c0, r0 in blocks:
        for t in range(MIX_HC // MIX_CB):
            relayout_out(c0 + t * MIX_CB)


def _s5_mix(u, kpad, mb, mc, dec_re, dec_im):
    grid = (NJ, NCHUNK // MIX_CT)
    blk = (MIX_CT, TCH, BATCH, LANES)
    dspec = pl.BlockSpec((PAIRS_PER_J, 1, LANES), lambda j, c: (j, 0, 0))
    return pl.pallas_call(
        _s5_mix_kernel,
        grid=grid,
        in_specs=[
            pl.BlockSpec(blk, lambda j, c: (c, 0, 0, j)),
            pl.BlockSpec((SLOTS, S5_GROUP, KPAD), lambda j, c: (j, 0, 0)),
            pl.BlockSpec((PAIRS_PER_J, 2 * GW, GW), lambda j, c: (j, 0, 0)),
            pl.BlockSpec((SLOTS, GW, GW), lambda j, c: (j, 0, 0)),
            pl.BlockSpec((SLOTS, GW, GW), lambda j, c: (0, 0, 0)),
            dspec, dspec,
        ],
        out_specs=pl.BlockSpec(blk, lambda j, c: (c, 0, 0, j)),
        out_shape=jax.ShapeDtypeStruct((NCHUNK, TCH, BATCH, BRANCH), _BF),
        scratch_shapes=[
            pltpu.VMEM((SLOTS, 2 * GW, GW), _BF),
            pltpu.VMEM((PAIRS_PER_J, MIX_R, 2 * GW), _BF),
            pltpu.VMEM((SLOTS, MIX_R, GW), _F32),
            pltpu.VMEM((PAIRS_PER_J, MIX_R, GW), _F32),
            pltpu.VMEM((PAIRS_PER_J, MIX_R, GW), _BF),
            pltpu.VMEM((PAIRS_PER_J, 2, BATCH, LANES), _F32),
        ],
        compiler_params=pltpu.CompilerParams(
            dimension_semantics=("parallel", "arbitrary"), vmem_limit_bytes=VMEM_LIMIT),
        name="s5_mix",
    )(u, kpad, mb, mc, jnp.asarray(_CPERM, _BF), dec_re, dec_im)


def _s5_out_kernel(y_ref, gate_ref, wglu_ref, bglu_ref, wout_ref, o_ref):
    rows = IN_MB * BATCH * TCH
    nb = IN_CT // IN_MB
    st = [dict() for _ in range(nb)]

    def act(i):
        st[i]["y"] = _gelu(y_ref[i * IN_MB:(i + 1) * IN_MB].reshape(rows, BRANCH).astype(_F32))

    def glu_mm(i):
        st[i]["glu"] = _dot(st[i]["y"].astype(_BF), wglu_ref[...])

    def gating(i):
        y = st[i].pop("y") * _sigmoid(st[i].pop("glu") + bglu_ref[...])
        sg = gate_ref[i * IN_MB:(i + 1) * IN_MB].reshape(rows, BRANCH).astype(_F32)
        st[i]["z"] = (y * sg).astype(_BF)

    def out_mm(i):
        st[i]["out"] = _dot(st[i].pop("z"), wout_ref[...])

    def residual(i):
        out = st[i].pop("out").reshape(IN_MB, TCH, BATCH, D_MODEL)
        for k in range(IN_MB):
            c = i * IN_MB + k
            o_ref[:, c * TCH:(c + 1) * TCH, :] = jnp.swapaxes(out[k], 0, 1)

    _wavefront([act, glu_mm, gating, out_mm, residual], nb)


def _s5_out(y, gate, w_glu, b_glu, w_out):
    grid = (NCHUNK // IN_CT,)
    blk4 = (IN_CT, TCH, BATCH, BRANCH)
    xblk = pl.BlockSpec((BATCH, IN_CT * TCH, D_MODEL), lambda i: (0, i, 0))
    return pl.pallas_call(
        _s5_out_kernel,
        grid=grid,
        in_specs=[
            pl.BlockSpec(blk4, lambda i: (i, 0, 0, 0)),
            pl.BlockSpec(blk4, lambda i: (i, 0, 0, 0)),
            pl.BlockSpec((BRANCH, BRANCH), lambda i: (0, 0)),
            pl.BlockSpec((1, BRANCH), lambda i: (0, 0)),
            pl.BlockSpec((BRANCH, D_MODEL), lambda i: (0, 0)),
        ],
        out_specs=xblk,
        out_shape=jax.ShapeDtypeStruct((BATCH, SEQ, D_MODEL), _F32),
        compiler_params=pltpu.CompilerParams(
            dimension_semantics=("parallel",), vmem_limit_bytes=VMEM_LIMIT),
        name="s5_out",
    )(y, gate, w_glu, b_glu, w_out)


SGU_TB = 1024
SGU_NB = SGU_TB // SGU_BLOCK


SGU_RB = 512
SGU_BPB = SGU_RB // SGU_BLOCK


def _sgu_kernel(x_ref, d_ref, g_ref, win_ref, lng_ref, lnb_ref, ws_ref, bs_ref, wout_ref, fg_ref, o_ref, mix_ref):
    blocks = [slice(r0, r0 + SGU_RB) for r0 in range(0, SGU_TB, SGU_RB)]
    st = [dict() for _ in blocks]

    def norm_in(i):
        st[i]["h"] = _rms(x_ref[blocks[i], :] + d_ref[blocks[i], :], g_ref[...]).astype(_BF)

    def proj_v(i):
        st[i]["v"] = _dot(st[i]["h"], win_ref[:, BRANCH:2 * BRANCH])

    def proj_ug(i):
        st[i]["u"] = _dot(st[i]["h"], win_ref[:, :BRANCH])
        st[i]["gate"] = _dot(st[i]["h"], win_ref[:, 2 * BRANCH:])

    def layer_norm(i):
        v = _gelu(st[i].pop("v"))
        mu = jnp.mean(v, axis=-1, keepdims=True)
        vc = v - mu
        var = jnp.mean(vc * vc, axis=-1, keepdims=True)
        st[i]["vn"] = (vc * lax.rsqrt(var + LN_EPS) * lng_ref[...] + lnb_ref[...]).astype(_BF)

    def spatial(i):
        vn = st[i].pop("vn")
        r0 = blocks[i].start
        for hd in range(SGU_HEADS):
            lo = hd * SGU_HEAD_DIM
            rhs = jnp.concatenate(
                [vn[b * SGU_BLOCK:(b + 1) * SGU_BLOCK, lo:lo + SGU_HEAD_DIM] for b in range(SGU_BPB)], axis=1)
            res = _dot(ws_ref[hd], rhs)
            for b in range(SGU_BPB):
                mix_ref[r0 + b * SGU_BLOCK:r0 + (b + 1) * SGU_BLOCK, lo:lo + SGU_HEAD_DIM] = (
                    res[:, b * SGU_HEAD_DIM:(b + 1) * SGU_HEAD_DIM] + bs_ref[:, lo:lo + SGU_HEAD_DIM])

    def gating(i):
        gate = st[i].pop("gate")
        y = _gelu(st[i].pop("u")) * mix_ref[blocks[i], :]
        st[i]["z"] = (y * _silu(gate)).astype(_BF)

    def proj_out(i):
        st[i]["out"] = _dot(st[i].pop("z"), wout_ref[...])

    def norm_out(i):
        x2 = (x_ref[blocks[i], :] + d_ref[blocks[i], :]) + st[i].pop("out")
        o_ref[blocks[i], :] = _rms(x2, fg_ref[...])

    _wavefront([norm_in, proj_v, layer_norm, proj_ug, spatial, gating, proj_out, norm_out], len(blocks))


def _sgu_layer(x, delta, g, w_in, ln_g, ln_b, ws, bs, w_out, final_g):
    n = BATCH * SEQ
    grid = (n // SGU_TB,)
    row = lambda i: (0, 0)
    xblk = pl.BlockSpec((SGU_TB, D_MODEL), lambda i: (i, 0))
    return pl.pallas_call(
        _sgu_kernel,
        grid=grid,
        in_specs=[
            xblk,
            xblk,
            pl.BlockSpec((1, D_MODEL), row),
            pl.BlockSpec((D_MODEL, 3 * BRANCH), row),
            pl.BlockSpec((1, BRANCH), row),
            pl.BlockSpec((1, BRANCH), row),
            pl.BlockSpec((SGU_HEADS, SGU_BLOCK, SGU_BLOCK), lambda i: (0, 0, 0)),
            pl.BlockSpec((SGU_BLOCK, BRANCH), row),
            pl.BlockSpec((BRANCH, D_MODEL), row),
            pl.BlockSpec((1, D_MODEL), row),
        ],
        out_specs=xblk,
        out_shape=jax.ShapeDtypeStruct((n, D_MODEL), _F32),
        scratch_shapes=[pltpu.VMEM((SGU_TB, BRANCH), _F32)],
        compiler_params=pltpu.CompilerParams(
            dimension_semantics=("parallel",), vmem_limit_bytes=VMEM_LIMIT),
        name="sgu_layer",
    )(x, delta, g, w_in, ln_g, ln_b, ws, bs, w_out, final_g)


def kernel(x, norm_g, final_g, s5_w_in, s5_A_re, s5_A_im, s5_log_dt, s5_B_re, s5_B_im, s5_C_re, s5_C_im, s5_D, s5_w_glu, s5_b_glu, s5_w_out, sgu_w_in, sgu_ln_g, sgu_ln_b, sgu_w_s, sgu_b_s, sgu_w_out):
    kpad, mb, mc, dec_re, dec_im = _s5_operators(
        s5_A_re[0], s5_A_im[0], s5_log_dt[0], s5_B_re[0], s5_B_im[0], s5_C_re[0], s5_C_im[0], s5_D[0])
    u, gate = _s5_in(x, norm_g[0][None, :], s5_w_in[0].astype(_BF))
    y = _s5_mix(u, kpad, mb, mc, dec_re, dec_im)
    delta = _s5_out(y, gate, s5_w_glu[0].astype(_BF), s5_b_glu[0][None, :], s5_w_out[0].astype(_BF))

    mask = jnp.tril(jnp.ones((SGU_BLOCK, SGU_BLOCK), dtype=bool))
    ws = jnp.where(mask[None], sgu_w_s[0], 0.0).astype(_BF)
    bs = jnp.repeat(jnp.transpose(sgu_b_s[0]), SGU_HEAD_DIM, axis=1)
    out = _sgu_layer(x.reshape(BATCH * SEQ, D_MODEL), delta.reshape(BATCH * SEQ, D_MODEL),
                     norm_g[1][None, :], sgu_w_in[0].astype(_BF),
                     sgu_ln_g[0][None, :], sgu_ln_b[0][None, :], ws, bs,
                     sgu_w_out[0].astype(_BF), final_g[None, :])
    return out.reshape(BATCH, SEQ, D_MODEL)
```

```python
import math

import numpy as np
import jax
import jax.numpy as jnp
from jax import lax
from jax.experimental import pallas as pl
from jax.experimental.pallas import tpu as pltpu

D_MODEL = 1024
BATCH = 16
SEQ = 2048
BRANCH = D_MODEL
S5_GROUP = 16
S5_GROUPS = BRANCH // S5_GROUP
S5_STATE = 64
SGU_BLOCK = 128
SGU_HEADS = 8
SGU_HEAD_DIM = BRANCH // SGU_HEADS
RMS_EPS = 1e-6
LN_EPS = 1e-5
RE_CLIP = -1e-4

LANES = 128
TCH = 16
NCHUNK = SEQ // TCH
SLOTS = LANES // S5_GROUP
NJ = BRANCH // LANES
GW = TCH * S5_GROUP
PAIRS_PER_J = SLOTS // 2
VMEM_LIMIT = 56 * 1024 * 1024

_HI = lax.Precision.HIGHEST
_BF = jnp.bfloat16
_F32 = jnp.float32


_GELU_C1 = math.sqrt(2.0 / math.pi)
_GELU_C2 = _GELU_C1 * 0.044715


def _gelu(x):
    inner = x * (_GELU_C1 + _GELU_C2 * (x * x))
    return (0.5 * x) * (1.0 + jnp.tanh(inner))


def _sigmoid(x):
    return jax.nn.sigmoid(x)


def _silu_tanh(x):
    h = 0.5 * x
    return h + h * jnp.tanh(h)


def _dot(a, b):
    return jnp.dot(a, b, preferred_element_type=_F32)


def _wavefront(stages, n):
    for t in range(len(stages) + n - 1):
        for i in range(n):
            k = t - i
            if 0 <= k < len(stages):
                stages[k](i)


def _rms(x, g):
    return x * lax.rsqrt(jnp.mean(x * x, axis=-1, keepdims=True) + RMS_EPS) * g


KPAD = 2 * GW


def _slot_frame_table():
    a = np.arange(SLOTS)[:, None]
    x = np.arange(TCH)[None, :]
    half, k = x // SLOTS, x % SLOTS
    return SLOTS * half + (k - a) % SLOTS


_FRAME_OF_SLOT = _slot_frame_table()


def _s5_ops_kernel(lre_ref, lim_ref, ldr_ref, ldi_ref, btr_ref, bti_ref, cr_ref, ci_ref, d_ref,
                   kpad_ref, mb_ref, mc_ref, dre_ref, dim_ref):
    P = S5_STATE
    lane = lax.broadcasted_iota(jnp.int32, (S5_GROUP, LANES), 1)
    kcol = lax.broadcasted_iota(jnp.int32, (TCH, 1), 0)
    lane256 = lax.broadcasted_iota(jnp.int32, (S5_GROUP, GW), 1)
    row256 = lax.broadcasted_iota(jnp.int32, (S5_GROUP, GW), 0)
    lane_gw = lax.broadcasted_iota(jnp.int32, (GW, LANES), 1)
    nt = (((1,), (1,)), ((), ()))
    zeros64 = jnp.zeros((P, GW), _F32)
    st = [dict() for _ in range(SLOTS)]

    def outer(t_re, t_im, m_re, m_im):
        re = [t_re[k:k + 1] * m_re - t_im[k:k + 1] * m_im for k in range(TCH)]
        im = [t_re[k:k + 1] * m_im + t_im[k:k + 1] * m_re for k in range(TCH)]
        return jnp.concatenate(re, axis=0), jnp.concatenate(im, axis=0)

    def tables(a):
        lam_re, lam_im = lre_ref[a:a + 1, :], lim_ref[a:a + 1, :]
        ldr, ldi = ldr_ref[a:a + 1, :], ldi_ref[a:a + 1, :]

        def power(k):
            mag = jnp.exp(k * ldr)
            return mag * jnp.cos(k * ldi), mag * jnp.sin(k * ldi)

        frame = (kcol & SLOTS) + (((kcol & (SLOTS - 1)) + (SLOTS - a)) & (SLOTS - 1))
        pw_re, pw_im = power(kcol.astype(_F32))
        pi_re, pi_im = power((TCH - 1 - frame).astype(_F32))
        po_re, po_im = power((frame + 1).astype(_F32))
        st[a]["p16"] = power(jnp.full((1, 1), float(TCH), _F32))

        nr, ni = pw_re[1:2] - 1.0, pw_im[1:2]
        den = lam_re * lam_re + lam_im * lam_im
        cf_re = (nr * lam_re + ni * lam_im) / den
        cf_im = (ni * lam_re - nr * lam_im) / den
        b_re, b_im = btr_ref[a], bti_ref[a]
        bb_re = cf_re * b_re - cf_im * b_im
        bb_im = cf_re * b_im + cf_im * b_re
        c_re, c_im = cr_ref[a], ci_ref[a]
        st[a]["bb"] = (bb_re, bb_im)
        st[a]["lc"] = outer(pw_re, pw_im, c_re, c_im)
        st[a]["in"] = outer(pi_re, pi_im, bb_re, bb_im)
        st[a]["ot"] = outer(po_re, po_im, c_re, c_im)

    def taps(a):
        bb_re, bb_im = st[a].pop("bb")
        lc_re, lc_im = st[a].pop("lc")
        once = lane < P
        k = (lax.dot_general(jnp.where(once, bb_re, 0.0), lc_re, nt, precision=_HI, preferred_element_type=_F32)
             - lax.dot_general(jnp.where(once, bb_im, 0.0), lc_im, nt, precision=_HI, preferred_element_type=_F32))
        kpad_ref[a, :, 0:GW] = jnp.zeros((S5_GROUP, GW), _F32)
        kpad_ref[a, :, GW:2 * GW] = k + jnp.where(lane256 == row256, d_ref[a], 0.0)

    def state_in(a):
        q, m = a // 2, a % 2
        mine = (lane_gw >= P) if m else (lane_gw < P)
        in_re, in_im = st[a].pop("in")
        mb_ref[q, m * GW:(m + 1) * GW, 0:LANES] = jnp.where(mine, in_re, 0.0).astype(_BF)
        mb_ref[q, m * GW:(m + 1) * GW, LANES:2 * LANES] = jnp.where(mine, in_im, 0.0).astype(_BF)

    def state_out(a):
        q, m = a // 2, a % 2
        ot_re, ot_im = st[a].pop("ot")
        o_re, o_im = ot_re.T[0:P], ot_im.T[0:P]
        blocks = [zeros64, zeros64, zeros64, zeros64]
        blocks[m], blocks[2 + m] = o_re, -o_im
        mc_ref[a] = jnp.concatenate(blocks, axis=0).astype(_BF)
        if m:
            first = lax.broadcasted_iota(jnp.int32, (1, LANES), 1) < P
            dre_ref[q] = jnp.where(first, st[a - 1]["p16"][0], st[a]["p16"][0])
            dim_ref[q] = jnp.where(first, st[a - 1]["p16"][1], st[a]["p16"][1])

    _wavefront([tables, taps, state_in, state_out], SLOTS)


def _s5_operators(A_re, A_im, log_dt, B_re, B_im, C_re, C_im, D):
    f32 = jnp.float32
    G, P, H = S5_GROUPS, S5_STATE, S5_GROUP
    lam_re = jnp.minimum(A_re.astype(f32), RE_CLIP)
    lam_im = A_im.astype(f32)
    dt = jnp.exp(log_dt.astype(f32))[:, None]

    twice = lambda t: jnp.concatenate([t, t], axis=-1)
    tables = [twice(lam_re), twice(lam_im), twice(lam_re * dt), twice(lam_im * dt)]
    mats = [twice(jnp.swapaxes(B_re.astype(f32), 1, 2)), twice(jnp.swapaxes(B_im.astype(f32), 1, 2)),
            twice(C_re.astype(f32)), twice(C_im.astype(f32))]
    tspec = pl.BlockSpec((SLOTS, LANES), lambda j: (j, 0))
    mspec = pl.BlockSpec((SLOTS, H, LANES), lambda j: (j, 0, 0))
    dspec = pl.BlockSpec((PAIRS_PER_J, 1, LANES), lambda j: (j, 0, 0))
    return pl.pallas_call(
        _s5_ops_kernel,
        grid=(NJ,),
        in_specs=[tspec] * 4 + [mspec] * 4 + [pl.BlockSpec((SLOTS, H, 1), lambda j: (j, 0, 0))],
        out_specs=[
            pl.BlockSpec((SLOTS, H, KPAD), lambda j: (j, 0, 0)),
            pl.BlockSpec((PAIRS_PER_J, 2 * GW, GW), lambda j: (j, 0, 0)),
            pl.BlockSpec((SLOTS, GW, GW), lambda j: (j, 0, 0)),
            dspec, dspec,
        ],
        out_shape=[
            jax.ShapeDtypeStruct((G, H, KPAD), f32),
            jax.ShapeDtypeStruct((G // 2, 2 * GW, GW), _BF),
            jax.ShapeDtypeStruct((G, GW, GW), _BF),
            jax.ShapeDtypeStruct((G // 2, 1, LANES), f32),
            jax.ShapeDtypeStruct((G // 2, 1, LANES), f32),
        ],
        compiler_params=pltpu.CompilerParams(dimension_semantics=("parallel",), vmem_limit_bytes=VMEM_LIMIT),
        name="s5_ops",
    )(*tables, *mats, D.astype(f32).reshape(G, H, 1))


IN_CT = 8


IN_MB = 1


def _s5_in_kernel(x_ref, g_ref, w_ref, u_ref, gate_ref):
    rows = BATCH * TCH
    nb = IN_CT // IN_MB
    st = [dict() for _ in range(nb)]

    def norm(i):
        parts = []
        for c in range(i * IN_MB, (i + 1) * IN_MB):
            xc = jnp.swapaxes(x_ref[:, c * TCH:(c + 1) * TCH, :], 0, 1).reshape(rows, D_MODEL)
            parts.append(_rms(xc, g_ref[...]).astype(_BF))
        st[i]["hp"] = jnp.concatenate(parts, axis=0)

    def gate_mm(i):
        gate = _dot(st[i]["hp"], w_ref[:, BRANCH:])
        gate_ref[i * IN_MB:(i + 1) * IN_MB] = (
            (gate * _sigmoid(gate)).astype(_BF).reshape(IN_MB, TCH, BATCH, BRANCH))

    def u_mm(i):
        u = _dot(st[i].pop("hp"), w_ref[:, :BRANCH])
        u_ref[i * IN_MB:(i + 1) * IN_MB] = u.astype(_BF).reshape(IN_MB, TCH, BATCH, BRANCH)

    _wavefront([norm, gate_mm, u_mm], nb)


def _s5_in(x, g, w_in):
    grid = (NCHUNK // IN_CT,)
    blk4 = (IN_CT, TCH, BATCH, BRANCH)
    return pl.pallas_call(
        _s5_in_kernel,
        grid=grid,
        in_specs=[
            pl.BlockSpec((BATCH, IN_CT * TCH, D_MODEL), lambda i: (0, i, 0)),
            pl.BlockSpec((1, D_MODEL), lambda i: (0, 0)),
            pl.BlockSpec((D_MODEL, 2 * BRANCH), lambda i: (0, 0)),
        ],
        out_specs=[
            pl.BlockSpec(blk4, lambda i: (i, 0, 0, 0)),
            pl.BlockSpec(blk4, lambda i: (i, 0, 0, 0)),
        ],
        out_shape=[
            jax.ShapeDtypeStruct((NCHUNK, TCH, BATCH, BRANCH), _BF),
            jax.ShapeDtypeStruct((NCHUNK, TCH, BATCH, BRANCH), _BF),
        ],
        compiler_params=pltpu.CompilerParams(
            dimension_semantics=("parallel",), vmem_limit_bytes=VMEM_LIMIT),
        name="s5_in",
    )(x, g, w_in)


MIX_CT = 64
MIX_R = MIX_CT * BATCH
MIX_CB = 4
MIX_RT = MIX_CB * BATCH
MIX_NH = 4
MIX_HC = MIX_CT // MIX_NH
MIX_HR = MIX_HC * BATCH

def _gather_slots(v, slot):
    rolled = [v[0]] + [pltpu.roll(v[k], k * S5_GROUP, axis=1) for k in range(1, SLOTS)]
    out = []
    for a in range(SLOTS):
        acc = rolled[0]
        for k in range(1, SLOTS):
            acc = jnp.where(slot == (a + k) % SLOTS, rolled[k], acc)
        out.append(acc)
    return out


def _scatter_slots(v, slot):
    out = []
    for k in range(SLOTS):
        acc = v[0]
        for a in range(1, SLOTS):
            acc = jnp.where(slot == (a + k) % SLOTS, v[a], acc)
        out.append(acc if k == 0 else pltpu.roll(acc, (SLOTS - k) * S5_GROUP, axis=1))
    return out


def _column_perms():
    p = np.zeros((SLOTS, GW, GW), np.float32)
    for a in range(SLOTS):
        for x in range(TCH):
            f = int(_FRAME_OF_SLOT[a, x])
            for c in range(S5_GROUP):
                p[a, f * S5_GROUP + c, x * S5_GROUP + c] = 1.0
    return p


_CPERM = _column_perms()


def _s5_mix_kernel(u_ref, kpad_ref, mb_ref, mc_ref, cperm_ref, dre_ref, dim_ref, y_ref,
                   op_ref, xs_ref, ys_ref, ss_ref, hs_ref, carry_ref):
    ct = pl.program_id(1)

    @pl.when(ct == 0)
    def _():
        carry_ref[...] = jnp.zeros_like(carry_ref)
        for a in range(SLOTS):
            kp = kpad_ref[a]
            rows = []
            for x in range(TCH):
                off = (TCH - int(_FRAME_OF_SLOT[a, x])) * S5_GROUP
                rows.append(kp[:, off:off + GW])
            toep = jnp.concatenate(rows, axis=0).astype(_BF)
            op_ref[a, 0:GW, :] = _dot(toep, cperm_ref[a]).astype(_BF)
            op_ref[a, GW:2 * GW, :] = mc_ref[a]

    slot = lax.broadcasted_iota(jnp.int32, (MIX_RT, LANES), 1) // S5_GROUP
    dre = [dre_ref[q] for q in range(PAIRS_PER_J)]
    dim_ = [dim_ref[q] for q in range(PAIRS_PER_J)]

    def relayout_in(c0):
        r0 = c0 * BATCH
        for half in range(2):
            pieces = [u_ref[c0:c0 + MIX_CB, SLOTS * half + k, :, :].reshape(MIX_RT, LANES) for k in range(SLOTS)]
            groups = _gather_slots(pieces, slot)
            for a in range(SLOTS):
                lo = (a % 2) * GW + half * LANES
                xs_ref[a // 2, r0:r0 + MIX_RT, lo:lo + LANES] = groups[a]

    def relayout_out(c0):
        r0 = c0 * BATCH
        for half in range(2):
            groups = [ys_ref[a, r0:r0 + MIX_RT, half * LANES:(half + 1) * LANES].astype(_BF) for a in range(SLOTS)]
            frames = _scatter_slots(groups, slot)
            for k in range(SLOTS):
                y_ref[c0:c0 + MIX_CB, SLOTS * half + k, :, :] = frames[k].reshape(MIX_CB, BATCH, LANES)

    def recurrence(c0, state):
        for c in range(c0, c0 + MIX_HC):
            r0 = c * BATCH
            new = []
            for q in range(PAIRS_PER_J):
                hre, him = state[2 * q], state[2 * q + 1]
                hs_ref[q, r0:r0 + BATCH, 0:LANES] = hre.astype(_BF)
                hs_ref[q, r0:r0 + BATCH, LANES:2 * LANES] = him.astype(_BF)
                sre = ss_ref[q, r0:r0 + BATCH, 0:LANES]
                sim = ss_ref[q, r0:r0 + BATCH, LANES:2 * LANES]
                new.append(dre[q] * hre - dim_[q] * him + sre)
                new.append(dre[q] * him + dim_[q] * hre + sim)
            state = new
        return state

    blocks = [(h * MIX_HC, h * MIX_HC * BATCH) for h in range(MIX_NH)]
    for c0, r0 in blocks:
        for t in range(MIX_HC // MIX_CB):
            relayout_in(c0 + t * MIX_CB)
        for q in range(PAIRS_PER_J):
            ss_ref[q, r0:r0 + MIX_HR, :] = _dot(xs_ref[q, r0:r0 + MIX_HR, :], mb_ref[q])
    state = [carry_ref[q, k] for q in range(PAIRS_PER_J) for k in range(2)]
    for c0, r0 in blocks:
        state = recurrence(c0, state)
    for q in range(PAIRS_PER_J):
        carry_ref[q, 0] = state[2 * q]
        carry_ref[q, 1] = state[2 * q + 1]
    for c0, r0 in blocks:
        for a in range(SLOTS):
            q, m = a // 2, a % 2
            lhs = jnp.concatenate([xs_ref[q, r0:r0 + MIX_HR, m * GW:(m + 1) * GW], hs_ref[q, r0:r0 + MIX_HR, :]], axis=1)
            ys_ref[a, r0:r0 + MIX_HR, :] = _dot(lhs, op_ref[a])
    for c0, r0 in blocks:
        for t in range(MIX_HC // MIX_CB):
            relayout_out(c0 + t * MIX_CB)


def _s5_mix(u, kpad, mb, mc, dec_re, dec_im):
    grid = (NJ, NCHUNK // MIX_CT)
    blk = (MIX_CT, TCH, BATCH, LANES)
    dspec = pl.BlockSpec((PAIRS_PER_J, 1, LANES), lambda j, c: (j, 0, 0))
    return pl.pallas_call(
        _s5_mix_kernel,
        grid=grid,
        in_specs=[
            pl.BlockSpec(blk, lambda j, c: (c, 0, 0, j)),
            pl.BlockSpec((SLOTS, S5_GROUP, KPAD), lambda j, c: (j, 0, 0)),
            pl.BlockSpec((PAIRS_PER_J, 2 * GW, GW), lambda j, c: (j, 0, 0)),
            pl.BlockSpec((SLOTS, GW, GW), lambda j, c: (j, 0, 0)),
            pl.BlockSpec((SLOTS, GW, GW), lambda j, c: (0, 0, 0)),
            dspec, dspec,
        ],
        out_specs=pl.BlockSpec(blk, lambda j, c: (c, 0, 0, j)),
        out_shape=jax.ShapeDtypeStruct((NCHUNK, TCH, BATCH, BRANCH), _BF),
        scratch_shapes=[
            pltpu.VMEM((SLOTS, 2 * GW, GW), _BF),
            pltpu.VMEM((PAIRS_PER_J, MIX_R, 2 * GW), _BF),
            pltpu.VMEM((SLOTS, MIX_R, GW), _F32),
            pltpu.VMEM((PAIRS_PER_J, MIX_R, GW), _F32),
            pltpu.VMEM((PAIRS_PER_J, MIX_R, GW), _BF),
            pltpu.VMEM((PAIRS_PER_J, 2, BATCH, LANES), _F32),
        ],
        compiler_params=pltpu.CompilerParams(
            dimension_semantics=("parallel", "arbitrary"), vmem_limit_bytes=VMEM_LIMIT),
        name="s5_mix",
    )(u, kpad, mb, mc, jnp.asarray(_CPERM, _BF), dec_re, dec_im)


def _s5_out_kernel(y_ref, gate_ref, wglu_ref, bglu_ref, wout_ref, o_ref):
    rows = IN_MB * BATCH * TCH
    nb = IN_CT // IN_MB
    st = [dict() for _ in range(nb)]

    def act(i):
        st[i]["y"] = _gelu(y_ref[i * IN_MB:(i + 1) * IN_MB].reshape(rows, BRANCH).astype(_F32))

    def glu_mm(i):
        st[i]["glu"] = _dot(st[i]["y"].astype(_BF), wglu_ref[...])

    def gating(i):
        y = st[i].pop("y") * _sigmoid(st[i].pop("glu") + bglu_ref[...])
        sg = gate_ref[i * IN_MB:(i + 1) * IN_MB].reshape(rows, BRANCH).astype(_F32)
        st[i]["z"] = (y * sg).astype(_BF)

    def out_mm(i):
        st[i]["out"] = _dot(st[i].pop("z"), wout_ref[...])

    def residual(i):
        out = st[i].pop("out").reshape(IN_MB, TCH, BATCH, D_MODEL)
        for k in range(IN_MB):
            c = i * IN_MB + k
            o_ref[:, c * TCH:(c + 1) * TCH, :] = jnp.swapaxes(out[k], 0, 1)

    _wavefront([act, glu_mm, gating, out_mm, residual], nb)


def _s5_out(y, gate, w_glu, b_glu, w_out):
    grid = (NCHUNK // IN_CT,)
    blk4 = (IN_CT, TCH, BATCH, BRANCH)
    xblk = pl.BlockSpec((BATCH, IN_CT * TCH, D_MODEL), lambda i: (0, i, 0))
    return pl.pallas_call(
        _s5_out_kernel,
        grid=grid,
        in_specs=[
            pl.BlockSpec(blk4, lambda i: (i, 0, 0, 0)),
            pl.BlockSpec(blk4, lambda i: (i, 0, 0, 0)),
            pl.BlockSpec((BRANCH, BRANCH), lambda i: (0, 0)),
            pl.BlockSpec((1, BRANCH), lambda i: (0, 0)),
            pl.BlockSpec((BRANCH, D_MODEL), lambda i: (0, 0)),
        ],
        out_specs=xblk,
        out_shape=jax.ShapeDtypeStruct((BATCH, SEQ, D_MODEL), _F32),
        compiler_params=pltpu.CompilerParams(
            dimension_semantics=("parallel",), vmem_limit_bytes=VMEM_LIMIT),
        name="s5_out",
    )(y, gate, w_glu, b_glu, w_out)


SGU_TB = 1024
SGU_NB = SGU_TB // SGU_BLOCK


SGU_RB = 512
SGU_BPB = SGU_RB // SGU_BLOCK


def _sgu_kernel(x_ref, d_ref, g_ref, win_ref, lng_ref, lnb_ref, ws_ref, bs_ref, wout_ref, fg_ref, o_ref, mix_ref):
    blocks = [slice(r0, r0 + SGU_RB) for r0 in range(0, SGU_TB, SGU_RB)]
    st = [dict() for _ in blocks]

    def norm_in(i):
        st[i]["h"] = _rms(x_ref[blocks[i], :] + d_ref[blocks[i], :], g_ref[...]).astype(_BF)

    def proj_v(i):
        st[i]["v"] = _dot(st[i]["h"], win_ref[:, BRANCH:2 * BRANCH])

    def proj_ug(i):
        st[i]["u"] = _dot(st[i]["h"], win_ref[:, :BRANCH])
        st[i]["gate"] = _dot(st[i]["h"], win_ref[:, 2 * BRANCH:])

    def layer_norm(i):
        v = _gelu(st[i].pop("v"))
        mu = jnp.mean(v, axis=-1, keepdims=True)
        vc = v - mu
        var = jnp.mean(vc * vc, axis=-1, keepdims=True)
        st[i]["vn"] = (vc * lax.rsqrt(var + LN_EPS) * lng_ref[...] + lnb_ref[...]).astype(_BF)

    def spatial(i):
        vn = st[i].pop("vn")
        r0 = blocks[i].start
        for hd in range(SGU_HEADS):
            lo = hd * SGU_HEAD_DIM
            rhs = jnp.concatenate(
                [vn[b * SGU_BLOCK:(b + 1) * SGU_BLOCK, lo:lo + SGU_HEAD_DIM] for b in range(SGU_BPB)], axis=1)
            res = _dot(ws_ref[hd], rhs)
            for b in range(SGU_BPB):
                mix_ref[r0 + b * SGU_BLOCK:r0 + (b + 1) * SGU_BLOCK, lo:lo + SGU_HEAD_DIM] = (
                    res[:, b * SGU_HEAD_DIM:(b + 1) * SGU_HEAD_DIM] + bs_ref[:, lo:lo + SGU_HEAD_DIM])

    def gating(i):
        gate = st[i].pop("gate")
        y = _gelu(st[i].pop("u")) * mix_ref[blocks[i], :]
        st[i]["z"] = (y * _silu_tanh(gate)).astype(_BF)

    def proj_out(i):
        st[i]["out"] = _dot(st[i].pop("z"), wout_ref[...])

    def norm_out(i):
        x2 = (x_ref[blocks[i], :] + d_ref[blocks[i], :]) + st[i].pop("out")
        o_ref[blocks[i], :] = _rms(x2, fg_ref[...])

    _wavefront([norm_in, proj_v, layer_norm, proj_ug, spatial, gating, proj_out, norm_out], len(blocks))


def _sgu_layer(x, delta, g, w_in, ln_g, ln_b, ws, bs, w_out, final_g):
    n = BATCH * SEQ
    grid = (n // SGU_TB,)
    row = lambda i: (0, 0)
    xblk = pl.BlockSpec((SGU_TB, D_MODEL), lambda i: (i, 0))
    return pl.pallas_call(
        _sgu_kernel,
        grid=grid,
        in_specs=[
            xblk,
            xblk,
            pl.BlockSpec((1, D_MODEL), row),
            pl.BlockSpec((D_MODEL, 3 * BRANCH), row),
            pl.BlockSpec((1, BRANCH), row),
            pl.BlockSpec((1, BRANCH), row),
            pl.BlockSpec((SGU_HEADS, SGU_BLOCK, SGU_BLOCK), lambda i: (0, 0, 0)),
            pl.BlockSpec((SGU_BLOCK, BRANCH), row),
            pl.BlockSpec((BRANCH, D_MODEL), row),
            pl.BlockSpec((1, D_MODEL), row),
        ],
        out_specs=xblk,
        out_shape=jax.ShapeDtypeStruct((n, D_MODEL), _F32),
        scratch_shapes=[pltpu.VMEM((SGU_TB, BRANCH), _F32)],
        compiler_params=pltpu.CompilerParams(
            dimension_semantics=("parallel",), vmem_limit_bytes=VMEM_LIMIT),
        name="sgu_layer",
    )(x, delta, g, w_in, ln_g, ln_b, ws, bs, w_out, final_g)


def kernel(x, norm_g, final_g, s5_w_in, s5_A_re, s5_A_im, s5_log_dt, s5_B_re, s5_B_im, s5_C_re, s5_C_im, s5_D, s5_w_glu, s5_b_glu, s5_w_out, sgu_w_in, sgu_ln_g, sgu_ln_b, sgu_w_s, sgu_b_s, sgu_w_out):
    kpad, mb, mc, dec_re, dec_im = _s5_operators(
        s5_A_re[0], s5_A_im[0], s5_log_dt[0], s5_B_re[0], s5_B_im[0], s5_C_re[0], s5_C_im[0], s5_D[0])
    u, gate = _s5_in(x, norm_g[0][None, :], s5_w_in[0].astype(_BF))
    y = _s5_mix(u, kpad, mb, mc, dec_re, dec_im)
    delta = _s5_out(y, gate, s5_w_glu[0].astype(_BF), s5_b_glu[0][None, :], s5_w_out[0].astype(_BF))

    mask = jnp.tril(jnp.ones((SGU_BLOCK, SGU_BLOCK), dtype=bool))
    ws = jnp.where(mask[None], sgu_w_s[0], 0.0).astype(_BF)
    bs = jnp.repeat(jnp.transpose(sgu_b_s[0]), SGU_HEAD_DIM, axis=1)
    out = _sgu_layer(x.reshape(BATCH * SEQ, D_MODEL), delta.reshape(BATCH * SEQ, D_MODEL),
                     norm_g[1][None, :], sgu_w_in[0].astype(_BF),
                     sgu_ln_g[0][None, :], sgu_ln_b[0][None, :], ws, bs,
                     sgu_w_out[0].astype(_BF), final_g[None, :])
    return out.reshape(BATCH, SEQ, D_MODEL)
```

```python
import math

import numpy as np
import jax
import jax.numpy as jnp
from jax import lax
from jax.experimental import pallas as pl
from jax.experimental.pallas import tpu as pltpu

D_MODEL = 1024
BATCH = 16
SEQ = 2048
BRANCH = D_MODEL
S5_GROUP = 16
S5_GROUPS = BRANCH // S5_GROUP
S5_STATE = 64
SGU_BLOCK = 128
SGU_HEADS = 8
SGU_HEAD_DIM = BRANCH // SGU_HEADS
RMS_EPS = 1e-6
LN_EPS = 1e-5
RE_CLIP = -1e-4

LANES = 128
TCH = 16
NCHUNK = SEQ // TCH
SLOTS = LANES // S5_GROUP
NJ = BRANCH // LANES
GW = TCH * S5_GROUP
PAIRS_PER_J = SLOTS // 2
VMEM_LIMIT = 56 * 1024 * 1024

_HI = lax.Precision.HIGHEST
_BF = jnp.bfloat16
_F32 = jnp.float32


_GELU_C1 = math.sqrt(2.0 / math.pi)
_GELU_C2 = _GELU_C1 * 0.044715


def _gelu(x):
    inner = x * (_GELU_C1 + _GELU_C2 * (x * x))
    return (0.5 * x) * (1.0 + jnp.tanh(inner))


def _sigmoid(x):
    return jax.nn.sigmoid(x)


def _silu_tanh(x):
    h = 0.5 * x
    return h + h * jnp.tanh(h)


def _dot(a, b):
    return jnp.dot(a, b, preferred_element_type=_F32)


def _wavefront(stages, n):
    for t in range(len(stages) + n - 1):
        for i in range(n):
            k = t - i
            if 0 <= k < len(stages):
                stages[k](i)


def _rms(x, g):
    return x * lax.rsqrt(jnp.mean(x * x, axis=-1, keepdims=True) + RMS_EPS) * g


KPAD = 2 * GW


def _slot_frame_table():
    a = np.arange(SLOTS)[:, None]
    x = np.arange(TCH)[None, :]
    half, k = x // SLOTS, x % SLOTS
    return SLOTS * half + (k - a) % SLOTS


_FRAME_OF_SLOT = _slot_frame_table()


def _s5_ops_kernel(lre_ref, lim_ref, ldr_ref, ldi_ref, btr_ref, bti_ref, cr_ref, ci_ref, d_ref,
                   kpad_ref, mb_ref, mc_ref, dre_ref, dim_ref):
    P = S5_STATE
    lane = lax.broadcasted_iota(jnp.int32, (S5_GROUP, LANES), 1)
    kcol = lax.broadcasted_iota(jnp.int32, (TCH, 1), 0)
    lane256 = lax.broadcasted_iota(jnp.int32, (S5_GROUP, GW), 1)
    row256 = lax.broadcasted_iota(jnp.int32, (S5_GROUP, GW), 0)
    lane_gw = lax.broadcasted_iota(jnp.int32, (GW, LANES), 1)
    nt = (((1,), (1,)), ((), ()))
    zeros64 = jnp.zeros((P, GW), _F32)
    st = [dict() for _ in range(SLOTS)]

    def outer(t_re, t_im, m_re, m_im):
        re = [t_re[k:k + 1] * m_re - t_im[k:k + 1] * m_im for k in range(TCH)]
        im = [t_re[k:k + 1] * m_im + t_im[k:k + 1] * m_re for k in range(TCH)]
        return jnp.concatenate(re, axis=0), jnp.concatenate(im, axis=0)

    def tables(a):
        lam_re, lam_im = lre_ref[a:a + 1, :], lim_ref[a:a + 1, :]
        ldr, ldi = ldr_ref[a:a + 1, :], ldi_ref[a:a + 1, :]

        def power(k):
            mag = jnp.exp(k * ldr)
            return mag * jnp.cos(k * ldi), mag * jnp.sin(k * ldi)

        frame = (kcol & SLOTS) + (((kcol & (SLOTS - 1)) + (SLOTS - a)) & (SLOTS - 1))
        pw_re, pw_im = power(kcol.astype(_F32))
        pi_re, pi_im = power((TCH - 1 - frame).astype(_F32))
        po_re, po_im = power((frame + 1).astype(_F32))
        st[a]["p16"] = power(jnp.full((1, 1), float(TCH), _F32))

        nr, ni = pw_re[1:2] - 1.0, pw_im[1:2]
        den = lam_re * lam_re + lam_im * lam_im
        cf_re = (nr * lam_re + ni * lam_im) / den
        cf_im = (ni * lam_re - nr * lam_im) / den
        b_re, b_im = btr_ref[a], bti_ref[a]
        bb_re = cf_re * b_re - cf_im * b_im
        bb_im = cf_re * b_im + cf_im * b_re
        c_re, c_im = cr_ref[a], ci_ref[a]
        st[a]["bb"] = (bb_re, bb_im)
        st[a]["lc"] = outer(pw_re, pw_im, c_re, c_im)
        st[a]["in"] = outer(pi_re, pi_im, bb_re, bb_im)
        st[a]["ot"] = outer(po_re, po_im, c_re, c_im)

    def taps(a):
        bb_re, bb_im = st[a].pop("bb")
        lc_re, lc_im = st[a].pop("lc")
        once = lane < P
        k = (lax.dot_general(jnp.where(once, bb_re, 0.0), lc_re, nt, precision=_HI, preferred_element_type=_F32)
             - lax.dot_general(jnp.where(once, bb_im, 0.0), lc_im, nt, precision=_HI, preferred_element_type=_F32))
        kpad_ref[a, :, 0:GW] = jnp.zeros((S5_GROUP, GW), _F32)
        kpad_ref[a, :, GW:2 * GW] = k + jnp.where(lane256 == row256, d_ref[a], 0.0)

    def state_in(a):
        q, m = a // 2, a % 2
        mine = (lane_gw >= P) if m else (lane_gw < P)
        in_re, in_im = st[a].pop("in")
        mb_ref[q, m * GW:(m + 1) * GW, 0:LANES] = jnp.where(mine, in_re, 0.0).astype(_BF)
        mb_ref[q, m * GW:(m + 1) * GW, LANES:2 * LANES] = jnp.where(mine, in_im, 0.0).astype(_BF)

    def state_out(a):
        q, m = a // 2, a % 2
        ot_re, ot_im = st[a].pop("ot")
        o_re, o_im = ot_re.T[0:P], ot_im.T[0:P]
        blocks = [zeros64, zeros64, zeros64, zeros64]
        blocks[m], blocks[2 + m] = o_re, -o_im
        mc_ref[a] = jnp.concatenate(blocks, axis=0).astype(_BF)
        if m:
            first = lax.broadcasted_iota(jnp.int32, (1, LANES), 1) < P
            dre_ref[q] = jnp.where(first, st[a - 1]["p16"][0], st[a]["p16"][0])
            dim_ref[q] = jnp.where(first, st[a - 1]["p16"][1], st[a]["p16"][1])

    _wavefront([tables, taps, state_in, state_out], SLOTS)


def _s5_operators(A_re, A_im, log_dt, B_re, B_im, C_re, C_im, D):
    f32 = jnp.float32
    G, P, H = S5_GROUPS, S5_STATE, S5_GROUP
    lam_re = jnp.minimum(A_re.astype(f32), RE_CLIP)
    lam_im = A_im.astype(f32)
    dt = jnp.exp(log_dt.astype(f32))[:, None]

    twice = lambda t: jnp.concatenate([t, t], axis=-1)
    tables = [twice(lam_re), twice(lam_im), twice(lam_re * dt), twice(lam_im * dt)]
    mats = [twice(jnp.swapaxes(B_re.astype(f32), 1, 2)), twice(jnp.swapaxes(B_im.astype(f32), 1, 2)),
            twice(C_re.astype(f32)), twice(C_im.astype(f32))]
    tspec = pl.BlockSpec((SLOTS, LANES), lambda j: (j, 0))
    mspec = pl.BlockSpec((SLOTS, H, LANES), lambda j: (j, 0, 0))
    dspec = pl.BlockSpec((PAIRS_PER_J, 1, LANES), lambda j: (j, 0, 0))
    return pl.pallas_call(
        _s5_ops_kernel,
        grid=(NJ,),
        in_specs=[tspec] * 4 + [mspec] * 4 + [pl.BlockSpec((SLOTS, H, 1), lambda j: (j, 0, 0))],
        out_specs=[
            pl.BlockSpec((SLOTS, H, KPAD), lambda j: (j, 0, 0)),
            pl.BlockSpec((PAIRS_PER_J, 2 * GW, GW), lambda j: (j, 0, 0)),
            pl.BlockSpec((SLOTS, GW, GW), lambda j: (j, 0, 0)),
            dspec, dspec,
        ],
        out_shape=[
            jax.ShapeDtypeStruct((G, H, KPAD), f32),
            jax.ShapeDtypeStruct((G // 2, 2 * GW, GW), _BF),
            jax.ShapeDtypeStruct((G, GW, GW), _BF),
            jax.ShapeDtypeStruct((G // 2, 1, LANES), f32),
            jax.ShapeDtypeStruct((G // 2, 1, LANES), f32),
        ],
        compiler_params=pltpu.CompilerParams(dimension_semantics=("parallel",), vmem_limit_bytes=VMEM_LIMIT),
        name="s5_ops",
    )(*tables, *mats, D.astype(f32).reshape(G, H, 1))


IN_CT = 8
IN_MB = 1


def _s5_in_kernel(x_ref, g_ref, w_ref, u_ref, gate_ref):
    rows = BATCH * TCH
    nb = IN_CT // IN_MB
    st = [dict() for _ in range(nb)]

    def norm(i):
        parts = []
        for c in range(i * IN_MB, (i + 1) * IN_MB):
            hc = _rms(x_ref[:, c * TCH:(c + 1) * TCH, :], g_ref[...]).astype(_BF)
            parts.append(jnp.swapaxes(hc, 0, 1).reshape(rows, D_MODEL))
        st[i]["hp"] = jnp.concatenate(parts, axis=0)

    def gate_mm(i):
        gate = _dot(st[i]["hp"], w_ref[:, BRANCH:])
        gate_ref[i * IN_MB:(i + 1) * IN_MB] = (
            (gate * _sigmoid(gate)).astype(_BF).reshape(IN_MB, TCH, BATCH, BRANCH))

    def u_mm(i):
        u = _dot(st[i].pop("hp"), w_ref[:, :BRANCH])
        u_ref[i * IN_MB:(i + 1) * IN_MB] = u.astype(_BF).reshape(IN_MB, TCH, BATCH, BRANCH)

    _wavefront([norm, gate_mm, u_mm], nb)


def _s5_in(x, g, w_in):
    grid = (NCHUNK // IN_CT,)
    blk4 = (IN_CT, TCH, BATCH, BRANCH)
    return pl.pallas_call(
        _s5_in_kernel,
        grid=grid,
        in_specs=[
            pl.BlockSpec((BATCH, IN_CT * TCH, D_MODEL), lambda i: (0, i, 0)),
            pl.BlockSpec((1, D_MODEL), lambda i: (0, 0)),
            pl.BlockSpec((D_MODEL, 2 * BRANCH), lambda i: (0, 0)),
        ],
        out_specs=[
            pl.BlockSpec(blk4, lambda i: (i, 0, 0, 0)),
            pl.BlockSpec(blk4, lambda i: (i, 0, 0, 0)),
        ],
        out_shape=[
            jax.ShapeDtypeStruct((NCHUNK, TCH, BATCH, BRANCH), _BF),
            jax.ShapeDtypeStruct((NCHUNK, TCH, BATCH, BRANCH), _BF),
        ],
        compiler_params=pltpu.CompilerParams(
            dimension_semantics=("parallel",), vmem_limit_bytes=VMEM_LIMIT),
        name="s5_in",
    )(x, g, w_in)


MIX_CT = 64
MIX_R = MIX_CT * BATCH
MIX_CB = 4
MIX_RT = MIX_CB * BATCH
MIX_NH = 4
MIX_HC = MIX_CT // MIX_NH
MIX_HR = MIX_HC * BATCH


def _gather_slots(v, slot):
    rolled = [v[0]] + [pltpu.roll(v[k], k * S5_GROUP, axis=1) for k in range(1, SLOTS)]
    out = []
    for a in range(SLOTS):
        acc = rolled[0]
        for k in range(1, SLOTS):
            acc = jnp.where(slot == (a + k) % SLOTS, rolled[k], acc)
        out.append(acc)
    return out


def _scatter_slots(v, slot):
    out = []
    for k in range(SLOTS):
        acc = v[0]
        for a in range(1, SLOTS):
            acc = jnp.where(slot == (a + k) % SLOTS, v[a], acc)
        out.append(acc if k == 0 else pltpu.roll(acc, (SLOTS - k) * S5_GROUP, axis=1))
    return out


def _column_perms():
    p = np.zeros((SLOTS, GW, GW), np.float32)
    for a in range(SLOTS):
        for x in range(TCH):
            f = int(_FRAME_OF_SLOT[a, x])
            for c in range(S5_GROUP):
                p[a, f * S5_GROUP + c, x * S5_GROUP + c] = 1.0
    return p


_CPERM = _column_perms()


def _s5_mix_kernel(u_ref, kpad_ref, mb_ref, mc_ref, cperm_ref, dre_ref, dim_ref, y_ref,
                   op_ref, xs_ref, ys_ref, ss_ref, hs_ref, carry_ref):
    ct = pl.program_id(1)

    @pl.when(ct == 0)
    def _():
        carry_ref[...] = jnp.zeros_like(carry_ref)
        for a in range(SLOTS):
            kp = kpad_ref[a]
            rows = []
            for x in range(TCH):
                off = (TCH - int(_FRAME_OF_SLOT[a, x])) * S5_GROUP
                rows.append(kp[:, off:off + GW])
            toep = jnp.concatenate(rows, axis=0).astype(_BF)
            op_ref[a, 0:GW, :] = _dot(toep, cperm_ref[a]).astype(_BF)
            op_ref[a, GW:2 * GW, :] = mc_ref[a]

    slot = lax.broadcasted_iota(jnp.int32, (MIX_RT, LANES), 1) // S5_GROUP
    dre = [dre_ref[q] for q in range(PAIRS_PER_J)]
    dim_ = [dim_ref[q] for q in range(PAIRS_PER_J)]

    def relayout_in(c0):
        r0 = c0 * BATCH
        for half in range(2):
            pieces = [u_ref[c0:c0 + MIX_CB, SLOTS * half + k, :, :].reshape(MIX_RT, LANES) for k in range(SLOTS)]
            groups = _gather_slots(pieces, slot)
            for a in range(SLOTS):
                lo = (a % 2) * GW + half * LANES
                xs_ref[a // 2, r0:r0 + MIX_RT, lo:lo + LANES] = groups[a]

    def relayout_out(c0):
        r0 = c0 * BATCH
        for half in range(2):
            groups = [ys_ref[a, r0:r0 + MIX_RT, half * LANES:(half + 1) * LANES].astype(_BF) for a in range(SLOTS)]
            frames = _scatter_slots(groups, slot)
            for k in range(SLOTS):
                y_ref[c0:c0 + MIX_CB, SLOTS * half + k, :, :] = frames[k].reshape(MIX_CB, BATCH, LANES)

    def recurrence(c0, state):
        for c in range(c0, c0 + MIX_HC):
            r0 = c * BATCH
            new = []
            for q in range(PAIRS_PER_J):
                hre, him = state[2 * q], state[2 * q + 1]
                hs_ref[q, r0:r0 + BATCH, 0:LANES] = hre.astype(_BF)
                hs_ref[q, r0:r0 + BATCH, LANES:2 * LANES] = him.astype(_BF)
                sre = ss_ref[q, r0:r0 + BATCH, 0:LANES]
                sim = ss_ref[q, r0:r0 + BATCH, LANES:2 * LANES]
                new.append(dre[q] * hre - dim_[q] * him + sre)
                new.append(dre[q] * him + dim_[q] * hre + sim)
            state = new
        return state

    blocks = [(h * MIX_HC, h * MIX_HC * BATCH) for h in range(MIX_NH)]
    for c0, r0 in blocks:
        for t in range(MIX_HC // MIX_CB):
            relayout_in(c0 + t * MIX_CB)
        for q in range(PAIRS_PER_J):
            ss_ref[q, r0:r0 + MIX_HR, :] = _dot(xs_ref[q, r0:r0 + MIX_HR, :], mb_ref[q])
    state = [carry_ref[q, k] for q in range(PAIRS_PER_J) for k in range(2)]
    for c0, r0 in blocks:
        state = recurrence(c0, state)
    for q in range(PAIRS_PER_J):
        carry_ref[q, 0] = state[2 * q]
        carry_ref[q, 1] = state[2 * q + 1]
    for c0, r0 in blocks:
        for a in range(SLOTS):
            q, m = a // 2, a % 2
            lhs = jnp.concatenate([xs_ref[q, r0:r0 + MIX_HR, m * GW:(m + 1) * GW], hs_ref[q, r0:r0 + MIX_HR, :]], axis=1)
            ys_ref[a, r0:r0 + MIX_HR, :] = _dot(lhs, op_ref[a])
    for c0, r0 in blocks:
        for t in range(MIX_HC // MIX_CB):
            relayout_out(c0 + t * MIX_CB)


def _s5_mix(u, kpad, mb, mc, dec_re, dec_im):
    grid = (NJ, NCHUNK // MIX_CT)
    blk = (MIX_CT, TCH, BATCH, LANES)
    dspec = pl.BlockSpec((PAIRS_PER_J, 1, LANES), lambda j, c: (j, 0, 0))
    return pl.pallas_call(
        _s5_mix_kernel,
        grid=grid,
        in_specs=[
            pl.BlockSpec(blk, lambda j, c: (c, 0, 0, j)),
            pl.BlockSpec((SLOTS, S5_GROUP, KPAD), lambda j, c: (j, 0, 0)),
            pl.BlockSpec((PAIRS_PER_J, 2 * GW, GW), lambda j, c: (j, 0, 0)),
            pl.BlockSpec((SLOTS, GW, GW), lambda j, c: (j, 0, 0)),
            pl.BlockSpec((SLOTS, GW, GW), lambda j, c: (0, 0, 0)),
            dspec, dspec,
        ],
        out_specs=pl.BlockSpec(blk, lambda j, c: (c, 0, 0, j)),
        out_shape=jax.ShapeDtypeStruct((NCHUNK, TCH, BATCH, BRANCH), _BF),
        scratch_shapes=[
            pltpu.VMEM((SLOTS, 2 * GW, GW), _BF),
            pltpu.VMEM((PAIRS_PER_J, MIX_R, 2 * GW), _BF),
            pltpu.VMEM((SLOTS, MIX_R, GW), _F32),
            pltpu.VMEM((PAIRS_PER_J, MIX_R, GW), _F32),
            pltpu.VMEM((PAIRS_PER_J, MIX_R, GW), _BF),
            pltpu.VMEM((PAIRS_PER_J, 2, BATCH, LANES), _F32),
        ],
        compiler_params=pltpu.CompilerParams(
            dimension_semantics=("parallel", "arbitrary"), vmem_limit_bytes=VMEM_LIMIT),
        name="s5_mix",
    )(u, kpad, mb, mc, jnp.asarray(_CPERM, _BF), dec_re, dec_im)


def _s5_out_kernel(y_ref, gate_ref, wglu_ref, bglu_ref, wout_ref, o_ref):
    rows = IN_MB * BATCH * TCH
    nb = IN_CT // IN_MB
    st = [dict() for _ in range(nb)]

    def act(i):
        st[i]["y"] = _gelu(y_ref[i * IN_MB:(i + 1) * IN_MB].reshape(rows, BRANCH).astype(_F32))

    def glu_mm(i):
        st[i]["glu"] = _dot(st[i]["y"].astype(_BF), wglu_ref[...])

    def gating(i):
        y = st[i].pop("y") * _sigmoid(st[i].pop("glu") + bglu_ref[...])
        sg = gate_ref[i * IN_MB:(i + 1) * IN_MB].reshape(rows, BRANCH).astype(_F32)
        z = (y * sg).astype(_BF).reshape(IN_MB, TCH, BATCH, BRANCH)
        parts = [jnp.swapaxes(z[k], 0, 1).reshape(BATCH * TCH, BRANCH) for k in range(IN_MB)]
        st[i]["z"] = parts[0] if IN_MB == 1 else jnp.concatenate(parts, axis=0)

    def out_mm(i):
        st[i]["out"] = _dot(st[i].pop("z"), wout_ref[...])

    def residual(i):
        out = st[i].pop("out").reshape(IN_MB, BATCH, TCH, D_MODEL)
        for k in range(IN_MB):
            c = i * IN_MB + k
            o_ref[:, c * TCH:(c + 1) * TCH, :] = out[k]

    _wavefront([act, glu_mm, gating, out_mm, residual], nb)


def _s5_out(y, gate, w_glu, b_glu, w_out):
    grid = (NCHUNK // IN_CT,)
    blk4 = (IN_CT, TCH, BATCH, BRANCH)
    xblk = pl.BlockSpec((BATCH, IN_CT * TCH, D_MODEL), lambda i: (0, i, 0))
    return pl.pallas_call(
        _s5_out_kernel,
        grid=grid,
        in_specs=[
            pl.BlockSpec(blk4, lambda i: (i, 0, 0, 0)),
            pl.BlockSpec(blk4, lambda i: (i, 0, 0, 0)),
            pl.BlockSpec((BRANCH, BRANCH), lambda i: (0, 0)),
            pl.BlockSpec((1, BRANCH), lambda i: (0, 0)),
            pl.BlockSpec((BRANCH, D_MODEL), lambda i: (0, 0)),
        ],
        out_specs=xblk,
        out_shape=jax.ShapeDtypeStruct((BATCH, SEQ, D_MODEL), _F32),
        compiler_params=pltpu.CompilerParams(
            dimension_semantics=("parallel",), vmem_limit_bytes=VMEM_LIMIT),
        name="s5_out",
    )(y, gate, w_glu, b_glu, w_out)


SGU_TB = 1024
SGU_RB = 512
SGU_BPB = SGU_RB // SGU_BLOCK


def _sgu_kernel(x_ref, d_ref, g_ref, win_ref, lng_ref, lnb_ref, ws_ref, bs_ref, wout_ref, fg_ref, o_ref, mix_ref):
    blocks = [slice(r0, r0 + SGU_RB) for r0 in range(0, SGU_TB, SGU_RB)]
    st = [dict() for _ in blocks]

    def norm_in(i):
        st[i]["h"] = _rms(x_ref[blocks[i], :] + d_ref[blocks[i], :], g_ref[...]).astype(_BF)

    def proj_v(i):
        st[i]["v"] = _dot(st[i]["h"], win_ref[:, BRANCH:2 * BRANCH])

    def proj_ug(i):
        st[i]["u"] = _dot(st[i]["h"], win_ref[:, :BRANCH])
        st[i]["gate"] = _dot(st[i]["h"], win_ref[:, 2 * BRANCH:])

    def layer_norm(i):
        v = _gelu(st[i].pop("v"))
        mu = jnp.mean(v, axis=-1, keepdims=True)
        vc = v - mu
        var = jnp.mean(vc * vc, axis=-1, keepdims=True)
        st[i]["vn"] = (vc * lax.rsqrt(var + LN_EPS) * lng_ref[...] + lnb_ref[...]).astype(_BF)

    def spatial(i):
        vn = st[i].pop("vn")
        r0 = blocks[i].start
        for hd in range(SGU_HEADS):
            lo = hd * SGU_HEAD_DIM
            rhs = jnp.concatenate(
                [vn[b * SGU_BLOCK:(b + 1) * SGU_BLOCK, lo:lo + SGU_HEAD_DIM] for b in range(SGU_BPB)], axis=1)
            res = _dot(ws_ref[hd], rhs)
            for b in range(SGU_BPB):
                mix_ref[r0 + b * SGU_BLOCK:r0 + (b + 1) * SGU_BLOCK, lo:lo + SGU_HEAD_DIM] = (
                    res[:, b * SGU_HEAD_DIM:(b + 1) * SGU_HEAD_DIM] + bs_ref[:, lo:lo + SGU_HEAD_DIM])

    def gating(i):
        gate = st[i].pop("gate")
        y = _gelu(st[i].pop("u")) * mix_ref[blocks[i], :]
        st[i]["z"] = (y * _silu_tanh(gate)).astype(_BF)

    def proj_out(i):
        st[i]["out"] = _dot(st[i].pop("z"), wout_ref[...])

    def norm_out(i):
        x2 = (x_ref[blocks[i], :] + d_ref[blocks[i], :]) + st[i].pop("out")
        o_ref[blocks[i], :] = _rms(x2, fg_ref[...])

    _wavefront([norm_in, proj_v, layer_norm, proj_ug, spatial, gating, proj_out, norm_out], len(blocks))


def _sgu_layer(x, delta, g, w_in, ln_g, ln_b, ws, bs, w_out, final_g):
    n = BATCH * SEQ
    grid = (n // SGU_TB,)
    row = lambda i: (0, 0)
    xblk = pl.BlockSpec((SGU_TB, D_MODEL), lambda i: (i, 0))
    return pl.pallas_call(
        _sgu_kernel,
        grid=grid,
        in_specs=[
            xblk,
            xblk,
            pl.BlockSpec((1, D_MODEL), row),
            pl.BlockSpec((D_MODEL, 3 * BRANCH), row),
            pl.BlockSpec((1, BRANCH), row),
            pl.BlockSpec((1, BRANCH), row),
            pl.BlockSpec((SGU_HEADS, SGU_BLOCK, SGU_BLOCK), lambda i: (0, 0, 0)),
            pl.BlockSpec((SGU_BLOCK, BRANCH), row),
            pl.BlockSpec((BRANCH, D_MODEL), row),
            pl.BlockSpec((1, D_MODEL), row),
        ],
        out_specs=xblk,
        out_shape=jax.ShapeDtypeStruct((n, D_MODEL), _F32),
        scratch_shapes=[pltpu.VMEM((SGU_TB, BRANCH), _F32)],
        compiler_params=pltpu.CompilerParams(
            dimension_semantics=("parallel",), vmem_limit_bytes=VMEM_LIMIT),
        name="sgu_layer",
    )(x, delta, g, w_in, ln_g, ln_b, ws, bs, w_out, final_g)


def kernel(x, norm_g, final_g, s5_w_in, s5_A_re, s5_A_im, s5_log_dt, s5_B_re, s5_B_im, s5_C_re, s5_C_im, s5_D, s5_w_glu, s5_b_glu, s5_w_out, sgu_w_in, sgu_ln_g, sgu_ln_b, sgu_w_s, sgu_b_s, sgu_w_out):
    kpad, mb, mc, dec_re, dec_im = _s5_operators(
        s5_A_re[0], s5_A_im[0], s5_log_dt[0], s5_B_re[0], s5_B_im[0], s5_C_re[0], s5_C_im[0], s5_D[0])
    u, gate = _s5_in(x, norm_g[0][None, :], s5_w_in[0].astype(_BF))
    y = _s5_mix(u, kpad, mb, mc, dec_re, dec_im)
    delta = _s5_out(y, gate, s5_w_glu[0].astype(_BF), s5_b_glu[0][None, :], s5_w_out[0].astype(_BF))

    mask = jnp.tril(jnp.ones((SGU_BLOCK, SGU_BLOCK), dtype=bool))
    ws = jnp.where(mask[None], sgu_w_s[0], 0.0).astype(_BF)
    bs = jnp.repeat(jnp.transpose(sgu_b_s[0]), SGU_HEAD_DIM, axis=1)
    out = _sgu_layer(x.reshape(BATCH * SEQ, D_MODEL), delta.reshape(BATCH * SEQ, D_MODEL),
                     norm_g[1][None, :], sgu_w_in[0].astype(_BF),
                     sgu_ln_g[0][None, :], sgu_ln_b[0][None, :], ws, bs,
                     sgu_w_out[0].astype(_BF), final_g[None, :])
    return out.reshape(BATCH, SEQ, D_MODEL)
```

```python
import math

import numpy as np
import jax
import jax.numpy as jnp
from jax import lax
from jax.experimental import pallas as pl
from jax.experimental.pallas import tpu as pltpu

D_MODEL = 1024
BATCH = 16
SEQ = 2048
BRANCH = D_MODEL
S5_GROUP = 16
S5_GROUPS = BRANCH // S5_GROUP
S5_STATE = 64
SGU_BLOCK = 128
SGU_HEADS = 8
SGU_HEAD_DIM = BRANCH // SGU_HEADS
RMS_EPS = 1e-6
LN_EPS = 1e-5
RE_CLIP = -1e-4

LANES = 128
TCH = 16
NCHUNK = SEQ // TCH
SLOTS = LANES // S5_GROUP
NJ = BRANCH // LANES
GW = TCH * S5_GROUP
PAIRS_PER_J = SLOTS // 2
VMEM_LIMIT = 56 * 1024 * 1024

_HI = lax.Precision.HIGHEST
_BF = jnp.bfloat16
_F32 = jnp.float32


_GELU_C1 = math.sqrt(2.0 / math.pi)
_GELU_C2 = _GELU_C1 * 0.044715


def _gelu(x):
    inner = x * (_GELU_C1 + _GELU_C2 * (x * x))
    return (0.5 * x) * (1.0 + jnp.tanh(inner))


def _sigmoid(x):
    return jax.nn.sigmoid(x)


def _silu_tanh(x):
    h = 0.5 * x
    return h + h * jnp.tanh(h)


def _dot(a, b):
    return jnp.dot(a, b, preferred_element_type=_F32)


def _wavefront(stages, n):
    for t in range(len(stages) + n - 1):
        for i in range(n):
            k = t - i
            if 0 <= k < len(stages):
                stages[k](i)


def _rms(x, g):
    return x * lax.rsqrt(jnp.mean(x * x, axis=-1, keepdims=True) + RMS_EPS) * g


KPAD = 2 * GW


def _slot_frame_table():
    a = np.arange(SLOTS)[:, None]
    x = np.arange(TCH)[None, :]
    half, k = x // SLOTS, x % SLOTS
    return SLOTS * half + (k - a) % SLOTS


_FRAME_OF_SLOT = _slot_frame_table()


def _s5_ops_kernel(lre_ref, lim_ref, ldr_ref, ldi_ref, btr_ref, bti_ref, cr_ref, ci_ref, d_ref,
                   kpad_ref, mb_ref, mc_ref, dre_ref, dim_ref):
    P = S5_STATE
    lane = lax.broadcasted_iota(jnp.int32, (S5_GROUP, LANES), 1)
    kcol = lax.broadcasted_iota(jnp.int32, (TCH, 1), 0)
    lane256 = lax.broadcasted_iota(jnp.int32, (S5_GROUP, GW), 1)
    row256 = lax.broadcasted_iota(jnp.int32, (S5_GROUP, GW), 0)
    lane_gw = lax.broadcasted_iota(jnp.int32, (GW, LANES), 1)
    nt = (((1,), (1,)), ((), ()))
    zeros64 = jnp.zeros((P, GW), _F32)
    st = [dict() for _ in range(SLOTS)]

    def outer(t_re, t_im, m_re, m_im):
        re = [t_re[k:k + 1] * m_re - t_im[k:k + 1] * m_im for k in range(TCH)]
        im = [t_re[k:k + 1] * m_im + t_im[k:k + 1] * m_re for k in range(TCH)]
        return jnp.concatenate(re, axis=0), jnp.concatenate(im, axis=0)

    def tables(a):
        lam_re, lam_im = lre_ref[a:a + 1, :], lim_ref[a:a + 1, :]
        ldr, ldi = ldr_ref[a:a + 1, :], ldi_ref[a:a + 1, :]

        def power(k):
            mag = jnp.exp(k * ldr)
            return mag * jnp.cos(k * ldi), mag * jnp.sin(k * ldi)

        frame = (kcol & SLOTS) + (((kcol & (SLOTS - 1)) + (SLOTS - a)) & (SLOTS - 1))
        pw_re, pw_im = power(kcol.astype(_F32))
        pi_re, pi_im = power((TCH - 1 - frame).astype(_F32))
        po_re, po_im = power((frame + 1).astype(_F32))
        st[a]["p16"] = power(jnp.full((1, 1), float(TCH), _F32))

        nr, ni = pw_re[1:2] - 1.0, pw_im[1:2]
        den = lam_re * lam_re + lam_im * lam_im
        cf_re = (nr * lam_re + ni * lam_im) / den
        cf_im = (ni * lam_re - nr * lam_im) / den
        b_re, b_im = btr_ref[a], bti_ref[a]
        bb_re = cf_re * b_re - cf_im * b_im
        bb_im = cf_re * b_im + cf_im * b_re
        c_re, c_im = cr_ref[a], ci_ref[a]
        st[a]["bb"] = (bb_re, bb_im)
        st[a]["lc"] = outer(pw_re, pw_im, c_re, c_im)
        st[a]["in"] = outer(pi_re, pi_im, bb_re, bb_im)
        st[a]["ot"] = outer(po_re, po_im, c_re, c_im)

    def taps(a):
        bb_re, bb_im = st[a].pop("bb")
        lc_re, lc_im = st[a].pop("lc")
        once = lane < P
        k = (lax.dot_general(jnp.where(once, bb_re, 0.0), lc_re, nt, precision=_HI, preferred_element_type=_F32)
             - lax.dot_general(jnp.where(once, bb_im, 0.0), lc_im, nt, precision=_HI, preferred_element_type=_F32))
        kpad_ref[a, :, 0:GW] = jnp.zeros((S5_GROUP, GW), _F32)
        kpad_ref[a, :, GW:2 * GW] = k + jnp.where(lane256 == row256, d_ref[a], 0.0)

    def state_in(a):
        q, m = a // 2, a % 2
        mine = (lane_gw >= P) if m else (lane_gw < P)
        in_re, in_im = st[a].pop("in")
        mb_ref[q, m * GW:(m + 1) * GW, 0:LANES] = jnp.where(mine, in_re, 0.0).astype(_BF)
        mb_ref[q, m * GW:(m + 1) * GW, LANES:2 * LANES] = jnp.where(mine, in_im, 0.0).astype(_BF)

    def state_out(a):
        q, m = a // 2, a % 2
        ot_re, ot_im = st[a].pop("ot")
        o_re, o_im = ot_re.T[0:P], ot_im.T[0:P]
        blocks = [zeros64, zeros64, zeros64, zeros64]
        blocks[m], blocks[2 + m] = o_re, -o_im
        mc_ref[a] = jnp.concatenate(blocks, axis=0).astype(_BF)
        if m:
            first = lax.broadcasted_iota(jnp.int32, (1, LANES), 1) < P
            dre_ref[q] = jnp.where(first, st[a - 1]["p16"][0], st[a]["p16"][0])
            dim_ref[q] = jnp.where(first, st[a - 1]["p16"][1], st[a]["p16"][1])

    _wavefront([tables, taps, state_in, state_out], SLOTS)


def _s5_operators(A_re, A_im, log_dt, B_re, B_im, C_re, C_im, D):
    f32 = jnp.float32
    G, P, H = S5_GROUPS, S5_STATE, S5_GROUP
    lam_re = jnp.minimum(A_re.astype(f32), RE_CLIP)
    lam_im = A_im.astype(f32)
    dt = jnp.exp(log_dt.astype(f32))[:, None]

    twice = lambda t: jnp.concatenate([t, t], axis=-1)
    tables = [twice(lam_re), twice(lam_im), twice(lam_re * dt), twice(lam_im * dt)]
    mats = [twice(jnp.swapaxes(B_re.astype(f32), 1, 2)), twice(jnp.swapaxes(B_im.astype(f32), 1, 2)),
            twice(C_re.astype(f32)), twice(C_im.astype(f32))]
    tspec = pl.BlockSpec((SLOTS, LANES), lambda j: (j, 0))
    mspec = pl.BlockSpec((SLOTS, H, LANES), lambda j: (j, 0, 0))
    dspec = pl.BlockSpec((PAIRS_PER_J, 1, LANES), lambda j: (j, 0, 0))
    return pl.pallas_call(
        _s5_ops_kernel,
        grid=(NJ,),
        in_specs=[tspec] * 4 + [mspec] * 4 + [pl.BlockSpec((SLOTS, H, 1), lambda j: (j, 0, 0))],
        out_specs=[
            pl.BlockSpec((SLOTS, H, KPAD), lambda j: (j, 0, 0)),
            pl.BlockSpec((PAIRS_PER_J, 2 * GW, GW), lambda j: (j, 0, 0)),
            pl.BlockSpec((SLOTS, GW, GW), lambda j: (j, 0, 0)),
            dspec, dspec,
        ],
        out_shape=[
            jax.ShapeDtypeStruct((G, H, KPAD), f32),
            jax.ShapeDtypeStruct((G // 2, 2 * GW, GW), _BF),
            jax.ShapeDtypeStruct((G, GW, GW), _BF),
            jax.ShapeDtypeStruct((G // 2, 1, LANES), f32),
            jax.ShapeDtypeStruct((G // 2, 1, LANES), f32),
        ],
        compiler_params=pltpu.CompilerParams(dimension_semantics=("parallel",), vmem_limit_bytes=VMEM_LIMIT),
        name="s5_ops",
    )(*tables, *mats, D.astype(f32).reshape(G, H, 1))


IN_CT = 8
IN_MB = 1


def _s5_in_kernel(x_ref, g_ref, w_ref, u_ref, gate_ref):
    rows = BATCH * TCH
    nb = IN_CT // IN_MB
    st = [dict() for _ in range(nb)]

    def norm(i):
        parts = []
        for c in range(i * IN_MB, (i + 1) * IN_MB):
            hc = _rms(x_ref[:, c * TCH:(c + 1) * TCH, :], g_ref[...]).astype(_BF)
            parts.append(jnp.swapaxes(hc, 0, 1).reshape(rows, D_MODEL))
        st[i]["hp"] = jnp.concatenate(parts, axis=0)

    def gate_mm(i):
        gate = _dot(st[i]["hp"], w_ref[:, BRANCH:])
        gate_ref[i * IN_MB:(i + 1) * IN_MB] = (
            (gate * _sigmoid(gate)).astype(_BF).reshape(IN_MB, TCH, BATCH, BRANCH))

    def u_mm(i):
        u = _dot(st[i].pop("hp"), w_ref[:, :BRANCH])
        u_ref[i * IN_MB:(i + 1) * IN_MB] = u.astype(_BF).reshape(IN_MB, TCH, BATCH, BRANCH)

    _wavefront([norm, gate_mm, u_mm], nb)


def _s5_in(x, g, w_in):
    grid = (NCHUNK // IN_CT,)
    blk4 = (IN_CT, TCH, BATCH, BRANCH)
    return pl.pallas_call(
        _s5_in_kernel,
        grid=grid,
        in_specs=[
            pl.BlockSpec((BATCH, IN_CT * TCH, D_MODEL), lambda i: (0, i, 0)),
            pl.BlockSpec((1, D_MODEL), lambda i: (0, 0)),
            pl.BlockSpec((D_MODEL, 2 * BRANCH), lambda i: (0, 0)),
        ],
        out_specs=[
            pl.BlockSpec(blk4, lambda i: (i, 0, 0, 0)),
            pl.BlockSpec(blk4, lambda i: (i, 0, 0, 0)),
        ],
        out_shape=[
            jax.ShapeDtypeStruct((NCHUNK, TCH, BATCH, BRANCH), _BF),
            jax.ShapeDtypeStruct((NCHUNK, TCH, BATCH, BRANCH), _BF),
        ],
        compiler_params=pltpu.CompilerParams(
            dimension_semantics=("parallel",), vmem_limit_bytes=VMEM_LIMIT),
        name="s5_in",
    )(x, g, w_in)


MIX_CT = 64
MIX_R = MIX_CT * BATCH
MIX_CB = 4
MIX_RT = MIX_CB * BATCH
MIX_NH = 4
MIX_HC = MIX_CT // MIX_NH
MIX_HR = MIX_HC * BATCH


def _gather_slots(v, slot):
    rolled = [v[0]] + [pltpu.roll(v[k], k * S5_GROUP, axis=1) for k in range(1, SLOTS)]
    out = []
    for a in range(SLOTS):
        acc = rolled[0]
        for k in range(1, SLOTS):
            acc = jnp.where(slot == (a + k) % SLOTS, rolled[k], acc)
        out.append(acc)
    return out


def _scatter_slots(v, slot):
    out = []
    for k in range(SLOTS):
        acc = v[0]
        for a in range(1, SLOTS):
            acc = jnp.where(slot == (a + k) % SLOTS, v[a], acc)
        out.append(acc if k == 0 else pltpu.roll(acc, (SLOTS - k) * S5_GROUP, axis=1))
    return out


def _column_perms():
    p = np.zeros((SLOTS, GW, GW), np.float32)
    for a in range(SLOTS):
        for x in range(TCH):
            f = int(_FRAME_OF_SLOT[a, x])
            for c in range(S5_GROUP):
                p[a, f * S5_GROUP + c, x * S5_GROUP + c] = 1.0
    return p


_CPERM = _column_perms()


def _s5_mix_kernel(u_ref, kpad_ref, mb_ref, mc_ref, cperm_ref, dre_ref, dim_ref, y_ref,
                   op_ref, xs_ref, ys_ref, ss_ref, hs_ref, carry_ref):
    ct = pl.program_id(1)

    @pl.when(ct == 0)
    def _():
        carry_ref[...] = jnp.zeros_like(carry_ref)
        for a in range(SLOTS):
            kp = kpad_ref[a]
            rows = []
            for x in range(TCH):
                off = (TCH - int(_FRAME_OF_SLOT[a, x])) * S5_GROUP
                rows.append(kp[:, off:off + GW])
            toep = jnp.concatenate(rows, axis=0).astype(_BF)
            op_ref[a, 0:GW, :] = _dot(toep, cperm_ref[a]).astype(_BF)
            op_ref[a, GW:2 * GW, :] = mc_ref[a]

    slot = lax.broadcasted_iota(jnp.int32, (MIX_RT, LANES), 1) // S5_GROUP
    dre = [dre_ref[q] for q in range(PAIRS_PER_J)]
    dim_ = [dim_ref[q] for q in range(PAIRS_PER_J)]

    def relayout_in(c0):
        r0 = c0 * BATCH
        for half in range(2):
            pieces = [u_ref[c0:c0 + MIX_CB, SLOTS * half + k, :, :].reshape(MIX_RT, LANES) for k in range(SLOTS)]
            groups = _gather_slots(pieces, slot)
            for a in range(SLOTS):
                lo = (a % 2) * GW + half * LANES
                xs_ref[a // 2, r0:r0 + MIX_RT, lo:lo + LANES] = groups[a]

    def relayout_out(c0):
        r0 = c0 * BATCH
        for half in range(2):
            groups = [ys_ref[a, r0:r0 + MIX_RT, half * LANES:(half + 1) * LANES].astype(_BF) for a in range(SLOTS)]
            frames = _scatter_slots(groups, slot)
            for k in range(SLOTS):
                y_ref[c0:c0 + MIX_CB, SLOTS * half + k, :, :] = frames[k].reshape(MIX_CB, BATCH, LANES)

    def recurrence(c0, state):
        for c in range(c0, c0 + MIX_HC):
            r0 = c * BATCH
            new = []
            for q in range(PAIRS_PER_J):
                hre, him = state[2 * q], state[2 * q + 1]
                hs_ref[q, r0:r0 + BATCH, 0:LANES] = hre.astype(_BF)
                hs_ref[q, r0:r0 + BATCH, LANES:2 * LANES] = him.astype(_BF)
                sre = ss_ref[q, r0:r0 + BATCH, 0:LANES]
                sim = ss_ref[q, r0:r0 + BATCH, LANES:2 * LANES]
                new.append(dre[q] * hre - dim_[q] * him + sre)
                new.append(dre[q] * him + dim_[q] * hre + sim)
            state = new
        return state

    blocks = [(h * MIX_HC, h * MIX_HC * BATCH) for h in range(MIX_NH)]
    for c0, r0 in blocks:
        for t in range(MIX_HC // MIX_CB):
            relayout_in(c0 + t * MIX_CB)
        for q in range(PAIRS_PER_J):
            ss_ref[q, r0:r0 + MIX_HR, :] = _dot(xs_ref[q, r0:r0 + MIX_HR, :], mb_ref[q])
    state = [carry_ref[q, k] for q in range(PAIRS_PER_J) for k in range(2)]
    for c0, r0 in blocks:
        state = recurrence(c0, state)
    for q in range(PAIRS_PER_J):
        carry_ref[q, 0] = state[2 * q]
        carry_ref[q, 1] = state[2 * q + 1]
    for c0, r0 in blocks:
        for a in range(SLOTS):
            q, m = a // 2, a % 2
            lhs = jnp.concatenate([xs_ref[q, r0:r0 + MIX_HR, m * GW:(m + 1) * GW], hs_ref[q, r0:r0 + MIX_HR, :]], axis=1)
            ys_ref[a, r0:r0 + MIX_HR, :] = _dot(lhs, op_ref[a])
    for c0, r0 in blocks:
        for t in range(MIX_HC // MIX_CB):
            relayout_out(c0 + t * MIX_CB)


def _s5_mix(u, kpad, mb, mc, dec_re, dec_im):
    grid = (NJ, NCHUNK // MIX_CT)
    blk = (MIX_CT, TCH, BATCH, LANES)
    dspec = pl.BlockSpec((PAIRS_PER_J, 1, LANES), lambda j, c: (j, 0, 0))
    return pl.pallas_call(
        _s5_mix_kernel,
        grid=grid,
        in_specs=[
            pl.BlockSpec(blk, lambda j, c: (c, 0, 0, j)),
            pl.BlockSpec((SLOTS, S5_GROUP, KPAD), lambda j, c: (j, 0, 0)),
            pl.BlockSpec((PAIRS_PER_J, 2 * GW, GW), lambda j, c: (j, 0, 0)),
            pl.BlockSpec((SLOTS, GW, GW), lambda j, c: (j, 0, 0)),
            pl.BlockSpec((SLOTS, GW, GW), lambda j, c: (0, 0, 0)),
            dspec, dspec,
        ],
        out_specs=pl.BlockSpec(blk, lambda j, c: (c, 0, 0, j)),
        out_shape=jax.ShapeDtypeStruct((NCHUNK, TCH, BATCH, BRANCH), _BF),
        scratch_shapes=[
            pltpu.VMEM((SLOTS, 2 * GW, GW), _BF),
            pltpu.VMEM((PAIRS_PER_J, MIX_R, 2 * GW), _BF),
            pltpu.VMEM((SLOTS, MIX_R, GW), _F32),
            pltpu.VMEM((PAIRS_PER_J, MIX_R, GW), _F32),
            pltpu.VMEM((PAIRS_PER_J, MIX_R, GW), _BF),
            pltpu.VMEM((PAIRS_PER_J, 2, BATCH, LANES), _F32),
        ],
        compiler_params=pltpu.CompilerParams(
            dimension_semantics=("parallel", "arbitrary"), vmem_limit_bytes=VMEM_LIMIT),
        name="s5_mix",
    )(u, kpad, mb, mc, jnp.asarray(_CPERM, _BF), dec_re, dec_im)


def _s5_out_kernel(y_ref, gate_ref, wglu_ref, bglu_ref, wout_ref, o_ref):
    rows = IN_MB * BATCH * TCH
    nb = IN_CT // IN_MB
    st = [dict() for _ in range(nb)]

    def act(i):
        st[i]["y"] = _gelu(y_ref[i * IN_MB:(i + 1) * IN_MB].reshape(rows, BRANCH).astype(_F32))

    def glu_mm(i):
        st[i]["glu"] = _dot(st[i]["y"].astype(_BF), wglu_ref[...])

    def half_glu(i):
        st[i]["y"] = st[i].pop("y") * _sigmoid(st[i].pop("glu") + bglu_ref[...])

    def gating(i):
        y = st[i].pop("y")
        sg = gate_ref[i * IN_MB:(i + 1) * IN_MB].reshape(rows, BRANCH).astype(_F32)
        z = (y * sg).astype(_BF).reshape(IN_MB, TCH, BATCH, BRANCH)
        parts = [jnp.swapaxes(z[k], 0, 1).reshape(BATCH * TCH, BRANCH) for k in range(IN_MB)]
        st[i]["z"] = parts[0] if IN_MB == 1 else jnp.concatenate(parts, axis=0)

    def out_mm(i):
        st[i]["out"] = _dot(st[i].pop("z"), wout_ref[...])

    def residual(i):
        out = st[i].pop("out").reshape(IN_MB, BATCH, TCH, D_MODEL)
        for k in range(IN_MB):
            c = i * IN_MB + k
            o_ref[:, c * TCH:(c + 1) * TCH, :] = out[k]

    _wavefront([act, glu_mm, half_glu, gating, out_mm, residual], nb)


def _s5_out(y, gate, w_glu, b_glu, w_out):
    grid = (NCHUNK // IN_CT,)
    blk4 = (IN_CT, TCH, BATCH, BRANCH)
    xblk = pl.BlockSpec((BATCH, IN_CT * TCH, D_MODEL), lambda i: (0, i, 0))
    return pl.pallas_call(
        _s5_out_kernel,
        grid=grid,
        in_specs=[
            pl.BlockSpec(blk4, lambda i: (i, 0, 0, 0)),
            pl.BlockSpec(blk4, lambda i: (i, 0, 0, 0)),
            pl.BlockSpec((BRANCH, BRANCH), lambda i: (0, 0)),
            pl.BlockSpec((1, BRANCH), lambda i: (0, 0)),
            pl.BlockSpec((BRANCH, D_MODEL), lambda i: (0, 0)),
        ],
        out_specs=xblk,
        out_shape=jax.ShapeDtypeStruct((BATCH, SEQ, D_MODEL), _F32),
        compiler_params=pltpu.CompilerParams(
            dimension_semantics=("parallel",), vmem_limit_bytes=VMEM_LIMIT),
        name="s5_out",
    )(y, gate, w_glu, b_glu, w_out)


SGU_TB = 1024
SGU_RB = 512
SGU_BPB = SGU_RB // SGU_BLOCK


def _sgu_kernel(x_ref, d_ref, g_ref, win_ref, lng_ref, lnb_ref, ws_ref, bs_ref, wout_ref, fg_ref, o_ref, mix_ref):
    blocks = [slice(r0, r0 + SGU_RB) for r0 in range(0, SGU_TB, SGU_RB)]
    st = [dict() for _ in blocks]

    def norm_in(i):
        st[i]["h"] = _rms(x_ref[blocks[i], :] + d_ref[blocks[i], :], g_ref[...]).astype(_BF)

    def proj_v(i):
        st[i]["v"] = _dot(st[i]["h"], win_ref[:, BRANCH:2 * BRANCH])

    def proj_ug(i):
        st[i]["u"] = _dot(st[i]["h"], win_ref[:, :BRANCH])
        st[i]["gate"] = _dot(st[i]["h"], win_ref[:, 2 * BRANCH:])

    def layer_norm(i):
        v = _gelu(st[i].pop("v"))
        mu = jnp.mean(v, axis=-1, keepdims=True)
        vc = v - mu
        var = jnp.mean(vc * vc, axis=-1, keepdims=True)
        st[i]["vn"] = (vc * lax.rsqrt(var + LN_EPS) * lng_ref[...] + lnb_ref[...]).astype(_BF)

    def spatial(i):
        vn = st[i].pop("vn")
        r0 = blocks[i].start
        for hd in range(SGU_HEADS):
            lo = hd * SGU_HEAD_DIM
            rhs = jnp.concatenate(
                [vn[b * SGU_BLOCK:(b + 1) * SGU_BLOCK, lo:lo + SGU_HEAD_DIM] for b in range(SGU_BPB)], axis=1)
            res = _dot(ws_ref[hd], rhs)
            for b in range(SGU_BPB):
                mix_ref[r0 + b * SGU_BLOCK:r0 + (b + 1) * SGU_BLOCK, lo:lo + SGU_HEAD_DIM] = (
                    res[:, b * SGU_HEAD_DIM:(b + 1) * SGU_HEAD_DIM] + bs_ref[:, lo:lo + SGU_HEAD_DIM])

    def activate(i):
        st[i]["ug"] = _gelu(st[i].pop("u")) * _silu_tanh(st[i].pop("gate"))

    def gating(i):
        st[i]["z"] = (st[i].pop("ug") * mix_ref[blocks[i], :]).astype(_BF)

    def proj_out(i):
        st[i]["out"] = _dot(st[i].pop("z"), wout_ref[...])

    def norm_out(i):
        x2 = (x_ref[blocks[i], :] + d_ref[blocks[i], :]) + st[i].pop("out")
        o_ref[blocks[i], :] = _rms(x2, fg_ref[...])

    _wavefront([norm_in, proj_v, proj_ug, layer_norm, activate, spatial, gating, proj_out, norm_out], len(blocks))


def _sgu_layer(x, delta, g, w_in, ln_g, ln_b, ws, bs, w_out, final_g):
    n = BATCH * SEQ
    grid = (n // SGU_TB,)
    row = lambda i: (0, 0)
    xblk = pl.BlockSpec((SGU_TB, D_MODEL), lambda i: (i, 0))
    return pl.pallas_call(
        _sgu_kernel,
        grid=grid,
        in_specs=[
            xblk,
            xblk,
            pl.BlockSpec((1, D_MODEL), row),
            pl.BlockSpec((D_MODEL, 3 * BRANCH), row),
            pl.BlockSpec((1, BRANCH), row),
            pl.BlockSpec((1, BRANCH), row),
            pl.BlockSpec((SGU_HEADS, SGU_BLOCK, SGU_BLOCK), lambda i: (0, 0, 0)),
            pl.BlockSpec((SGU_BLOCK, BRANCH), row),
            pl.BlockSpec((BRANCH, D_MODEL), row),
            pl.BlockSpec((1, D_MODEL), row),
        ],
        out_specs=xblk,
        out_shape=jax.ShapeDtypeStruct((n, D_MODEL), _F32),
        scratch_shapes=[pltpu.VMEM((SGU_TB, BRANCH), _F32)],
        compiler_params=pltpu.CompilerParams(
            dimension_semantics=("parallel",), vmem_limit_bytes=VMEM_LIMIT),
        name="sgu_layer",
    )(x, delta, g, w_in, ln_g, ln_b, ws, bs, w_out, final_g)


def kernel(x, norm_g, final_g, s5_w_in, s5_A_re, s5_A_im, s5_log_dt, s5_B_re, s5_B_im, s5_C_re, s5_C_im, s5_D, s5_w_glu, s5_b_glu, s5_w_out, sgu_w_in, sgu_ln_g, sgu_ln_b, sgu_w_s, sgu_b_s, sgu_w_out):
    kpad, mb, mc, dec_re, dec_im = _s5_operators(
        s5_A_re[0], s5_A_im[0], s5_log_dt[0], s5_B_re[0], s5_B_im[0], s5_C_re[0], s5_C_im[0], s5_D[0])
    u, gate = _s5_in(x, norm_g[0][None, :], s5_w_in[0].astype(_BF))
    y = _s5_mix(u, kpad, mb, mc, dec_re, dec_im)
    delta = _s5_out(y, gate, s5_w_glu[0].astype(_BF), s5_b_glu[0][None, :], s5_w_out[0].astype(_BF))

    mask = jnp.tril(jnp.ones((SGU_BLOCK, SGU_BLOCK), dtype=bool))
    ws = jnp.where(mask[None], sgu_w_s[0], 0.0).astype(_BF)
    bs = jnp.repeat(jnp.transpose(sgu_b_s[0]), SGU_HEAD_DIM, axis=1)
    out = _sgu_layer(x.reshape(BATCH * SEQ, D_MODEL), delta.reshape(BATCH * SEQ, D_MODEL),
                     norm_g[1][None, :], sgu_w_in[0].astype(_BF),
                     sgu_ln_g[0][None, :], sgu_ln_b[0][None, :], ws, bs,
                     sgu_w_out[0].astype(_BF), final_g[None, :])
    return out.reshape(BATCH, SEQ, D_MODEL)
```

```python
import math

import numpy as np
import jax
import jax.numpy as jnp
from jax import lax
from jax.experimental import pallas as pl
from jax.experimental.pallas import tpu as pltpu

D_MODEL = 1024
BATCH = 16
SEQ = 2048
BRANCH = D_MODEL
S5_GROUP = 16
S5_GROUPS = BRANCH // S5_GROUP
S5_STATE = 64
SGU_BLOCK = 128
SGU_HEADS = 8
SGU_HEAD_DIM = BRANCH // SGU_HEADS
RMS_EPS = 1e-6
LN_EPS = 1e-5
RE_CLIP = -1e-4

LANES = 128
TCH = 16
NCHUNK = SEQ // TCH
SLOTS = LANES // S5_GROUP
NJ = BRANCH // LANES
GW = TCH * S5_GROUP
PAIRS_PER_J = SLOTS // 2
VMEM_LIMIT = 56 * 1024 * 1024

_HI = lax.Precision.HIGHEST
_BF = jnp.bfloat16
_F32 = jnp.float32


_GELU_C1 = math.sqrt(2.0 / math.pi)
_GELU_C2 = _GELU_C1 * 0.044715


def _gelu(x):
    inner = x * (_GELU_C1 + _GELU_C2 * (x * x))
    return (0.5 * x) * (1.0 + jnp.tanh(inner))


def _sigmoid(x):
    return jax.nn.sigmoid(x)


def _silu_tanh(x):
    h = 0.5 * x
    return h + h * jnp.tanh(h)


def _dot(a, b):
    return jnp.dot(a, b, preferred_element_type=_F32)


def _wavefront(stages, n):
    for t in range(len(stages) + n - 1):
        for i in range(n):
            k = t - i
            if 0 <= k < len(stages):
                stages[k](i)


def _rms(x, g):
    return x * lax.rsqrt(jnp.mean(x * x, axis=-1, keepdims=True) + RMS_EPS) * g


KPAD = 2 * GW


def _slot_frame_table():
    a = np.arange(SLOTS)[:, None]
    x = np.arange(TCH)[None, :]
    half, k = x // SLOTS, x % SLOTS
    return SLOTS * half + (k - a) % SLOTS


_FRAME_OF_SLOT = _slot_frame_table()


def _s5_ops_kernel(lre_ref, lim_ref, ldr_ref, ldi_ref, btr_ref, bti_ref, cr_ref, ci_ref, d_ref,
                   kpad_ref, mb_ref, mc_ref, dre_ref, dim_ref):
    P = S5_STATE
    lane = lax.broadcasted_iota(jnp.int32, (S5_GROUP, LANES), 1)
    kcol = lax.broadcasted_iota(jnp.int32, (TCH, 1), 0)
    lane256 = lax.broadcasted_iota(jnp.int32, (S5_GROUP, GW), 1)
    row256 = lax.broadcasted_iota(jnp.int32, (S5_GROUP, GW), 0)
    lane_gw = lax.broadcasted_iota(jnp.int32, (GW, LANES), 1)
    nt = (((1,), (1,)), ((), ()))
    zeros64 = jnp.zeros((P, GW), _F32)
    st = [dict() for _ in range(SLOTS)]

    def outer(t_re, t_im, m_re, m_im):
        re = [t_re[k:k + 1] * m_re - t_im[k:k + 1] * m_im for k in range(TCH)]
        im = [t_re[k:k + 1] * m_im + t_im[k:k + 1] * m_re for k in range(TCH)]
        return jnp.concatenate(re, axis=0), jnp.concatenate(im, axis=0)

    def tables(a):
        lam_re, lam_im = lre_ref[a:a + 1, :], lim_ref[a:a + 1, :]
        ldr, ldi = ldr_ref[a:a + 1, :], ldi_ref[a:a + 1, :]

        def power(k):
            mag = jnp.exp(k * ldr)
            return mag * jnp.cos(k * ldi), mag * jnp.sin(k * ldi)

        frame = (kcol & SLOTS) + (((kcol & (SLOTS - 1)) + (SLOTS - a)) & (SLOTS - 1))
        pw_re, pw_im = power(kcol.astype(_F32))
        pi_re, pi_im = power((TCH - 1 - frame).astype(_F32))
        po_re, po_im = power((frame + 1).astype(_F32))
        st[a]["p16"] = power(jnp.full((1, 1), float(TCH), _F32))

        nr, ni = pw_re[1:2] - 1.0, pw_im[1:2]
        den = lam_re * lam_re + lam_im * lam_im
        cf_re = (nr * lam_re + ni * lam_im) / den
        cf_im = (ni * lam_re - nr * lam_im) / den
        b_re, b_im = btr_ref[a], bti_ref[a]
        bb_re = cf_re * b_re - cf_im * b_im
        bb_im = cf_re * b_im + cf_im * b_re
        c_re, c_im = cr_ref[a], ci_ref[a]
        st[a]["bb"] = (bb_re, bb_im)
        st[a]["lc"] = outer(pw_re, pw_im, c_re, c_im)
        st[a]["in"] = outer(pi_re, pi_im, bb_re, bb_im)
        st[a]["ot"] = outer(po_re, po_im, c_re, c_im)

    def taps(a):
        bb_re, bb_im = st[a].pop("bb")
        lc_re, lc_im = st[a].pop("lc")
        once = lane < P
        k = (lax.dot_general(jnp.where(once, bb_re, 0.0), lc_re, nt, precision=_HI, preferred_element_type=_F32)
             - lax.dot_general(jnp.where(once, bb_im, 0.0), lc_im, nt, precision=_HI, preferred_element_type=_F32))
        kpad_ref[a, :, 0:GW] = jnp.zeros((S5_GROUP, GW), _F32)
        kpad_ref[a, :, GW:2 * GW] = k + jnp.where(lane256 == row256, d_ref[a], 0.0)

    def state_in(a):
        q, m = a // 2, a % 2
        mine = (lane_gw >= P) if m else (lane_gw < P)
        in_re, in_im = st[a].pop("in")
        mb_ref[q, m * GW:(m + 1) * GW, 0:LANES] = jnp.where(mine, in_re, 0.0).astype(_BF)
        mb_ref[q, m * GW:(m + 1) * GW, LANES:2 * LANES] = jnp.where(mine, in_im, 0.0).astype(_BF)

    def state_out(a):
        q, m = a // 2, a % 2
        ot_re, ot_im = st[a].pop("ot")
        o_re, o_im = ot_re.T[0:P], ot_im.T[0:P]
        blocks = [zeros64, zeros64, zeros64, zeros64]
        blocks[m], blocks[2 + m] = o_re, -o_im
        mc_ref[a] = jnp.concatenate(blocks, axis=0).astype(_BF)
        if m:
            first = lax.broadcasted_iota(jnp.int32, (1, LANES), 1) < P
            dre_ref[q] = jnp.where(first, st[a - 1]["p16"][0], st[a]["p16"][0])
            dim_ref[q] = jnp.where(first, st[a - 1]["p16"][1], st[a]["p16"][1])

    _wavefront([tables, taps, state_in, state_out], SLOTS)


def _s5_operators(A_re, A_im, log_dt, B_re, B_im, C_re, C_im, D):
    f32 = jnp.float32
    G, P, H = S5_GROUPS, S5_STATE, S5_GROUP
    lam_re = jnp.minimum(A_re.astype(f32), RE_CLIP)
    lam_im = A_im.astype(f32)
    dt = jnp.exp(log_dt.astype(f32))[:, None]

    twice = lambda t: jnp.concatenate([t, t], axis=-1)
    tables = [twice(lam_re), twice(lam_im), twice(lam_re * dt), twice(lam_im * dt)]
    mats = [twice(jnp.swapaxes(B_re.astype(f32), 1, 2)), twice(jnp.swapaxes(B_im.astype(f32), 1, 2)),
            twice(C_re.astype(f32)), twice(C_im.astype(f32))]
    tspec = pl.BlockSpec((SLOTS, LANES), lambda j: (j, 0))
    mspec = pl.BlockSpec((SLOTS, H, LANES), lambda j: (j, 0, 0))
    dspec = pl.BlockSpec((PAIRS_PER_J, 1, LANES), lambda j: (j, 0, 0))
    return pl.pallas_call(
        _s5_ops_kernel,
        grid=(NJ,),
        in_specs=[tspec] * 4 + [mspec] * 4 + [pl.BlockSpec((SLOTS, H, 1), lambda j: (j, 0, 0))],
        out_specs=[
            pl.BlockSpec((SLOTS, H, KPAD), lambda j: (j, 0, 0)),
            pl.BlockSpec((PAIRS_PER_J, 2 * GW, GW), lambda j: (j, 0, 0)),
            pl.BlockSpec((SLOTS, GW, GW), lambda j: (j, 0, 0)),
            dspec, dspec,
        ],
        out_shape=[
            jax.ShapeDtypeStruct((G, H, KPAD), f32),
            jax.ShapeDtypeStruct((G // 2, 2 * GW, GW), _BF),
            jax.ShapeDtypeStruct((G, GW, GW), _BF),
            jax.ShapeDtypeStruct((G // 2, 1, LANES), f32),
            jax.ShapeDtypeStruct((G // 2, 1, LANES), f32),
        ],
        compiler_params=pltpu.CompilerParams(dimension_semantics=("parallel",), vmem_limit_bytes=VMEM_LIMIT),
        name="s5_ops",
    )(*tables, *mats, D.astype(f32).reshape(G, H, 1))


IN_CT = 8
IN_MB = 1


def _s5_in_kernel(x_ref, g_ref, w_ref, u_ref, gate_ref):
    rows = BATCH * TCH
    nb = IN_CT // IN_MB
    st = [dict() for _ in range(nb)]

    def norm(i):
        parts = []
        for c in range(i * IN_MB, (i + 1) * IN_MB):
            hc = _rms(x_ref[:, c * TCH:(c + 1) * TCH, :], g_ref[...]).astype(_BF)
            parts.append(jnp.swapaxes(hc, 0, 1).reshape(rows, D_MODEL))
        st[i]["hp"] = jnp.concatenate(parts, axis=0)

    def gate_mm(i):
        gate = _dot(st[i]["hp"], w_ref[:, BRANCH:])
        gate_ref[i * IN_MB:(i + 1) * IN_MB] = (
            (gate * _sigmoid(gate)).astype(_BF).reshape(IN_MB, TCH, BATCH, BRANCH))

    def u_mm(i):
        u = _dot(st[i].pop("hp"), w_ref[:, :BRANCH])
        u_ref[i * IN_MB:(i + 1) * IN_MB] = u.astype(_BF).reshape(IN_MB, TCH, BATCH, BRANCH)

    _wavefront([norm, gate_mm, u_mm], nb)


def _s5_in(x, g, w_in):
    grid = (NCHUNK // IN_CT,)
    blk4 = (IN_CT, TCH, BATCH, BRANCH)
    return pl.pallas_call(
        _s5_in_kernel,
        grid=grid,
        in_specs=[
            pl.BlockSpec((BATCH, IN_CT * TCH, D_MODEL), lambda i: (0, i, 0)),
            pl.BlockSpec((1, D_MODEL), lambda i: (0, 0)),
            pl.BlockSpec((D_MODEL, 2 * BRANCH), lambda i: (0, 0)),
        ],
        out_specs=[
            pl.BlockSpec(blk4, lambda i: (i, 0, 0, 0)),
            pl.BlockSpec(blk4, lambda i: (i, 0, 0, 0)),
        ],
        out_shape=[
            jax.ShapeDtypeStruct((NCHUNK, TCH, BATCH, BRANCH), _BF),
            jax.ShapeDtypeStruct((NCHUNK, TCH, BATCH, BRANCH), _BF),
        ],
        compiler_params=pltpu.CompilerParams(
            dimension_semantics=("parallel",), vmem_limit_bytes=VMEM_LIMIT),
        name="s5_in",
    )(x, g, w_in)


MIX_CT = 64
MIX_R = MIX_CT * BATCH
MIX_CB = 4
MIX_RT = MIX_CB * BATCH
MIX_NH = 4
MIX_HC = MIX_CT // MIX_NH
MIX_HR = MIX_HC * BATCH


def _gather_slots(v, slot):
    rolled = [v[0]] + [pltpu.roll(v[k], k * S5_GROUP, axis=1) for k in range(1, SLOTS)]
    out = []
    for a in range(SLOTS):
        acc = rolled[0]
        for k in range(1, SLOTS):
            acc = jnp.where(slot == (a + k) % SLOTS, rolled[k], acc)
        out.append(acc)
    return out


def _scatter_slots(v, slot):
    out = []
    for k in range(SLOTS):
        acc = v[0]
        for a in range(1, SLOTS):
            acc = jnp.where(slot == (a + k) % SLOTS, v[a], acc)
        out.append(acc if k == 0 else pltpu.roll(acc, (SLOTS - k) * S5_GROUP, axis=1))
    return out


def _column_perms():
    p = np.zeros((SLOTS, GW, GW), np.float32)
    for a in range(SLOTS):
        for x in range(TCH):
            f = int(_FRAME_OF_SLOT[a, x])
            for c in range(S5_GROUP):
                p[a, f * S5_GROUP + c, x * S5_GROUP + c] = 1.0
    return p


_CPERM = _column_perms()


def _s5_mix_kernel(u_ref, kpad_ref, mb_ref, mc_ref, cperm_ref, dre_ref, dim_ref, y_ref,
                   op_ref, xs_ref, ys_ref, ss_ref, hs_ref, carry_ref):
    ct = pl.program_id(1)

    @pl.when(ct == 0)
    def _():
        carry_ref[...] = jnp.zeros_like(carry_ref)
        for a in range(SLOTS):
            kp = kpad_ref[a]
            rows = []
            for x in range(TCH):
                off = (TCH - int(_FRAME_OF_SLOT[a, x])) * S5_GROUP
                rows.append(kp[:, off:off + GW])
            toep = jnp.concatenate(rows, axis=0).astype(_BF)
            op_ref[a, 0:GW, :] = _dot(toep, cperm_ref[a]).astype(_BF)
            op_ref[a, GW:2 * GW, :] = mc_ref[a]

    slot = lax.broadcasted_iota(jnp.int32, (MIX_RT, LANES), 1) // S5_GROUP
    dre = [dre_ref[q] for q in range(PAIRS_PER_J)]
    dim_ = [dim_ref[q] for q in range(PAIRS_PER_J)]

    def relayout_in(c0):
        r0 = c0 * BATCH
        for half in range(2):
            pieces = [u_ref[c0:c0 + MIX_CB, SLOTS * half + k, :, :].reshape(MIX_RT, LANES) for k in range(SLOTS)]
            groups = _gather_slots(pieces, slot)
            for a in range(SLOTS):
                lo = (a % 2) * GW + half * LANES
                xs_ref[a // 2, r0:r0 + MIX_RT, lo:lo + LANES] = groups[a]

    def relayout_out(c0):
        r0 = c0 * BATCH
        for half in range(2):
            groups = [ys_ref[a, r0:r0 + MIX_RT, half * LANES:(half + 1) * LANES].astype(_BF) for a in range(SLOTS)]
            frames = _scatter_slots(groups, slot)
            for k in range(SLOTS):
                y_ref[c0:c0 + MIX_CB, SLOTS * half + k, :, :] = frames[k].reshape(MIX_CB, BATCH, LANES)

    def recurrence(c0, state):
        for c in range(c0, c0 + MIX_HC):
            r0 = c * BATCH
            new = []
            for q in range(PAIRS_PER_J):
                hre, him = state[2 * q], state[2 * q + 1]
                hs_ref[q, r0:r0 + BATCH, 0:LANES] = hre.astype(_BF)
                hs_ref[q, r0:r0 + BATCH, LANES:2 * LANES] = him.astype(_BF)
                sre = ss_ref[q, r0:r0 + BATCH, 0:LANES]
                sim = ss_ref[q, r0:r0 + BATCH, LANES:2 * LANES]
                new.append(dre[q] * hre - dim_[q] * him + sre)
                new.append(dre[q] * him + dim_[q] * hre + sim)
            state = new
        return state

    blocks = [(h * MIX_HC, h * MIX_HC * BATCH) for h in range(MIX_NH)]
    for c0, r0 in blocks:
        for t in range(MIX_HC // MIX_CB):
            relayout_in(c0 + t * MIX_CB)
        for q in range(PAIRS_PER_J):
            ss_ref[q, r0:r0 + MIX_HR, :] = _dot(xs_ref[q, r0:r0 + MIX_HR, :], mb_ref[q])
    state = [carry_ref[q, k] for q in range(PAIRS_PER_J) for k in range(2)]
    for c0, r0 in blocks:
        state = recurrence(c0, state)
    for q in range(PAIRS_PER_J):
        carry_ref[q, 0] = state[2 * q]
        carry_ref[q, 1] = state[2 * q + 1]
    for c0, r0 in blocks:
        for a in range(SLOTS):
            q, m = a // 2, a % 2
            lhs = jnp.concatenate([xs_ref[q, r0:r0 + MIX_HR, m * GW:(m + 1) * GW], hs_ref[q, r0:r0 + MIX_HR, :]], axis=1)
            ys_ref[a, r0:r0 + MIX_HR, :] = _dot(lhs, op_ref[a])
    for c0, r0 in blocks:
        for t in range(MIX_HC // MIX_CB):
            relayout_out(c0 + t * MIX_CB)


def _s5_mix(u, kpad, mb, mc, dec_re, dec_im):
    grid = (NJ, NCHUNK // MIX_CT)
    blk = (MIX_CT, TCH, BATCH, LANES)
    dspec = pl.BlockSpec((PAIRS_PER_J, 1, LANES), lambda j, c: (j, 0, 0))
    return pl.pallas_call(
        _s5_mix_kernel,
        grid=grid,
        in_specs=[
            pl.BlockSpec(blk, lambda j, c: (c, 0, 0, j)),
            pl.BlockSpec((SLOTS, S5_GROUP, KPAD), lambda j, c: (j, 0, 0)),
            pl.BlockSpec((PAIRS_PER_J, 2 * GW, GW), lambda j, c: (j, 0, 0)),
            pl.BlockSpec((SLOTS, GW, GW), lambda j, c: (j, 0, 0)),
            pl.BlockSpec((SLOTS, GW, GW), lambda j, c: (0, 0, 0)),
            dspec, dspec,
        ],
        out_specs=pl.BlockSpec(blk, lambda j, c: (c, 0, 0, j)),
        out_shape=jax.ShapeDtypeStruct((NCHUNK, TCH, BATCH, BRANCH), _BF),
        scratch_shapes=[
            pltpu.VMEM((SLOTS, 2 * GW, GW), _BF),
            pltpu.VMEM((PAIRS_PER_J, MIX_R, 2 * GW), _BF),
            pltpu.VMEM((SLOTS, MIX_R, GW), _F32),
            pltpu.VMEM((PAIRS_PER_J, MIX_R, GW), _F32),
            pltpu.VMEM((PAIRS_PER_J, MIX_R, GW), _BF),
            pltpu.VMEM((PAIRS_PER_J, 2, BATCH, LANES), _F32),
        ],
        compiler_params=pltpu.CompilerParams(
            dimension_semantics=("parallel", "arbitrary"), vmem_limit_bytes=VMEM_LIMIT),
        name="s5_mix",
    )(u, kpad, mb, mc, jnp.asarray(_CPERM, _BF), dec_re, dec_im)


def _s5_out_kernel(y_ref, gate_ref, wglu_ref, bglu_ref, wout_ref, o_ref):
    rows = IN_MB * BATCH * TCH
    nb = IN_CT // IN_MB
    st = [dict() for _ in range(nb)]

    def act(i):
        st[i]["y"] = _gelu(y_ref[i * IN_MB:(i + 1) * IN_MB].reshape(rows, BRANCH).astype(_F32))

    def glu_mm(i):
        st[i]["glu"] = _dot(st[i]["y"].astype(_BF), wglu_ref[...])

    def half_glu(i):
        st[i]["y"] = st[i].pop("y") * _sigmoid(st[i].pop("glu") + bglu_ref[...])

    def gating(i):
        y = st[i].pop("y")
        sg = gate_ref[i * IN_MB:(i + 1) * IN_MB].reshape(rows, BRANCH).astype(_F32)
        z = (y * sg).astype(_BF).reshape(IN_MB, TCH, BATCH, BRANCH)
        parts = [jnp.swapaxes(z[k], 0, 1).reshape(BATCH * TCH, BRANCH) for k in range(IN_MB)]
        st[i]["z"] = parts[0] if IN_MB == 1 else jnp.concatenate(parts, axis=0)

    def out_mm(i):
        st[i]["out"] = _dot(st[i].pop("z"), wout_ref[...])

    def residual(i):
        out = st[i].pop("out").reshape(IN_MB, BATCH, TCH, D_MODEL)
        for k in range(IN_MB):
            c = i * IN_MB + k
            o_ref[:, c * TCH:(c + 1) * TCH, :] = out[k]

    _wavefront([act, glu_mm, half_glu, gating, out_mm, residual], nb)


def _s5_out(y, gate, w_glu, b_glu, w_out):
    grid = (NCHUNK // IN_CT,)
    blk4 = (IN_CT, TCH, BATCH, BRANCH)
    xblk = pl.BlockSpec((BATCH, IN_CT * TCH, D_MODEL), lambda i: (0, i, 0))
    return pl.pallas_call(
        _s5_out_kernel,
        grid=grid,
        in_specs=[
            pl.BlockSpec(blk4, lambda i: (i, 0, 0, 0)),
            pl.BlockSpec(blk4, lambda i: (i, 0, 0, 0)),
            pl.BlockSpec((BRANCH, BRANCH), lambda i: (0, 0)),
            pl.BlockSpec((1, BRANCH), lambda i: (0, 0)),
            pl.BlockSpec((BRANCH, D_MODEL), lambda i: (0, 0)),
        ],
        out_specs=xblk,
        out_shape=jax.ShapeDtypeStruct((BATCH, SEQ, D_MODEL), _F32),
        compiler_params=pltpu.CompilerParams(
            dimension_semantics=("parallel",), vmem_limit_bytes=VMEM_LIMIT),
        name="s5_out",
    )(y, gate, w_glu, b_glu, w_out)


SGU_TB = 1024
SGU_RB = 512
SGU_BPB = SGU_RB // SGU_BLOCK


def _sgu_kernel(x_ref, d_ref, g_ref, win_ref, lng_ref, lnb_ref, ws_ref, bs_ref, wout_ref, fg_ref, o_ref, mix_ref):
    blocks = [slice(r0, r0 + SGU_RB) for r0 in range(0, SGU_TB, SGU_RB)]
    st = [dict() for _ in blocks]

    def norm_in(i):
        st[i]["h"] = _rms(x_ref[blocks[i], :] + d_ref[blocks[i], :], g_ref[...]).astype(_BF)

    def proj_v(i):
        st[i]["v"] = _dot(st[i]["h"], win_ref[:, BRANCH:2 * BRANCH])

    def proj_ug(i):
        st[i]["u"] = _dot(st[i]["h"], win_ref[:, :BRANCH])
        st[i]["gate"] = _dot(st[i]["h"], win_ref[:, 2 * BRANCH:])

    def layer_norm(i):
        v = _gelu(st[i].pop("v"))
        mu = jnp.mean(v, axis=-1, keepdims=True)
        vc = v - mu
        var = jnp.mean(vc * vc, axis=-1, keepdims=True)
        st[i]["vn"] = (vc * lax.rsqrt(var + LN_EPS) * lng_ref[...] + lnb_ref[...]).astype(_BF)

    def spatial(i):
        vn = st[i].pop("vn")
        r0 = blocks[i].start
        causal = (lax.broadcasted_iota(jnp.int32, (SGU_BLOCK, SGU_BLOCK), 1)
                  <= lax.broadcasted_iota(jnp.int32, (SGU_BLOCK, SGU_BLOCK), 0))
        for hd in range(SGU_HEADS):
            lo = hd * SGU_HEAD_DIM
            w = jnp.where(causal, ws_ref[hd], 0.0).astype(_BF)
            bias = bs_ref[:, hd:hd + 1]
            rhs = jnp.concatenate(
                [vn[b * SGU_BLOCK:(b + 1) * SGU_BLOCK, lo:lo + SGU_HEAD_DIM] for b in range(SGU_BPB)], axis=1)
            res = _dot(w, rhs)
            for b in range(SGU_BPB):
                mix_ref[r0 + b * SGU_BLOCK:r0 + (b + 1) * SGU_BLOCK, lo:lo + SGU_HEAD_DIM] = (
                    res[:, b * SGU_HEAD_DIM:(b + 1) * SGU_HEAD_DIM] + bias)

    def activate(i):
        st[i]["ug"] = _gelu(st[i].pop("u")) * _silu_tanh(st[i].pop("gate"))

    def gating(i):
        st[i]["z"] = (st[i].pop("ug") * mix_ref[blocks[i], :]).astype(_BF)

    def proj_out(i):
        st[i]["out"] = _dot(st[i].pop("z"), wout_ref[...])

    def norm_out(i):
        x2 = (x_ref[blocks[i], :] + d_ref[blocks[i], :]) + st[i].pop("out")
        o_ref[blocks[i], :] = _rms(x2, fg_ref[...])

    _wavefront([norm_in, proj_v, proj_ug, layer_norm, activate, spatial, gating, proj_out, norm_out], len(blocks))


def _sgu_layer(x, delta, g, w_in, ln_g, ln_b, ws, bs, w_out, final_g):
    n = BATCH * SEQ
    grid = (n // SGU_TB,)
    row = lambda i: (0, 0)
    xblk = pl.BlockSpec((SGU_TB, D_MODEL), lambda i: (i, 0))
    return pl.pallas_call(
        _sgu_kernel,
        grid=grid,
        in_specs=[
            xblk,
            xblk,
            pl.BlockSpec((1, D_MODEL), row),
            pl.BlockSpec((D_MODEL, 3 * BRANCH), row),
            pl.BlockSpec((1, BRANCH), row),
            pl.BlockSpec((1, BRANCH), row),
            pl.BlockSpec((SGU_HEADS, SGU_BLOCK, SGU_BLOCK), lambda i: (0, 0, 0)),
            pl.BlockSpec((SGU_BLOCK, SGU_HEADS), row),
            pl.BlockSpec((BRANCH, D_MODEL), row),
            pl.BlockSpec((1, D_MODEL), row),
        ],
        out_specs=xblk,
        out_shape=jax.ShapeDtypeStruct((n, D_MODEL), _F32),
        scratch_shapes=[pltpu.VMEM((SGU_TB, BRANCH), _F32)],
        compiler_params=pltpu.CompilerParams(
            dimension_semantics=("parallel",), vmem_limit_bytes=VMEM_LIMIT),
        name="sgu_layer",
    )(x, delta, g, w_in, ln_g, ln_b, ws, bs, w_out, final_g)


def kernel(x, norm_g, final_g, s5_w_in, s5_A_re, s5_A_im, s5_log_dt, s5_B_re, s5_B_im, s5_C_re, s5_C_im, s5_D, s5_w_glu, s5_b_glu, s5_w_out, sgu_w_in, sgu_ln_g, sgu_ln_b, sgu_w_s, sgu_b_s, sgu_w_out):
    kpad, mb, mc, dec_re, dec_im = _s5_operators(
        s5_A_re[0], s5_A_im[0], s5_log_dt[0], s5_B_re[0], s5_B_im[0], s5_C_re[0], s5_C_im[0], s5_D[0])
    u, gate = _s5_in(x, norm_g[0][None, :], s5_w_in[0].astype(_BF))
    y = _s5_mix(u, kpad, mb, mc, dec_re, dec_im)
    delta = _s5_out(y, gate, s5_w_glu[0].astype(_BF), s5_b_glu[0][None, :], s5_w_out[0].astype(_BF))

    out = _sgu_layer(x.reshape(BATCH * SEQ, D_MODEL), delta.reshape(BATCH * SEQ, D_MODEL),
                     norm_g[1][None, :], sgu_w_in[0].astype(_BF),
                     sgu_ln_g[0][None, :], sgu_ln_b[0][None, :], sgu_w_s[0], jnp.transpose(sgu_b_s[0]),
                     sgu_w_out[0].astype(_BF), final_g[None, :])
    return out.reshape(BATCH, SEQ, D_MODEL)
```

```python
import math

import numpy as np
import jax
import jax.numpy as jnp
from jax import lax
from jax.experimental import pallas as pl
from jax.experimental.pallas import tpu as pltpu

D_MODEL = 1024
BATCH = 16
SEQ = 2048
BRANCH = D_MODEL
S5_GROUP = 16
S5_GROUPS = BRANCH // S5_GROUP
S5_STATE = 64
SGU_BLOCK = 128
SGU_HEADS = 8
SGU_HEAD_DIM = BRANCH // SGU_HEADS
RMS_EPS = 1e-6
LN_EPS = 1e-5
RE_CLIP = -1e-4

LANES = 128
TCH = 16
NCHUNK = SEQ // TCH
SLOTS = LANES // S5_GROUP
NJ = BRANCH // LANES
GW = TCH * S5_GROUP
PAIRS_PER_J = SLOTS // 2
VMEM_LIMIT = 56 * 1024 * 1024

_HI = lax.Precision.HIGHEST
_BF = jnp.bfloat16
_F32 = jnp.float32


_GELU_C1 = math.sqrt(2.0 / math.pi)
_GELU_C2 = _GELU_C1 * 0.044715


def _gelu(x):
    inner = x * (_GELU_C1 + _GELU_C2 * (x * x))
    return (0.5 * x) * (1.0 + jnp.tanh(inner))


def _sigmoid(x):
    return jax.nn.sigmoid(x)


def _silu_tanh(x):
    h = 0.5 * x
    return h + h * jnp.tanh(h)


def _dot(a, b):
    return jnp.dot(a, b, preferred_element_type=_F32)


def _wavefront(stages, n):
    for t in range(len(stages) + n - 1):
        for i in range(n):
            k = t - i
            if 0 <= k < len(stages):
                stages[k](i)


def _rms(x, g):
    return x * lax.rsqrt(jnp.mean(x * x, axis=-1, keepdims=True) + RMS_EPS) * g


KPAD = 2 * GW


def _slot_frame_table():
    a = np.arange(SLOTS)[:, None]
    x = np.arange(TCH)[None, :]
    half, k = x // SLOTS, x % SLOTS
    return SLOTS * half + (k - a) % SLOTS


_FRAME_OF_SLOT = _slot_frame_table()


def _s5_ops_kernel(lre_ref, lim_ref, ldr_ref, ldi_ref, btr_ref, bti_ref, cr_ref, ci_ref, d_ref,
                   kpad_ref, mb_ref, mc_ref, dre_ref, dim_ref):
    P = S5_STATE
    lane = lax.broadcasted_iota(jnp.int32, (S5_GROUP, LANES), 1)
    kcol = lax.broadcasted_iota(jnp.int32, (TCH, 1), 0)
    lane256 = lax.broadcasted_iota(jnp.int32, (S5_GROUP, GW), 1)
    row256 = lax.broadcasted_iota(jnp.int32, (S5_GROUP, GW), 0)
    lane_gw = lax.broadcasted_iota(jnp.int32, (GW, LANES), 1)
    nt = (((1,), (1,)), ((), ()))
    zeros64 = jnp.zeros((P, GW), _F32)
    st = [dict() for _ in range(SLOTS)]

    def outer(t_re, t_im, m_re, m_im):
        re = [t_re[k:k + 1] * m_re - t_im[k:k + 1] * m_im for k in range(TCH)]
        im = [t_re[k:k + 1] * m_im + t_im[k:k + 1] * m_re for k in range(TCH)]
        return jnp.concatenate(re, axis=0), jnp.concatenate(im, axis=0)

    def tables(a):
        lam_re, lam_im = lre_ref[a:a + 1, :], lim_ref[a:a + 1, :]
        ldr, ldi = ldr_ref[a:a + 1, :], ldi_ref[a:a + 1, :]

        def power(k):
            mag = jnp.exp(k * ldr)
            return mag * jnp.cos(k * ldi), mag * jnp.sin(k * ldi)

        frame = (kcol & SLOTS) + (((kcol & (SLOTS - 1)) + (SLOTS - a)) & (SLOTS - 1))
        pw_re, pw_im = power(kcol.astype(_F32))
        pi_re, pi_im = power((TCH - 1 - frame).astype(_F32))
        po_re, po_im = power((frame + 1).astype(_F32))
        st[a]["p16"] = power(jnp.full((1, 1), float(TCH), _F32))

        nr, ni = pw_re[1:2] - 1.0, pw_im[1:2]
        den = lam_re * lam_re + lam_im * lam_im
        cf_re = (nr * lam_re + ni * lam_im) / den
        cf_im = (ni * lam_re - nr * lam_im) / den
        b_re, b_im = btr_ref[a], bti_ref[a]
        bb_re = cf_re * b_re - cf_im * b_im
        bb_im = cf_re * b_im + cf_im * b_re
        c_re, c_im = cr_ref[a], ci_ref[a]
        st[a]["bb"] = (bb_re, bb_im)
        st[a]["lc"] = outer(pw_re, pw_im, c_re, c_im)
        st[a]["in"] = outer(pi_re, pi_im, bb_re, bb_im)
        st[a]["ot"] = outer(po_re, po_im, c_re, c_im)

    def taps(a):
        bb_re, bb_im = st[a].pop("bb")
        lc_re, lc_im = st[a].pop("lc")
        once = lane < P
        k = (lax.dot_general(jnp.where(once, bb_re, 0.0), lc_re, nt, precision=_HI, preferred_element_type=_F32)
             - lax.dot_general(jnp.where(once, bb_im, 0.0), lc_im, nt, precision=_HI, preferred_element_type=_F32))
        kpad_ref[a, :, 0:GW] = jnp.zeros((S5_GROUP, GW), _F32)
        kpad_ref[a, :, GW:2 * GW] = k + jnp.where(lane256 == row256, d_ref[a], 0.0)

    def state_in(a):
        q, m = a // 2, a % 2
        mine = (lane_gw >= P) if m else (lane_gw < P)
        in_re, in_im = st[a].pop("in")
        mb_ref[q, m * GW:(m + 1) * GW, 0:LANES] = jnp.where(mine, in_re, 0.0).astype(_BF)
        mb_ref[q, m * GW:(m + 1) * GW, LANES:2 * LANES] = jnp.where(mine, in_im, 0.0).astype(_BF)

    def state_out(a):
        q, m = a // 2, a % 2
        ot_re, ot_im = st[a].pop("ot")
        o_re, o_im = ot_re.T[0:P], ot_im.T[0:P]
        blocks = [zeros64, zeros64, zeros64, zeros64]
        blocks[m], blocks[2 + m] = o_re, -o_im
        mc_ref[a] = jnp.concatenate(blocks, axis=0).astype(_BF)
        if m:
            first = lax.broadcasted_iota(jnp.int32, (1, LANES), 1) < P
            dre_ref[q] = jnp.where(first, st[a - 1]["p16"][0], st[a]["p16"][0])
            dim_ref[q] = jnp.where(first, st[a - 1]["p16"][1], st[a]["p16"][1])

    _wavefront([tables, taps, state_in, state_out], SLOTS)


def _s5_operators(A_re, A_im, log_dt, B_re, B_im, C_re, C_im, D):
    f32 = jnp.float32
    G, P, H = S5_GROUPS, S5_STATE, S5_GROUP
    lam_re = jnp.minimum(A_re.astype(f32), RE_CLIP)
    lam_im = A_im.astype(f32)
    dt = jnp.exp(log_dt.astype(f32))[:, None]

    twice = lambda t: jnp.concatenate([t, t], axis=-1)
    tables = [twice(lam_re), twice(lam_im), twice(lam_re * dt), twice(lam_im * dt)]
    mats = [twice(jnp.swapaxes(B_re.astype(f32), 1, 2)), twice(jnp.swapaxes(B_im.astype(f32), 1, 2)),
            twice(C_re.astype(f32)), twice(C_im.astype(f32))]
    tspec = pl.BlockSpec((SLOTS, LANES), lambda j: (j, 0))
    mspec = pl.BlockSpec((SLOTS, H, LANES), lambda j: (j, 0, 0))
    dspec = pl.BlockSpec((PAIRS_PER_J, 1, LANES), lambda j: (j, 0, 0))
    return pl.pallas_call(
        _s5_ops_kernel,
        grid=(NJ,),
        in_specs=[tspec] * 4 + [mspec] * 4 + [pl.BlockSpec((SLOTS, H, 1), lambda j: (j, 0, 0))],
        out_specs=[
            pl.BlockSpec((SLOTS, H, KPAD), lambda j: (j, 0, 0)),
            pl.BlockSpec((PAIRS_PER_J, 2 * GW, GW), lambda j: (j, 0, 0)),
            pl.BlockSpec((SLOTS, GW, GW), lambda j: (j, 0, 0)),
            dspec, dspec,
        ],
        out_shape=[
            jax.ShapeDtypeStruct((G, H, KPAD), f32),
            jax.ShapeDtypeStruct((G // 2, 2 * GW, GW), _BF),
            jax.ShapeDtypeStruct((G, GW, GW), _BF),
            jax.ShapeDtypeStruct((G // 2, 1, LANES), f32),
            jax.ShapeDtypeStruct((G // 2, 1, LANES), f32),
        ],
        compiler_params=pltpu.CompilerParams(dimension_semantics=("parallel",), vmem_limit_bytes=VMEM_LIMIT),
        name="s5_ops",
    )(*tables, *mats, D.astype(f32).reshape(G, H, 1))


IN_CT = 8
IN_MB = 1


def _s5_in_kernel(x_ref, g_ref, w_ref, u_ref, gate_ref):
    rows = BATCH * TCH
    nb = IN_CT // IN_MB
    st = [dict() for _ in range(nb)]

    def norm(i):
        parts = []
        for c in range(i * IN_MB, (i + 1) * IN_MB):
            hc = _rms(x_ref[:, c * TCH:(c + 1) * TCH, :], g_ref[...]).astype(_BF)
            parts.append(jnp.swapaxes(hc, 0, 1).reshape(rows, D_MODEL))
        st[i]["hp"] = jnp.concatenate(parts, axis=0)

    def gate_mm(i):
        gate = _dot(st[i]["hp"], w_ref[:, BRANCH:])
        gate_ref[i * IN_MB:(i + 1) * IN_MB] = (
            (gate * _sigmoid(gate)).astype(_BF).reshape(IN_MB, TCH, BATCH, BRANCH))

    def u_mm(i):
        u = _dot(st[i].pop("hp"), w_ref[:, :BRANCH])
        u_ref[i * IN_MB:(i + 1) * IN_MB] = u.astype(_BF).reshape(IN_MB, TCH, BATCH, BRANCH)

    _wavefront([norm, gate_mm, u_mm], nb)


def _s5_in(x, g, w_in):
    grid = (NCHUNK // IN_CT,)
    blk4 = (IN_CT, TCH, BATCH, BRANCH)
    return pl.pallas_call(
        _s5_in_kernel,
        grid=grid,
        in_specs=[
            pl.BlockSpec((BATCH, IN_CT * TCH, D_MODEL), lambda i: (0, i, 0)),
            pl.BlockSpec((1, D_MODEL), lambda i: (0, 0)),
            pl.BlockSpec((D_MODEL, 2 * BRANCH), lambda i: (0, 0)),
        ],
        out_specs=[
            pl.BlockSpec(blk4, lambda i: (i, 0, 0, 0)),
            pl.BlockSpec(blk4, lambda i: (i, 0, 0, 0)),
        ],
        out_shape=[
            jax.ShapeDtypeStruct((NCHUNK, TCH, BATCH, BRANCH), _BF),
            jax.ShapeDtypeStruct((NCHUNK, TCH, BATCH, BRANCH), _BF),
        ],
        compiler_params=pltpu.CompilerParams(
            dimension_semantics=("parallel",), vmem_limit_bytes=VMEM_LIMIT),
        name="s5_in",
    )(x, g, w_in)


MIX_CT = 64
MIX_R = MIX_CT * BATCH
MIX_CB = 4
MIX_RT = MIX_CB * BATCH
MIX_NH = 4
MIX_HC = MIX_CT // MIX_NH
MIX_HR = MIX_HC * BATCH


def _gather_slots(v, slot):
    rolled = [v[0]] + [pltpu.roll(v[k], k * S5_GROUP, axis=1) for k in range(1, SLOTS)]
    out = []
    for a in range(SLOTS):
        acc = rolled[0]
        for k in range(1, SLOTS):
            acc = jnp.where(slot == (a + k) % SLOTS, rolled[k], acc)
        out.append(acc)
    return out


def _scatter_slots(v, slot):
    out = []
    for k in range(SLOTS):
        acc = v[0]
        for a in range(1, SLOTS):
            acc = jnp.where(slot == (a + k) % SLOTS, v[a], acc)
        out.append(acc if k == 0 else pltpu.roll(acc, (SLOTS - k) * S5_GROUP, axis=1))
    return out


def _column_perms():
    p = np.zeros((SLOTS, GW, GW), np.float32)
    for a in range(SLOTS):
        for x in range(TCH):
            f = int(_FRAME_OF_SLOT[a, x])
            for c in range(S5_GROUP):
                p[a, f * S5_GROUP + c, x * S5_GROUP + c] = 1.0
    return p


_CPERM = _column_perms()


def _s5_mix_kernel(u_ref, kpad_ref, mb_ref, mc_ref, cperm_ref, dre_ref, dim_ref, y_ref,
                   op_ref, xs_ref, ys_ref, ss_ref, hs_ref, carry_ref):
    ct = pl.program_id(1)

    @pl.when(ct == 0)
    def _():
        carry_ref[...] = jnp.zeros_like(carry_ref)
        for a in range(SLOTS):
            kp = kpad_ref[a]
            rows = []
            for x in range(TCH):
                off = (TCH - int(_FRAME_OF_SLOT[a, x])) * S5_GROUP
                rows.append(kp[:, off:off + GW])
            toep = jnp.concatenate(rows, axis=0).astype(_BF)
            op_ref[a, 0:GW, :] = _dot(toep, cperm_ref[a]).astype(_BF)
            op_ref[a, GW:2 * GW, :] = mc_ref[a]

    slot = lax.broadcasted_iota(jnp.int32, (MIX_RT, LANES), 1) // S5_GROUP
    dre = [dre_ref[q] for q in range(PAIRS_PER_J)]
    dim_ = [dim_ref[q] for q in range(PAIRS_PER_J)]

    def relayout_in(c0):
        r0 = c0 * BATCH
        for half in range(2):
            pieces = [u_ref[c0:c0 + MIX_CB, SLOTS * half + k, :, :].reshape(MIX_RT, LANES) for k in range(SLOTS)]
            groups = _gather_slots(pieces, slot)
            for a in range(SLOTS):
                lo = (a % 2) * GW + half * LANES
                xs_ref[a // 2, r0:r0 + MIX_RT, lo:lo + LANES] = groups[a]

    def relayout_out(c0):
        r0 = c0 * BATCH
        for half in range(2):
            groups = [ys_ref[a, r0:r0 + MIX_RT, half * LANES:(half + 1) * LANES].astype(_BF) for a in range(SLOTS)]
            frames = _scatter_slots(groups, slot)
            for k in range(SLOTS):
                y_ref[c0:c0 + MIX_CB, SLOTS * half + k, :, :] = frames[k].reshape(MIX_CB, BATCH, LANES)

    def recurrence(c0, state):
        for c in range(c0, c0 + MIX_HC):
            r0 = c * BATCH
            new = []
            for q in range(PAIRS_PER_J):
                hre, him = state[2 * q], state[2 * q + 1]
                hs_ref[q, r0:r0 + BATCH, 0:LANES] = hre.astype(_BF)
                hs_ref[q, r0:r0 + BATCH, LANES:2 * LANES] = him.astype(_BF)
                sre = ss_ref[q, r0:r0 + BATCH, 0:LANES]
                sim = ss_ref[q, r0:r0 + BATCH, LANES:2 * LANES]
                new.append(dre[q] * hre - dim_[q] * him + sre)
                new.append(dre[q] * him + dim_[q] * hre + sim)
            state = new
        return state

    blocks = [(h * MIX_HC, h * MIX_HC * BATCH) for h in range(MIX_NH)]
    for c0, r0 in blocks:
        for t in range(MIX_HC // MIX_CB):
            relayout_in(c0 + t * MIX_CB)
        for q in range(PAIRS_PER_J):
            ss_ref[q, r0:r0 + MIX_HR, :] = _dot(xs_ref[q, r0:r0 + MIX_HR, :], mb_ref[q])
    state = [carry_ref[q, k] for q in range(PAIRS_PER_J) for k in range(2)]
    for c0, r0 in blocks:
        state = recurrence(c0, state)
    for q in range(PAIRS_PER_J):
        carry_ref[q, 0] = state[2 * q]
        carry_ref[q, 1] = state[2 * q + 1]
    for c0, r0 in blocks:
        for a in range(SLOTS):
            q, m = a // 2, a % 2
            lhs = jnp.concatenate([xs_ref[q, r0:r0 + MIX_HR, m * GW:(m + 1) * GW], hs_ref[q, r0:r0 + MIX_HR, :]], axis=1)
            ys_ref[a, r0:r0 + MIX_HR, :] = _dot(lhs, op_ref[a])
    for c0, r0 in blocks:
        for t in range(MIX_HC // MIX_CB):
            relayout_out(c0 + t * MIX_CB)


def _s5_mix(u, kpad, mb, mc, dec_re, dec_im):
    grid = (NJ, NCHUNK // MIX_CT)
    blk = (MIX_CT, TCH, BATCH, LANES)
    dspec = pl.BlockSpec((PAIRS_PER_J, 1, LANES), lambda j, c: (j, 0, 0))
    return pl.pallas_call(
        _s5_mix_kernel,
        grid=grid,
        in_specs=[
            pl.BlockSpec(blk, lambda j, c: (c, 0, 0, j)),
            pl.BlockSpec((SLOTS, S5_GROUP, KPAD), lambda j, c: (j, 0, 0)),
            pl.BlockSpec((PAIRS_PER_J, 2 * GW, GW), lambda j, c: (j, 0, 0)),
            pl.BlockSpec((SLOTS, GW, GW), lambda j, c: (j, 0, 0)),
            pl.BlockSpec((SLOTS, GW, GW), lambda j, c: (0, 0, 0)),
            dspec, dspec,
        ],
        out_specs=pl.BlockSpec(blk, lambda j, c: (c, 0, 0, j)),
        out_shape=jax.ShapeDtypeStruct((NCHUNK, TCH, BATCH, BRANCH), _BF),
        scratch_shapes=[
            pltpu.VMEM((SLOTS, 2 * GW, GW), _BF),
            pltpu.VMEM((PAIRS_PER_J, MIX_R, 2 * GW), _BF),
            pltpu.VMEM((SLOTS, MIX_R, GW), _F32),
            pltpu.VMEM((PAIRS_PER_J, MIX_R, GW), _F32),
            pltpu.VMEM((PAIRS_PER_J, MIX_R, GW), _BF),
            pltpu.VMEM((PAIRS_PER_J, 2, BATCH, LANES), _F32),
        ],
        compiler_params=pltpu.CompilerParams(
            dimension_semantics=("parallel", "arbitrary"), vmem_limit_bytes=VMEM_LIMIT),
        name="s5_mix",
    )(u, kpad, mb, mc, jnp.asarray(_CPERM, _BF), dec_re, dec_im)


def _s5_out_kernel(y_ref, gate_ref, wglu_ref, bglu_ref, wout_ref, o_ref):
    rows = IN_MB * BATCH * TCH
    nb = IN_CT // IN_MB
    st = [dict() for _ in range(nb)]

    def act(i):
        st[i]["y"] = _gelu(y_ref[i * IN_MB:(i + 1) * IN_MB].reshape(rows, BRANCH).astype(_F32))

    def glu_mm(i):
        st[i]["glu"] = _dot(st[i]["y"].astype(_BF), wglu_ref[...])

    def half_glu(i):
        st[i]["y"] = st[i].pop("y") * _sigmoid(st[i].pop("glu") + bglu_ref[...])

    def gating(i):
        y = st[i].pop("y")
        sg = gate_ref[i * IN_MB:(i + 1) * IN_MB].reshape(rows, BRANCH).astype(_F32)
        z = (y * sg).astype(_BF).reshape(IN_MB, TCH, BATCH, BRANCH)
        parts = [jnp.swapaxes(z[k], 0, 1).reshape(BATCH * TCH, BRANCH) for k in range(IN_MB)]
        st[i]["z"] = parts[0] if IN_MB == 1 else jnp.concatenate(parts, axis=0)

    def out_mm(i):
        st[i]["out"] = _dot(st[i].pop("z"), wout_ref[...])

    def residual(i):
        out = st[i].pop("out").reshape(IN_MB, BATCH, TCH, D_MODEL)
        for k in range(IN_MB):
            c = i * IN_MB + k
            o_ref[:, c * TCH:(c + 1) * TCH, :] = out[k]

    _wavefront([act, glu_mm, half_glu, gating, out_mm, residual], nb)


def _s5_out(y, gate, w_glu, b_glu, w_out):
    grid = (NCHUNK // IN_CT,)
    blk4 = (IN_CT, TCH, BATCH, BRANCH)
    xblk = pl.BlockSpec((BATCH, IN_CT * TCH, D_MODEL), lambda i: (0, i, 0))
    return pl.pallas_call(
        _s5_out_kernel,
        grid=grid,
        in_specs=[
            pl.BlockSpec(blk4, lambda i: (i, 0, 0, 0)),
            pl.BlockSpec(blk4, lambda i: (i, 0, 0, 0)),
            pl.BlockSpec((BRANCH, BRANCH), lambda i: (0, 0)),
            pl.BlockSpec((1, BRANCH), lambda i: (0, 0)),
            pl.BlockSpec((BRANCH, D_MODEL), lambda i: (0, 0)),
        ],
        out_specs=xblk,
        out_shape=jax.ShapeDtypeStruct((BATCH, SEQ, D_MODEL), _F32),
        compiler_params=pltpu.CompilerParams(
            dimension_semantics=("parallel",), vmem_limit_bytes=VMEM_LIMIT),
        name="s5_out",
    )(y, gate, w_glu, b_glu, w_out)


SGU_TB = 1024
SGU_RB = 512
SGU_BPB = SGU_RB // SGU_BLOCK


def _sgu_kernel(x_ref, d_ref, g_ref, win_ref, lng_ref, lnb_ref, ws_ref, bs_ref, wout_ref, fg_ref, o_ref, mix_ref):
    blocks = [slice(r0, r0 + SGU_RB) for r0 in range(0, SGU_TB, SGU_RB)]
    st = [dict() for _ in blocks]

    def norm_in(i):
        st[i]["x1"] = x_ref[blocks[i], :] + d_ref[blocks[i], :]
        st[i]["h"] = _rms(st[i]["x1"], g_ref[...]).astype(_BF)

    def proj_v(i):
        st[i]["v"] = _dot(st[i]["h"], win_ref[:, BRANCH:2 * BRANCH])

    def proj_ug(i):
        st[i]["u"] = _dot(st[i]["h"], win_ref[:, :BRANCH])
        st[i]["gate"] = _dot(st[i]["h"], win_ref[:, 2 * BRANCH:])

    def layer_norm(i):
        v = _gelu(st[i].pop("v"))
        mu = jnp.mean(v, axis=-1, keepdims=True)
        vc = v - mu
        var = jnp.mean(vc * vc, axis=-1, keepdims=True)
        st[i]["vn"] = (vc * lax.rsqrt(var + LN_EPS) * lng_ref[...] + lnb_ref[...]).astype(_BF)

    def spatial(i):
        vn = st[i].pop("vn")
        r0 = blocks[i].start
        for hd in range(SGU_HEADS):
            lo = hd * SGU_HEAD_DIM
            rhs = jnp.concatenate(
                [vn[b * SGU_BLOCK:(b + 1) * SGU_BLOCK, lo:lo + SGU_HEAD_DIM] for b in range(SGU_BPB)], axis=1)
            res = _dot(ws_ref[hd], rhs)
            for b in range(SGU_BPB):
                mix_ref[r0 + b * SGU_BLOCK:r0 + (b + 1) * SGU_BLOCK, lo:lo + SGU_HEAD_DIM] = (
                    res[:, b * SGU_HEAD_DIM:(b + 1) * SGU_HEAD_DIM] + bs_ref[:, lo:lo + SGU_HEAD_DIM])

    def activate(i):
        st[i]["ug"] = _gelu(st[i].pop("u")) * _silu_tanh(st[i].pop("gate"))

    def gating(i):
        st[i]["z"] = (st[i].pop("ug") * mix_ref[blocks[i], :]).astype(_BF)

    def proj_out(i):
        st[i]["out"] = _dot(st[i].pop("z"), wout_ref[...])

    def norm_out(i):
        x2 = st[i].pop("x1") + st[i].pop("out")
        o_ref[blocks[i], :] = _rms(x2, fg_ref[...])

    _wavefront([norm_in, proj_v, proj_ug, layer_norm, activate, spatial, gating, proj_out, norm_out], len(blocks))


def _sgu_layer(x, delta, g, w_in, ln_g, ln_b, ws, bs, w_out, final_g):
    n = BATCH * SEQ
    grid = (n // SGU_TB,)
    row = lambda i: (0, 0)
    xblk = pl.BlockSpec((SGU_TB, D_MODEL), lambda i: (i, 0))
    return pl.pallas_call(
        _sgu_kernel,
        grid=grid,
        in_specs=[
            xblk,
            xblk,
            pl.BlockSpec((1, D_MODEL), row),
            pl.BlockSpec((D_MODEL, 3 * BRANCH), row),
            pl.BlockSpec((1, BRANCH), row),
            pl.BlockSpec((1, BRANCH), row),
            pl.BlockSpec((SGU_HEADS, SGU_BLOCK, SGU_BLOCK), lambda i: (0, 0, 0)),
            pl.BlockSpec((SGU_BLOCK, BRANCH), row),
            pl.BlockSpec((BRANCH, D_MODEL), row),
            pl.BlockSpec((1, D_MODEL), row),
        ],
        out_specs=xblk,
        out_shape=jax.ShapeDtypeStruct((n, D_MODEL), _F32),
        scratch_shapes=[pltpu.VMEM((SGU_TB, BRANCH), _F32)],
        compiler_params=pltpu.CompilerParams(
            dimension_semantics=("parallel",), vmem_limit_bytes=VMEM_LIMIT),
        name="sgu_layer",
    )(x, delta, g, w_in, ln_g, ln_b, ws, bs, w_out, final_g)


def kernel(x, norm_g, final_g, s5_w_in, s5_A_re, s5_A_im, s5_log_dt, s5_B_re, s5_B_im, s5_C_re, s5_C_im, s5_D, s5_w_glu, s5_b_glu, s5_w_out, sgu_w_in, sgu_ln_g, sgu_ln_b, sgu_w_s, sgu_b_s, sgu_w_out):
    kpad, mb, mc, dec_re, dec_im = _s5_operators(
        s5_A_re[0], s5_A_im[0], s5_log_dt[0], s5_B_re[0], s5_B_im[0], s5_C_re[0], s5_C_im[0], s5_D[0])
    u, gate = _s5_in(x, norm_g[0][None, :], s5_w_in[0].astype(_BF))
    y = _s5_mix(u, kpad, mb, mc, dec_re, dec_im)
    delta = _s5_out(y, gate, s5_w_glu[0].astype(_BF), s5_b_glu[0][None, :], s5_w_out[0].astype(_BF))

    mask = jnp.tril(jnp.ones((SGU_BLOCK, SGU_BLOCK), dtype=bool))
    ws = jnp.where(mask[None], sgu_w_s[0], 0.0).astype(_BF)
    bs = jnp.repeat(jnp.transpose(sgu_b_s[0]), SGU_HEAD_DIM, axis=1)
    out = _sgu_layer(x.reshape(BATCH * SEQ, D_MODEL), delta.reshape(BATCH * SEQ, D_MODEL),
                     norm_g[1][None, :], sgu_w_in[0].astype(_BF),
                     sgu_ln_g[0][None, :], sgu_ln_b[0][None, :], ws, bs,
                     sgu_w_out[0].astype(_BF), final_g[None, :])
    return out.reshape(BATCH, SEQ, D_MODEL)
```

```python
import math

import numpy as np
import jax
import jax.numpy as jnp
from jax import lax
from jax.experimental import pallas as pl
from jax.experimental.pallas import tpu as pltpu

D_MODEL = 1024
BATCH = 16
SEQ = 2048
BRANCH = D_MODEL
S5_GROUP = 16
S5_GROUPS = BRANCH // S5_GROUP
S5_STATE = 64
SGU_BLOCK = 128
SGU_HEADS = 8
SGU_HEAD_DIM = BRANCH // SGU_HEADS
RMS_EPS = 1e-6
LN_EPS = 1e-5
RE_CLIP = -1e-4

LANES = 128
TCH = 16
NCHUNK = SEQ // TCH
SLOTS = LANES // S5_GROUP
NJ = BRANCH // LANES
GW = TCH * S5_GROUP
PAIRS_PER_J = SLOTS // 2
VMEM_LIMIT = 56 * 1024 * 1024

_HI = lax.Precision.HIGHEST
_BF = jnp.bfloat16
_F32 = jnp.float32


_GELU_C1 = math.sqrt(2.0 / math.pi)
_GELU_C2 = _GELU_C1 * 0.044715


def _gelu(x):
    inner = x * (_GELU_C1 + _GELU_C2 * (x * x))
    return (0.5 * x) * (1.0 + jnp.tanh(inner))


def _sigmoid(x):
    return jax.nn.sigmoid(x)


def _silu_tanh(x):
    h = 0.5 * x
    return h + h * jnp.tanh(h)


def _dot(a, b):
    return jnp.dot(a, b, preferred_element_type=_F32)


def _wavefront(stages, n):
    for t in range(len(stages) + n - 1):
        for i in range(n):
            k = t - i
            if 0 <= k < len(stages):
                stages[k](i)


def _rms(x, g):
    return x * lax.rsqrt(jnp.mean(x * x, axis=-1, keepdims=True) + RMS_EPS) * g


KPAD = 2 * GW


def _slot_frame_table():
    a = np.arange(SLOTS)[:, None]
    x = np.arange(TCH)[None, :]
    half, k = x // SLOTS, x % SLOTS
    return SLOTS * half + (k - a) % SLOTS


_FRAME_OF_SLOT = _slot_frame_table()


def _s5_ops_kernel(lre_ref, lim_ref, ldr_ref, ldi_ref, btr_ref, bti_ref, cr_ref, ci_ref, d_ref,
                   kpad_ref, mb_ref, mc_ref, dre_ref, dim_ref):
    P = S5_STATE
    lane = lax.broadcasted_iota(jnp.int32, (S5_GROUP, LANES), 1)
    kcol = lax.broadcasted_iota(jnp.int32, (TCH, 1), 0)
    lane256 = lax.broadcasted_iota(jnp.int32, (S5_GROUP, GW), 1)
    row256 = lax.broadcasted_iota(jnp.int32, (S5_GROUP, GW), 0)
    lane_gw = lax.broadcasted_iota(jnp.int32, (GW, LANES), 1)
    nt = (((1,), (1,)), ((), ()))
    zeros64 = jnp.zeros((P, GW), _F32)
    st = [dict() for _ in range(SLOTS)]

    def outer(t_re, t_im, m_re, m_im):
        re = [t_re[k:k + 1] * m_re - t_im[k:k + 1] * m_im for k in range(TCH)]
        im = [t_re[k:k + 1] * m_im + t_im[k:k + 1] * m_re for k in range(TCH)]
        return jnp.concatenate(re, axis=0), jnp.concatenate(im, axis=0)

    def tables(a):
        lam_re, lam_im = lre_ref[a:a + 1, :], lim_ref[a:a + 1, :]
        ldr, ldi = ldr_ref[a:a + 1, :], ldi_ref[a:a + 1, :]

        def power(k):
            mag = jnp.exp(k * ldr)
            return mag * jnp.cos(k * ldi), mag * jnp.sin(k * ldi)

        frame = (kcol & SLOTS) + (((kcol & (SLOTS - 1)) + (SLOTS - a)) & (SLOTS - 1))
        pw_re, pw_im = power(kcol.astype(_F32))
        pi_re, pi_im = power((TCH - 1 - frame).astype(_F32))
        po_re, po_im = power((frame + 1).astype(_F32))
        st[a]["p16"] = power(jnp.full((1, 1), float(TCH), _F32))

        nr, ni = pw_re[1:2] - 1.0, pw_im[1:2]
        den = lam_re * lam_re + lam_im * lam_im
        cf_re = (nr * lam_re + ni * lam_im) / den
        cf_im = (ni * lam_re - nr * lam_im) / den
        b_re, b_im = btr_ref[a], bti_ref[a]
        bb_re = cf_re * b_re - cf_im * b_im
        bb_im = cf_re * b_im + cf_im * b_re
        c_re, c_im = cr_ref[a], ci_ref[a]
        st[a]["bb"] = (bb_re, bb_im)
        st[a]["lc"] = outer(pw_re, pw_im, c_re, c_im)
        st[a]["in"] = outer(pi_re, pi_im, bb_re, bb_im)
        st[a]["ot"] = outer(po_re, po_im, c_re, c_im)

    def taps(a):
        bb_re, bb_im = st[a].pop("bb")
        lc_re, lc_im = st[a].pop("lc")
        once = lane < P
        k = (lax.dot_general(jnp.where(once, bb_re, 0.0), lc_re, nt, precision=_HI, preferred_element_type=_F32)
             - lax.dot_general(jnp.where(once, bb_im, 0.0), lc_im, nt, precision=_HI, preferred_element_type=_F32))
        kpad_ref[a, :, 0:GW] = jnp.zeros((S5_GROUP, GW), _F32)
        kpad_ref[a, :, GW:2 * GW] = k + jnp.where(lane256 == row256, d_ref[a], 0.0)

    def state_in(a):
        q, m = a // 2, a % 2
        mine = (lane_gw >= P) if m else (lane_gw < P)
        in_re, in_im = st[a].pop("in")
        mb_ref[q, m * GW:(m + 1) * GW, 0:LANES] = jnp.where(mine, in_re, 0.0).astype(_BF)
        mb_ref[q, m * GW:(m + 1) * GW, LANES:2 * LANES] = jnp.where(mine, in_im, 0.0).astype(_BF)

    def state_out(a):
        q, m = a // 2, a % 2
        ot_re, ot_im = st[a].pop("ot")
        o_re, o_im = ot_re.T[0:P], ot_im.T[0:P]
        blocks = [zeros64, zeros64, zeros64, zeros64]
        blocks[m], blocks[2 + m] = o_re, -o_im
        mc_ref[a] = jnp.concatenate(blocks, axis=0).astype(_BF)
        if m:
            first = lax.broadcasted_iota(jnp.int32, (1, LANES), 1) < P
            dre_ref[q] = jnp.where(first, st[a - 1]["p16"][0], st[a]["p16"][0])
            dim_ref[q] = jnp.where(first, st[a - 1]["p16"][1], st[a]["p16"][1])

    _wavefront([tables, taps, state_in, state_out], SLOTS)


def _s5_operators(A_re, A_im, log_dt, B_re, B_im, C_re, C_im, D):
    f32 = jnp.float32
    G, P, H = S5_GROUPS, S5_STATE, S5_GROUP
    lam_re = jnp.minimum(A_re.astype(f32), RE_CLIP)
    lam_im = A_im.astype(f32)
    dt = jnp.exp(log_dt.astype(f32))[:, None]

    twice = lambda t: jnp.concatenate([t, t], axis=-1)
    tables = [twice(lam_re), twice(lam_im), twice(lam_re * dt), twice(lam_im * dt)]
    mats = [twice(jnp.swapaxes(B_re.astype(f32), 1, 2)), twice(jnp.swapaxes(B_im.astype(f32), 1, 2)),
            twice(C_re.astype(f32)), twice(C_im.astype(f32))]
    tspec = pl.BlockSpec((SLOTS, LANES), lambda j: (j, 0))
    mspec = pl.BlockSpec((SLOTS, H, LANES), lambda j: (j, 0, 0))
    dspec = pl.BlockSpec((PAIRS_PER_J, 1, LANES), lambda j: (j, 0, 0))
    return pl.pallas_call(
        _s5_ops_kernel,
        grid=(NJ,),
        in_specs=[tspec] * 4 + [mspec] * 4 + [pl.BlockSpec((SLOTS, H, 1), lambda j: (j, 0, 0))],
        out_specs=[
            pl.BlockSpec((SLOTS, H, KPAD), lambda j: (j, 0, 0)),
            pl.BlockSpec((PAIRS_PER_J, 2 * GW, GW), lambda j: (j, 0, 0)),
            pl.BlockSpec((SLOTS, GW, GW), lambda j: (j, 0, 0)),
            dspec, dspec,
        ],
        out_shape=[
            jax.ShapeDtypeStruct((G, H, KPAD), f32),
            jax.ShapeDtypeStruct((G // 2, 2 * GW, GW), _BF),
            jax.ShapeDtypeStruct((G, GW, GW), _BF),
            jax.ShapeDtypeStruct((G // 2, 1, LANES), f32),
            jax.ShapeDtypeStruct((G // 2, 1, LANES), f32),
        ],
        compiler_params=pltpu.CompilerParams(dimension_semantics=("parallel",), vmem_limit_bytes=VMEM_LIMIT),
        name="s5_ops",
    )(*tables, *mats, D.astype(f32).reshape(G, H, 1))


IN_CT = 8
IN_MB = 1


def _s5_in_kernel(x_ref, g_ref, w_ref, u_ref, gate_ref):
    rows = BATCH * TCH
    nb = IN_CT // IN_MB
    st = [dict() for _ in range(nb)]

    def norm(i):
        parts = []
        for c in range(i * IN_MB, (i + 1) * IN_MB):
            hc = _rms(x_ref[:, c * TCH:(c + 1) * TCH, :], g_ref[...]).astype(_BF)
            parts.append(jnp.swapaxes(hc, 0, 1).reshape(rows, D_MODEL))
        st[i]["hp"] = jnp.concatenate(parts, axis=0)

    def gate_mm(i):
        gate = _dot(st[i]["hp"], w_ref[:, BRANCH:])
        gate_ref[i * IN_MB:(i + 1) * IN_MB] = (
            (gate * _sigmoid(gate)).astype(_BF).reshape(IN_MB, TCH, BATCH, BRANCH))

    def u_mm(i):
        u = _dot(st[i].pop("hp"), w_ref[:, :BRANCH])
        u_ref[i * IN_MB:(i + 1) * IN_MB] = u.astype(_BF).reshape(IN_MB, TCH, BATCH, BRANCH)

    _wavefront([norm, gate_mm, u_mm], nb)


def _s5_in(x, g, w_in):
    grid = (NCHUNK // IN_CT,)
    blk4 = (IN_CT, TCH, BATCH, BRANCH)
    return pl.pallas_call(
        _s5_in_kernel,
        grid=grid,
        in_specs=[
            pl.BlockSpec((BATCH, IN_CT * TCH, D_MODEL), lambda i: (0, i, 0)),
            pl.BlockSpec((1, D_MODEL), lambda i: (0, 0)),
            pl.BlockSpec((D_MODEL, 2 * BRANCH), lambda i: (0, 0)),
        ],
        out_specs=[
            pl.BlockSpec(blk4, lambda i: (i, 0, 0, 0)),
            pl.BlockSpec(blk4, lambda i: (i, 0, 0, 0)),
        ],
        out_shape=[
            jax.ShapeDtypeStruct((NCHUNK, TCH, BATCH, BRANCH), _BF),
            jax.ShapeDtypeStruct((NCHUNK, TCH, BATCH, BRANCH), _BF),
        ],
        compiler_params=pltpu.CompilerParams(
            dimension_semantics=("parallel",), vmem_limit_bytes=VMEM_LIMIT),
        name="s5_in",
    )(x, g, w_in)


MIX_CT = 64
MIX_R = MIX_CT * BATCH
MIX_CB = 4
MIX_RT = MIX_CB * BATCH
MIX_NH = 4
MIX_HC = MIX_CT // MIX_NH
MIX_HR = MIX_HC * BATCH


def _gather_slots(v, slot):
    rolled = [v[0]] + [pltpu.roll(v[k], k * S5_GROUP, axis=1) for k in range(1, SLOTS)]
    out = []
    for a in range(SLOTS):
        acc = rolled[0]
        for k in range(1, SLOTS):
            acc = jnp.where(slot == (a + k) % SLOTS, rolled[k], acc)
        out.append(acc)
    return out


def _scatter_slots(v, slot):
    out = []
    for k in range(SLOTS):
        acc = v[0]
        for a in range(1, SLOTS):
            acc = jnp.where(slot == (a + k) % SLOTS, v[a], acc)
        out.append(acc if k == 0 else pltpu.roll(acc, (SLOTS - k) * S5_GROUP, axis=1))
    return out


def _column_perms():
    p = np.zeros((SLOTS, GW, GW), np.float32)
    for a in range(SLOTS):
        for x in range(TCH):
            f = int(_FRAME_OF_SLOT[a, x])
            for c in range(S5_GROUP):
                p[a, f * S5_GROUP + c, x * S5_GROUP + c] = 1.0
    return p


_CPERM = _column_perms()


def _s5_mix_kernel(u_ref, kpad_ref, mb_ref, mc_ref, cperm_ref, dre_ref, dim_ref, y_ref,
                   op_ref, xs_ref, ys_ref, ss_ref, hs_ref, carry_ref):
    ct = pl.program_id(1)

    @pl.when(ct == 0)
    def _():
        carry_ref[...] = jnp.zeros_like(carry_ref)
        for a in range(SLOTS):
            kp = kpad_ref[a]
            rows = []
            for x in range(TCH):
                off = (TCH - int(_FRAME_OF_SLOT[a, x])) * S5_GROUP
                rows.append(kp[:, off:off + GW])
            toep = jnp.concatenate(rows, axis=0).astype(_BF)
            op_ref[a, 0:GW, :] = _dot(toep, cperm_ref[a]).astype(_BF)
            op_ref[a, GW:2 * GW, :] = mc_ref[a]

    slot = lax.broadcasted_iota(jnp.int32, (MIX_RT, LANES), 1) // S5_GROUP
    dre = [dre_ref[q] for q in range(PAIRS_PER_J)]
    dim_ = [dim_ref[q] for q in range(PAIRS_PER_J)]

    def relayout_in(c0):
        r0 = c0 * BATCH
        for half in range(2):
            pieces = [u_ref[c0:c0 + MIX_CB, SLOTS * half + k, :, :].reshape(MIX_RT, LANES) for k in range(SLOTS)]
            groups = _gather_slots(pieces, slot)
            for a in range(SLOTS):
                lo = (a % 2) * GW + half * LANES
                xs_ref[a // 2, r0:r0 + MIX_RT, lo:lo + LANES] = groups[a]

    def relayout_out(c0):
        r0 = c0 * BATCH
        for half in range(2):
            groups = [ys_ref[a, r0:r0 + MIX_RT, half * LANES:(half + 1) * LANES].astype(_BF) for a in range(SLOTS)]
            frames = _scatter_slots(groups, slot)
            for k in range(SLOTS):
                y_ref[c0:c0 + MIX_CB, SLOTS * half + k, :, :] = frames[k].reshape(MIX_CB, BATCH, LANES)

    def recurrence(c0, state):
        for c in range(c0, c0 + MIX_HC):
            r0 = c * BATCH
            new = []
            for q in range(PAIRS_PER_J):
                hre, him = state[2 * q], state[2 * q + 1]
                hs_ref[q, r0:r0 + BATCH, 0:LANES] = hre.astype(_BF)
                hs_ref[q, r0:r0 + BATCH, LANES:2 * LANES] = him.astype(_BF)
                sre = ss_ref[q, r0:r0 + BATCH, 0:LANES]
                sim = ss_ref[q, r0:r0 + BATCH, LANES:2 * LANES]
                new.append(dre[q] * hre - dim_[q] * him + sre)
                new.append(dre[q] * him + dim_[q] * hre + sim)
            state = new
        return state

    blocks = [(h * MIX_HC, h * MIX_HC * BATCH) for h in range(MIX_NH)]
    for c0, r0 in blocks:
        for t in range(MIX_HC // MIX_CB):
            relayout_in(c0 + t * MIX_CB)
        for q in range(PAIRS_PER_J):
            ss_ref[q, r0:r0 + MIX_HR, :] = _dot(xs_ref[q, r0:r0 + MIX_HR, :], mb_ref[q])
    state = [carry_ref[q, k] for q in range(PAIRS_PER_J) for k in range(2)]
    for c0, r0 in blocks:
        state = recurrence(c0, state)
    for q in range(PAIRS_PER_J):
        carry_ref[q, 0] = state[2 * q]
        carry_ref[q, 1] = state[2 * q + 1]
    for c0, r0 in blocks:
        for a in range(SLOTS):
            q, m = a // 2, a % 2
            lhs = jnp.concatenate([xs_ref[q, r0:r0 + MIX_HR, m * GW:(m + 1) * GW], hs_ref[q, r0:r0 + MIX_HR, :]], axis=1)
            ys_ref[a, r0:r0 + MIX_HR, :] = _dot(lhs, op_ref[a])
    for c0, r0 in blocks:
        for t in range(MIX_HC // MIX_CB):
            relayout_out(c0 + t * MIX_CB)


def _s5_mix(u, kpad, mb, mc, dec_re, dec_im):
    grid = (NJ, NCHUNK // MIX_CT)
    blk = (MIX_CT, TCH, BATCH, LANES)
    dspec = pl.BlockSpec((PAIRS_PER_J, 1, LANES), lambda j, c: (j, 0, 0))
    return pl.pallas_call(
        _s5_mix_kernel,
        grid=grid,
        in_specs=[
            pl.BlockSpec(blk, lambda j, c: (c, 0, 0, j)),
            pl.BlockSpec((SLOTS, S5_GROUP, KPAD), lambda j, c: (j, 0, 0)),
            pl.BlockSpec((PAIRS_PER_J, 2 * GW, GW), lambda j, c: (j, 0, 0)),
            pl.BlockSpec((SLOTS, GW, GW), lambda j, c: (j, 0, 0)),
            pl.BlockSpec((SLOTS, GW, GW), lambda j, c: (0, 0, 0)),
            dspec, dspec,
        ],
        out_specs=pl.BlockSpec(blk, lambda j, c: (c, 0, 0, j)),
        out_shape=jax.ShapeDtypeStruct((NCHUNK, TCH, BATCH, BRANCH), _BF),
        scratch_shapes=[
            pltpu.VMEM((SLOTS, 2 * GW, GW), _BF),
            pltpu.VMEM((PAIRS_PER_J, MIX_R, 2 * GW), _BF),
            pltpu.VMEM((SLOTS, MIX_R, GW), _F32),
            pltpu.VMEM((PAIRS_PER_J, MIX_R, GW), _F32),
            pltpu.VMEM((PAIRS_PER_J, MIX_R, GW), _BF),
            pltpu.VMEM((PAIRS_PER_J, 2, BATCH, LANES), _F32),
        ],
        compiler_params=pltpu.CompilerParams(
            dimension_semantics=("parallel", "arbitrary"), vmem_limit_bytes=VMEM_LIMIT),
        name="s5_mix",
    )(u, kpad, mb, mc, jnp.asarray(_CPERM, _BF), dec_re, dec_im)


def _s5_out_kernel(y_ref, gate_ref, wglu_ref, bglu_ref, wout_ref, o_ref):
    rows = IN_MB * BATCH * TCH
    nb = IN_CT // IN_MB
    st = [dict() for _ in range(nb)]

    def act(i):
        st[i]["y"] = _gelu(y_ref[i * IN_MB:(i + 1) * IN_MB].reshape(rows, BRANCH).astype(_F32))

    def glu_mm(i):
        st[i]["glu"] = _dot(st[i]["y"].astype(_BF), wglu_ref[...])

    def half_glu(i):
        hy = 0.5 * st[i].pop("y")
        st[i]["y"] = hy + hy * jnp.tanh(0.5 * (st[i].pop("glu") + bglu_ref[...]))

    def gating(i):
        y = st[i].pop("y")
        sg = gate_ref[i * IN_MB:(i + 1) * IN_MB].reshape(rows, BRANCH).astype(_F32)
        z = (y * sg).astype(_BF).reshape(IN_MB, TCH, BATCH, BRANCH)
        parts = [jnp.swapaxes(z[k], 0, 1).reshape(BATCH * TCH, BRANCH) for k in range(IN_MB)]
        st[i]["z"] = parts[0] if IN_MB == 1 else jnp.concatenate(parts, axis=0)

    def out_mm(i):
        st[i]["out"] = _dot(st[i].pop("z"), wout_ref[...])

    def residual(i):
        out = st[i].pop("out").reshape(IN_MB, BATCH, TCH, D_MODEL)
        for k in range(IN_MB):
            c = i * IN_MB + k
            o_ref[:, c * TCH:(c + 1) * TCH, :] = out[k]

    _wavefront([act, glu_mm, half_glu, gating, out_mm, residual], nb)


def _s5_out(y, gate, w_glu, b_glu, w_out):
    grid = (NCHUNK // IN_CT,)
    blk4 = (IN_CT, TCH, BATCH, BRANCH)
    xblk = pl.BlockSpec((BATCH, IN_CT * TCH, D_MODEL), lambda i: (0, i, 0))
    return pl.pallas_call(
        _s5_out_kernel,
        grid=grid,
        in_specs=[
            pl.BlockSpec(blk4, lambda i: (i, 0, 0, 0)),
            pl.BlockSpec(blk4, lambda i: (i, 0, 0, 0)),
            pl.BlockSpec((BRANCH, BRANCH), lambda i: (0, 0)),
            pl.BlockSpec((1, BRANCH), lambda i: (0, 0)),
            pl.BlockSpec((BRANCH, D_MODEL), lambda i: (0, 0)),
        ],
        out_specs=xblk,
        out_shape=jax.ShapeDtypeStruct((BATCH, SEQ, D_MODEL), _F32),
        compiler_params=pltpu.CompilerParams(
            dimension_semantics=("parallel",), vmem_limit_bytes=VMEM_LIMIT),
        name="s5_out",
    )(y, gate, w_glu, b_glu, w_out)


SGU_TB = 1024
SGU_RB = 512
SGU_BPB = SGU_RB // SGU_BLOCK


def _sgu_kernel(x_ref, d_ref, g_ref, win_ref, lng_ref, lnb_ref, ws_ref, bs_ref, wout_ref, fg_ref, o_ref, mix_ref):
    blocks = [slice(r0, r0 + SGU_RB) for r0 in range(0, SGU_TB, SGU_RB)]
    st = [dict() for _ in blocks]

    def norm_in(i):
        st[i]["x1"] = x_ref[blocks[i], :] + d_ref[blocks[i], :]
        st[i]["h"] = _rms(st[i]["x1"], g_ref[...]).astype(_BF)

    def proj_v(i):
        st[i]["v"] = _dot(st[i]["h"], win_ref[:, BRANCH:2 * BRANCH])

    def proj_ug(i):
        st[i]["u"] = _dot(st[i]["h"], win_ref[:, :BRANCH])
        st[i]["gate"] = _dot(st[i]["h"], win_ref[:, 2 * BRANCH:])

    def layer_norm(i):
        v = _gelu(st[i].pop("v"))
        mu = jnp.mean(v, axis=-1, keepdims=True)
        vc = v - mu
        var = jnp.mean(vc * vc, axis=-1, keepdims=True)
        st[i]["vn"] = (vc * lax.rsqrt(var + LN_EPS) * lng_ref[...] + lnb_ref[...]).astype(_BF)

    def spatial(i):
        vn = st[i].pop("vn")
        r0 = blocks[i].start
        for hd in range(SGU_HEADS):
            lo = hd * SGU_HEAD_DIM
            rhs = jnp.concatenate(
                [vn[b * SGU_BLOCK:(b + 1) * SGU_BLOCK, lo:lo + SGU_HEAD_DIM] for b in range(SGU_BPB)], axis=1)
            res = _dot(ws_ref[hd], rhs)
            for b in range(SGU_BPB):
                mix_ref[r0 + b * SGU_BLOCK:r0 + (b + 1) * SGU_BLOCK, lo:lo + SGU_HEAD_DIM] = (
                    res[:, b * SGU_HEAD_DIM:(b + 1) * SGU_HEAD_DIM] + bs_ref[:, lo:lo + SGU_HEAD_DIM])

    def activate(i):
        st[i]["ug"] = _gelu(st[i].pop("u")) * _silu_tanh(st[i].pop("gate"))

    def gating(i):
        st[i]["z"] = (st[i].pop("ug") * mix_ref[blocks[i], :]).astype(_BF)

    def proj_out(i):
        st[i]["out"] = _dot(st[i].pop("z"), wout_ref[...])

    def norm_out(i):
        x2 = st[i].pop("x1") + st[i].pop("out")
        o_ref[blocks[i], :] = _rms(x2, fg_ref[...])

    _wavefront([norm_in, proj_v, proj_ug, layer_norm, activate, spatial, gating, proj_out, norm_out], len(blocks))


def _sgu_layer(x, delta, g, w_in, ln_g, ln_b, ws, bs, w_out, final_g):
    n = BATCH * SEQ
    grid = (n // SGU_TB,)
    row = lambda i: (0, 0)
    xblk = pl.BlockSpec((SGU_TB, D_MODEL), lambda i: (i, 0))
    return pl.pallas_call(
        _sgu_kernel,
        grid=grid,
        in_specs=[
            xblk,
            xblk,
            pl.BlockSpec((1, D_MODEL), row),
            pl.BlockSpec((D_MODEL, 3 * BRANCH), row),
            pl.BlockSpec((1, BRANCH), row),
            pl.BlockSpec((1, BRANCH), row),
            pl.BlockSpec((SGU_HEADS, SGU_BLOCK, SGU_BLOCK), lambda i: (0, 0, 0)),
            pl.BlockSpec((SGU_BLOCK, BRANCH), row),
            pl.BlockSpec((BRANCH, D_MODEL), row),
            pl.BlockSpec((1, D_MODEL), row),
        ],
        out_specs=xblk,
        out_shape=jax.ShapeDtypeStruct((n, D_MODEL), _F32),
        scratch_shapes=[pltpu.VMEM((SGU_TB, BRANCH), _F32)],
        compiler_params=pltpu.CompilerParams(
            dimension_semantics=("parallel",), vmem_limit_bytes=VMEM_LIMIT),
        name="sgu_layer",
    )(x, delta, g, w_in, ln_g, ln_b, ws, bs, w_out, final_g)


def kernel(x, norm_g, final_g, s5_w_in, s5_A_re, s5_A_im, s5_log_dt, s5_B_re, s5_B_im, s5_C_re, s5_C_im, s5_D, s5_w_glu, s5_b_glu, s5_w_out, sgu_w_in, sgu_ln_g, sgu_ln_b, sgu_w_s, sgu_b_s, sgu_w_out):
    kpad, mb, mc, dec_re, dec_im = _s5_operators(
        s5_A_re[0], s5_A_im[0], s5_log_dt[0], s5_B_re[0], s5_B_im[0], s5_C_re[0], s5_C_im[0], s5_D[0])
    u, gate = _s5_in(x, norm_g[0][None, :], s5_w_in[0].astype(_BF))
    y = _s5_mix(u, kpad, mb, mc, dec_re, dec_im)
    delta = _s5_out(y, gate, s5_w_glu[0].astype(_BF), s5_b_glu[0][None, :], s5_w_out[0].astype(_BF))

    mask = jnp.tril(jnp.ones((SGU_BLOCK, SGU_BLOCK), dtype=bool))
    ws = jnp.where(mask[None], sgu_w_s[0], 0.0).astype(_BF)
    bs = jnp.repeat(jnp.transpose(sgu_b_s[0]), SGU_HEAD_DIM, axis=1)
    out = _sgu_layer(x.reshape(BATCH * SEQ, D_MODEL), delta.reshape(BATCH * SEQ, D_MODEL),
                     norm_g[1][None, :], sgu_w_in[0].astype(_BF),
                     sgu_ln_g[0][None, :], sgu_ln_b[0][None, :], ws, bs,
                     sgu_w_out[0].astype(_BF), final_g[None, :])
    return out.reshape(BATCH, SEQ, D_MODEL)
```

```python
import math

import numpy as np
import jax
import jax.numpy as jnp
from jax import lax
from jax.experimental import pallas as pl
from jax.experimental.pallas import tpu as pltpu

D_MODEL = 1024
BATCH = 16
SEQ = 2048
BRANCH = D_MODEL
S5_GROUP = 16
S5_GROUPS = BRANCH // S5_GROUP
S5_STATE = 64
SGU_BLOCK = 128
SGU_HEADS = 8
SGU_HEAD_DIM = BRANCH // SGU_HEADS
RMS_EPS = 1e-6
LN_EPS = 1e-5
RE_CLIP = -1e-4

LANES = 128
TCH = 16
NCHUNK = SEQ // TCH
SLOTS = LANES // S5_GROUP
NJ = BRANCH // LANES
GW = TCH * S5_GROUP
PAIRS_PER_J = SLOTS // 2
VMEM_LIMIT = 56 * 1024 * 1024

_HI = lax.Precision.HIGHEST
_BF = jnp.bfloat16
_F32 = jnp.float32


_GELU_C1 = math.sqrt(2.0 / math.pi)
_GELU_C2 = _GELU_C1 * 0.044715


def _gelu(x):
    inner = x * (_GELU_C1 + _GELU_C2 * (x * x))
    return (0.5 * x) * (1.0 + jnp.tanh(inner))


def _sigmoid(x):
    return jax.nn.sigmoid(x)


def _silu_tanh(x):
    h = 0.5 * x
    return h + h * jnp.tanh(h)


def _dot(a, b):
    return jnp.dot(a, b, preferred_element_type=_F32)


def _wavefront(stages, n):
    for t in range(len(stages) + n - 1):
        for i in range(n):
            k = t - i
            if 0 <= k < len(stages):
                stages[k](i)


def _rms(x, g):
    return x * lax.rsqrt(jnp.mean(x * x, axis=-1, keepdims=True) + RMS_EPS) * g


KPAD = 2 * GW


def _slot_frame_table():
    a = np.arange(SLOTS)[:, None]
    x = np.arange(TCH)[None, :]
    half, k = x // SLOTS, x % SLOTS
    return SLOTS * half + (k - a) % SLOTS


_FRAME_OF_SLOT = _slot_frame_table()


def _s5_ops_kernel(lre_ref, lim_ref, ldr_ref, ldi_ref, btr_ref, bti_ref, cr_ref, ci_ref, d_ref,
                   kpad_ref, mb_ref, mc_ref, dre_ref, dim_ref):
    P = S5_STATE
    lane = lax.broadcasted_iota(jnp.int32, (S5_GROUP, LANES), 1)
    kcol = lax.broadcasted_iota(jnp.int32, (TCH, 1), 0)
    lane256 = lax.broadcasted_iota(jnp.int32, (S5_GROUP, GW), 1)
    row256 = lax.broadcasted_iota(jnp.int32, (S5_GROUP, GW), 0)
    lane_gw = lax.broadcasted_iota(jnp.int32, (GW, LANES), 1)
    nt = (((1,), (1,)), ((), ()))
    zeros64 = jnp.zeros((P, GW), _F32)
    st = [dict() for _ in range(SLOTS)]

    def outer(t_re, t_im, m_re, m_im):
        re = [t_re[k:k + 1] * m_re - t_im[k:k + 1] * m_im for k in range(TCH)]
        im = [t_re[k:k + 1] * m_im + t_im[k:k + 1] * m_re for k in range(TCH)]
        return jnp.concatenate(re, axis=0), jnp.concatenate(im, axis=0)

    def tables(a):
        lam_re, lam_im = lre_ref[a:a + 1, :], lim_ref[a:a + 1, :]
        ldr, ldi = ldr_ref[a:a + 1, :], ldi_ref[a:a + 1, :]

        def power(k):
            mag = jnp.exp(k * ldr)
            return mag * jnp.cos(k * ldi), mag * jnp.sin(k * ldi)

        frame = (kcol & SLOTS) + (((kcol & (SLOTS - 1)) + (SLOTS - a)) & (SLOTS - 1))
        pw_re, pw_im = power(kcol.astype(_F32))
        pi_re, pi_im = power((TCH - 1 - frame).astype(_F32))
        po_re, po_im = power((frame + 1).astype(_F32))
        st[a]["p16"] = power(jnp.full((1, 1), float(TCH), _F32))

        nr, ni = pw_re[1:2] - 1.0, pw_im[1:2]
        den = lam_re * lam_re + lam_im * lam_im
        cf_re = (nr * lam_re + ni * lam_im) / den
        cf_im = (ni * lam_re - nr * lam_im) / den
        b_re, b_im = btr_ref[a], bti_ref[a]
        bb_re = cf_re * b_re - cf_im * b_im
        bb_im = cf_re * b_im + cf_im * b_re
        c_re, c_im = cr_ref[a], ci_ref[a]
        st[a]["bb"] = (bb_re, bb_im)
        st[a]["lc"] = outer(pw_re, pw_im, c_re, c_im)
        st[a]["in"] = outer(pi_re, pi_im, bb_re, bb_im)
        st[a]["ot"] = outer(po_re, po_im, c_re, c_im)

    def taps(a):
        bb_re, bb_im = st[a].pop("bb")
        lc_re, lc_im = st[a].pop("lc")
        once = lane < P
        k = (lax.dot_general(jnp.where(once, bb_re, 0.0), lc_re, nt, precision=_HI, preferred_element_type=_F32)
             - lax.dot_general(jnp.where(once, bb_im, 0.0), lc_im, nt, precision=_HI, preferred_element_type=_F32))
        kpad_ref[a, :, 0:GW] = jnp.zeros((S5_GROUP, GW), _F32)
        kpad_ref[a, :, GW:2 * GW] = k + jnp.where(lane256 == row256, d_ref[a], 0.0)

    def state_in(a):
        q, m = a // 2, a % 2
        mine = (lane_gw >= P) if m else (lane_gw < P)
        in_re, in_im = st[a].pop("in")
        mb_ref[q, m * GW:(m + 1) * GW, 0:LANES] = jnp.where(mine, in_re, 0.0).astype(_BF)
        mb_ref[q, m * GW:(m + 1) * GW, LANES:2 * LANES] = jnp.where(mine, in_im, 0.0).astype(_BF)

    def state_out(a):
        q, m = a // 2, a % 2
        ot_re, ot_im = st[a].pop("ot")
        o_re, o_im = ot_re.T[0:P], ot_im.T[0:P]
        blocks = [zeros64, zeros64, zeros64, zeros64]
        blocks[m], blocks[2 + m] = o_re, -o_im
        mc_ref[a] = jnp.concatenate(blocks, axis=0).astype(_BF)
        if m:
            first = lax.broadcasted_iota(jnp.int32, (1, LANES), 1) < P
            dre_ref[q] = jnp.where(first, st[a - 1]["p16"][0], st[a]["p16"][0])
            dim_ref[q] = jnp.where(first, st[a - 1]["p16"][1], st[a]["p16"][1])

    _wavefront([tables, taps, state_in, state_out], SLOTS)


def _s5_operators(A_re, A_im, log_dt, B_re, B_im, C_re, C_im, D):
    f32 = jnp.float32
    G, P, H = S5_GROUPS, S5_STATE, S5_GROUP
    lam_re = jnp.minimum(A_re.astype(f32), RE_CLIP)
    lam_im = A_im.astype(f32)
    dt = jnp.exp(log_dt.astype(f32))[:, None]

    twice = lambda t: jnp.concatenate([t, t], axis=-1)
    tables = [twice(lam_re), twice(lam_im), twice(lam_re * dt), twice(lam_im * dt)]
    mats = [twice(jnp.swapaxes(B_re.astype(f32), 1, 2)), twice(jnp.swapaxes(B_im.astype(f32), 1, 2)),
            twice(C_re.astype(f32)), twice(C_im.astype(f32))]
    tspec = pl.BlockSpec((SLOTS, LANES), lambda j: (j, 0))
    mspec = pl.BlockSpec((SLOTS, H, LANES), lambda j: (j, 0, 0))
    dspec = pl.BlockSpec((PAIRS_PER_J, 1, LANES), lambda j: (j, 0, 0))
    return pl.pallas_call(
        _s5_ops_kernel,
        grid=(NJ,),
        in_specs=[tspec] * 4 + [mspec] * 4 + [pl.BlockSpec((SLOTS, H, 1), lambda j: (j, 0, 0))],
        out_specs=[
            pl.BlockSpec((SLOTS, H, KPAD), lambda j: (j, 0, 0)),
            pl.BlockSpec((PAIRS_PER_J, 2 * GW, GW), lambda j: (j, 0, 0)),
            pl.BlockSpec((SLOTS, GW, GW), lambda j: (j, 0, 0)),
            dspec, dspec,
        ],
        out_shape=[
            jax.ShapeDtypeStruct((G, H, KPAD), f32),
            jax.ShapeDtypeStruct((G // 2, 2 * GW, GW), _BF),
            jax.ShapeDtypeStruct((G, GW, GW), _BF),
            jax.ShapeDtypeStruct((G // 2, 1, LANES), f32),
            jax.ShapeDtypeStruct((G // 2, 1, LANES), f32),
        ],
        compiler_params=pltpu.CompilerParams(dimension_semantics=("parallel",), vmem_limit_bytes=VMEM_LIMIT),
        name="s5_ops",
    )(*tables, *mats, D.astype(f32).reshape(G, H, 1))


IN_CT = 8
IN_MB = 1


def _s5_in_kernel(x_ref, g_ref, w_ref, u_ref, gate_ref):
    rows = BATCH * TCH
    nb = IN_CT // IN_MB
    st = [dict() for _ in range(nb)]

    def norm(i):
        parts = []
        for c in range(i * IN_MB, (i + 1) * IN_MB):
            hc = _rms(x_ref[:, c * TCH:(c + 1) * TCH, :], g_ref[...]).astype(_BF)
            parts.append(jnp.swapaxes(hc, 0, 1).reshape(rows, D_MODEL))
        st[i]["hp"] = jnp.concatenate(parts, axis=0)

    def gate_mm(i):
        gate = _dot(st[i]["hp"], w_ref[:, BRANCH:])
        gate_ref[i * IN_MB:(i + 1) * IN_MB] = (
            _silu_tanh(gate).astype(_BF).reshape(IN_MB, TCH, BATCH, BRANCH))

    def u_mm(i):
        u = _dot(st[i].pop("hp"), w_ref[:, :BRANCH])
        u_ref[i * IN_MB:(i + 1) * IN_MB] = u.astype(_BF).reshape(IN_MB, TCH, BATCH, BRANCH)

    _wavefront([norm, gate_mm, u_mm], nb)


def _s5_in(x, g, w_in):
    grid = (NCHUNK // IN_CT,)
    blk4 = (IN_CT, TCH, BATCH, BRANCH)
    return pl.pallas_call(
        _s5_in_kernel,
        grid=grid,
        in_specs=[
            pl.BlockSpec((BATCH, IN_CT * TCH, D_MODEL), lambda i: (0, i, 0)),
            pl.BlockSpec((1, D_MODEL), lambda i: (0, 0)),
            pl.BlockSpec((D_MODEL, 2 * BRANCH), lambda i: (0, 0)),
        ],
        out_specs=[
            pl.BlockSpec(blk4, lambda i: (i, 0, 0, 0)),
            pl.BlockSpec(blk4, lambda i: (i, 0, 0, 0)),
        ],
        out_shape=[
            jax.ShapeDtypeStruct((NCHUNK, TCH, BATCH, BRANCH), _BF),
            jax.ShapeDtypeStruct((NCHUNK, TCH, BATCH, BRANCH), _BF),
        ],
        compiler_params=pltpu.CompilerParams(
            dimension_semantics=("parallel",), vmem_limit_bytes=VMEM_LIMIT),
        name="s5_in",
    )(x, g, w_in)


MIX_CT = 64
MIX_R = MIX_CT * BATCH
MIX_CB = 4
MIX_RT = MIX_CB * BATCH
MIX_NH = 4
MIX_HC = MIX_CT // MIX_NH
MIX_HR = MIX_HC * BATCH


def _gather_slots(v, slot):
    rolled = [v[0]] + [pltpu.roll(v[k], k * S5_GROUP, axis=1) for k in range(1, SLOTS)]
    out = []
    for a in range(SLOTS):
        acc = rolled[0]
        for k in range(1, SLOTS):
            acc = jnp.where(slot == (a + k) % SLOTS, rolled[k], acc)
        out.append(acc)
    return out


def _scatter_slots(v, slot):
    out = []
    for k in range(SLOTS):
        acc = v[0]
        for a in range(1, SLOTS):
            acc = jnp.where(slot == (a + k) % SLOTS, v[a], acc)
        out.append(acc if k == 0 else pltpu.roll(acc, (SLOTS - k) * S5_GROUP, axis=1))
    return out


def _column_perms():
    p = np.zeros((SLOTS, GW, GW), np.float32)
    for a in range(SLOTS):
        for x in range(TCH):
            f = int(_FRAME_OF_SLOT[a, x])
            for c in range(S5_GROUP):
                p[a, f * S5_GROUP + c, x * S5_GROUP + c] = 1.0
    return p


_CPERM = _column_perms()


def _s5_mix_kernel(u_ref, kpad_ref, mb_ref, mc_ref, cperm_ref, dre_ref, dim_ref, y_ref,
                   op_ref, xs_ref, ys_ref, ss_ref, hs_ref, carry_ref):
    ct = pl.program_id(1)

    @pl.when(ct == 0)
    def _():
        carry_ref[...] = jnp.zeros_like(carry_ref)
        for a in range(SLOTS):
            kp = kpad_ref[a]
            rows = []
            for x in range(TCH):
                off = (TCH - int(_FRAME_OF_SLOT[a, x])) * S5_GROUP
                rows.append(kp[:, off:off + GW])
            toep = jnp.concatenate(rows, axis=0).astype(_BF)
            op_ref[a, 0:GW, :] = _dot(toep, cperm_ref[a]).astype(_BF)
            op_ref[a, GW:2 * GW, :] = mc_ref[a]

    slot = lax.broadcasted_iota(jnp.int32, (MIX_RT, LANES), 1) // S5_GROUP
    dre = [dre_ref[q] for q in range(PAIRS_PER_J)]
    dim_ = [dim_ref[q] for q in range(PAIRS_PER_J)]

    def relayout_in(c0):
        r0 = c0 * BATCH
        for half in range(2):
            pieces = [u_ref[c0:c0 + MIX_CB, SLOTS * half + k, :, :].reshape(MIX_RT, LANES) for k in range(SLOTS)]
            groups = _gather_slots(pieces, slot)
            for a in range(SLOTS):
                lo = (a % 2) * GW + half * LANES
                xs_ref[a // 2, r0:r0 + MIX_RT, lo:lo + LANES] = groups[a]

    def relayout_out(c0):
        r0 = c0 * BATCH
        for half in range(2):
            groups = [ys_ref[a, r0:r0 + MIX_RT, half * LANES:(half + 1) * LANES].astype(_BF) for a in range(SLOTS)]
            frames = _scatter_slots(groups, slot)
            for k in range(SLOTS):
                y_ref[c0:c0 + MIX_CB, SLOTS * half + k, :, :] = frames[k].reshape(MIX_CB, BATCH, LANES)

    def recurrence(c0, state):
        for c in range(c0, c0 + MIX_HC):
            r0 = c * BATCH
            new = []
            for q in range(PAIRS_PER_J):
                hre, him = state[2 * q], state[2 * q + 1]
                hs_ref[q, r0:r0 + BATCH, 0:LANES] = hre.astype(_BF)
                hs_ref[q, r0:r0 + BATCH, LANES:2 * LANES] = him.astype(_BF)
                sre = ss_ref[q, r0:r0 + BATCH, 0:LANES]
                sim = ss_ref[q, r0:r0 + BATCH, LANES:2 * LANES]
                new.append(dre[q] * hre - dim_[q] * him + sre)
                new.append(dre[q] * him + dim_[q] * hre + sim)
            state = new
        return state

    blocks = [(h * MIX_HC, h * MIX_HC * BATCH) for h in range(MIX_NH)]
    for c0, r0 in blocks:
        for t in range(MIX_HC // MIX_CB):
            relayout_in(c0 + t * MIX_CB)
        for q in range(PAIRS_PER_J):
            ss_ref[q, r0:r0 + MIX_HR, :] = _dot(xs_ref[q, r0:r0 + MIX_HR, :], mb_ref[q])
    state = [carry_ref[q, k] for q in range(PAIRS_PER_J) for k in range(2)]
    for c0, r0 in blocks:
        state = recurrence(c0, state)
    for q in range(PAIRS_PER_J):
        carry_ref[q, 0] = state[2 * q]
        carry_ref[q, 1] = state[2 * q + 1]
    for c0, r0 in blocks:
        for a in range(SLOTS):
            q, m = a // 2, a % 2
            lhs = jnp.concatenate([xs_ref[q, r0:r0 + MIX_HR, m * GW:(m + 1) * GW], hs_ref[q, r0:r0 + MIX_HR, :]], axis=1)
            ys_ref[a, r0:r0 + MIX_HR, :] = _dot(lhs, op_ref[a])
    for c0, r0 in blocks:
        for t in range(MIX_HC // MIX_CB):
            relayout_out(c0 + t * MIX_CB)


def _s5_mix(u, kpad, mb, mc, dec_re, dec_im):
    grid = (NJ, NCHUNK // MIX_CT)
    blk = (MIX_CT, TCH, BATCH, LANES)
    dspec = pl.BlockSpec((PAIRS_PER_J, 1, LANES), lambda j, c: (j, 0, 0))
    return pl.pallas_call(
        _s5_mix_kernel,
        grid=grid,
        in_specs=[
            pl.BlockSpec(blk, lambda j, c: (c, 0, 0, j)),
            pl.BlockSpec((SLOTS, S5_GROUP, KPAD), lambda j, c: (j, 0, 0)),
            pl.BlockSpec((PAIRS_PER_J, 2 * GW, GW), lambda j, c: (j, 0, 0)),
            pl.BlockSpec((SLOTS, GW, GW), lambda j, c: (j, 0, 0)),
            pl.BlockSpec((SLOTS, GW, GW), lambda j, c: (0, 0, 0)),
            dspec, dspec,
        ],
        out_specs=pl.BlockSpec(blk, lambda j, c: (c, 0, 0, j)),
        out_shape=jax.ShapeDtypeStruct((NCHUNK, TCH, BATCH, BRANCH), _BF),
        scratch_shapes=[
            pltpu.VMEM((SLOTS, 2 * GW, GW), _BF),
            pltpu.VMEM((PAIRS_PER_J, MIX_R, 2 * GW), _BF),
            pltpu.VMEM((SLOTS, MIX_R, GW), _F32),
            pltpu.VMEM((PAIRS_PER_J, MIX_R, GW), _F32),
            pltpu.VMEM((PAIRS_PER_J, MIX_R, GW), _BF),
            pltpu.VMEM((PAIRS_PER_J, 2, BATCH, LANES), _F32),
        ],
        compiler_params=pltpu.CompilerParams(
            dimension_semantics=("parallel", "arbitrary"), vmem_limit_bytes=VMEM_LIMIT),
        name="s5_mix",
    )(u, kpad, mb, mc, jnp.asarray(_CPERM, _BF), dec_re, dec_im)


def _s5_out_kernel(y_ref, gate_ref, wglu_ref, bglu_ref, wout_ref, o_ref):
    rows = IN_MB * BATCH * TCH
    nb = IN_CT // IN_MB
    st = [dict() for _ in range(nb)]

    def act(i):
        st[i]["y"] = _gelu(y_ref[i * IN_MB:(i + 1) * IN_MB].reshape(rows, BRANCH).astype(_F32))

    def glu_mm(i):
        st[i]["glu"] = _dot(st[i]["y"].astype(_BF), wglu_ref[...])

    def half_glu(i):
        hy = 0.5 * st[i].pop("y")
        st[i]["y"] = hy + hy * jnp.tanh(0.5 * (st[i].pop("glu") + bglu_ref[...]))

    def gating(i):
        y = st[i].pop("y")
        sg = gate_ref[i * IN_MB:(i + 1) * IN_MB].reshape(rows, BRANCH).astype(_F32)
        z = (y * sg).astype(_BF).reshape(IN_MB, TCH, BATCH, BRANCH)
        parts = [jnp.swapaxes(z[k], 0, 1).reshape(BATCH * TCH, BRANCH) for k in range(IN_MB)]
        st[i]["z"] = parts[0] if IN_MB == 1 else jnp.concatenate(parts, axis=0)

    def out_mm(i):
        st[i]["out"] = _dot(st[i].pop("z"), wout_ref[...])

    def residual(i):
        out = st[i].pop("out").reshape(IN_MB, BATCH, TCH, D_MODEL)
        for k in range(IN_MB):
            c = i * IN_MB + k
            o_ref[:, c * TCH:(c + 1) * TCH, :] = out[k]

    _wavefront([act, glu_mm, half_glu, gating, out_mm, residual], nb)


def _s5_out(y, gate, w_glu, b_glu, w_out):
    grid = (NCHUNK // IN_CT,)
    blk4 = (IN_CT, TCH, BATCH, BRANCH)
    xblk = pl.BlockSpec((BATCH, IN_CT * TCH, D_MODEL), lambda i: (0, i, 0))
    return pl.pallas_call(
        _s5_out_kernel,
        grid=grid,
        in_specs=[
            pl.BlockSpec(blk4, lambda i: (i, 0, 0, 0)),
            pl.BlockSpec(blk4, lambda i: (i, 0, 0, 0)),
            pl.BlockSpec((BRANCH, BRANCH), lambda i: (0, 0)),
            pl.BlockSpec((1, BRANCH), lambda i: (0, 0)),
            pl.BlockSpec((BRANCH, D_MODEL), lambda i: (0, 0)),
        ],
        out_specs=xblk,
        out_shape=jax.ShapeDtypeStruct((BATCH, SEQ, D_MODEL), _F32),
        compiler_params=pltpu.CompilerParams(
            dimension_semantics=("parallel",), vmem_limit_bytes=VMEM_LIMIT),
        name="s5_out",
    )(y, gate, w_glu, b_glu, w_out)


SGU_TB = 1024
SGU_RB = 512
SGU_BPB = SGU_RB // SGU_BLOCK


def _sgu_kernel(x_ref, d_ref, g_ref, win_ref, lng_ref, lnb_ref, ws_ref, bs_ref, wout_ref, fg_ref, o_ref, mix_ref):
    blocks = [slice(r0, r0 + SGU_RB) for r0 in range(0, SGU_TB, SGU_RB)]
    st = [dict() for _ in blocks]

    def norm_in(i):
        st[i]["x1"] = x_ref[blocks[i], :] + d_ref[blocks[i], :]
        st[i]["h"] = _rms(st[i]["x1"], g_ref[...]).astype(_BF)

    def proj_v(i):
        st[i]["v"] = _dot(st[i]["h"], win_ref[:, BRANCH:2 * BRANCH])

    def proj_ug(i):
        st[i]["u"] = _dot(st[i]["h"], win_ref[:, :BRANCH])
        st[i]["gate"] = _dot(st[i]["h"], win_ref[:, 2 * BRANCH:])

    def layer_norm(i):
        v = _gelu(st[i].pop("v"))
        mu = jnp.mean(v, axis=-1, keepdims=True)
        vc = v - mu
        var = jnp.mean(vc * vc, axis=-1, keepdims=True)
        st[i]["vn"] = (vc * lax.rsqrt(var + LN_EPS) * lng_ref[...] + lnb_ref[...]).astype(_BF)

    def spatial(i):
        vn = st[i].pop("vn")
        r0 = blocks[i].start
        for hd in range(SGU_HEADS):
            lo = hd * SGU_HEAD_DIM
            rhs = jnp.concatenate(
                [vn[b * SGU_BLOCK:(b + 1) * SGU_BLOCK, lo:lo + SGU_HEAD_DIM] for b in range(SGU_BPB)], axis=1)
            res = _dot(ws_ref[hd], rhs)
            for b in range(SGU_BPB):
                mix_ref[r0 + b * SGU_BLOCK:r0 + (b + 1) * SGU_BLOCK, lo:lo + SGU_HEAD_DIM] = (
                    res[:, b * SGU_HEAD_DIM:(b + 1) * SGU_HEAD_DIM] + bs_ref[:, lo:lo + SGU_HEAD_DIM])

    def activate(i):
        st[i]["ug"] = _gelu(st[i].pop("u")) * _silu_tanh(st[i].pop("gate"))

    def gating(i):
        st[i]["z"] = (st[i].pop("ug") * mix_ref[blocks[i], :]).astype(_BF)

    def proj_out(i):
        st[i]["out"] = _dot(st[i].pop("z"), wout_ref[...])

    def norm_out(i):
        x2 = st[i].pop("x1") + st[i].pop("out")
        o_ref[blocks[i], :] = _rms(x2, fg_ref[...])

    _wavefront([norm_in, proj_v, proj_ug, layer_norm, activate, spatial, gating, proj_out, norm_out], len(blocks))


def _sgu_layer(x, delta, g, w_in, ln_g, ln_b, ws, bs, w_out, final_g):
    n = BATCH * SEQ
    grid = (n // SGU_TB,)
    row = lambda i: (0, 0)
    xblk = pl.BlockSpec((SGU_TB, D_MODEL), lambda i: (i, 0))
    return pl.pallas_call(
        _sgu_kernel,
        grid=grid,
        in_specs=[
            xblk,
            xblk,
            pl.BlockSpec((1, D_MODEL), row),
            pl.BlockSpec((D_MODEL, 3 * BRANCH), row),
            pl.BlockSpec((1, BRANCH), row),
            pl.BlockSpec((1, BRANCH), row),
            pl.BlockSpec((SGU_HEADS, SGU_BLOCK, SGU_BLOCK), lambda i: (0, 0, 0)),
            pl.BlockSpec((SGU_BLOCK, BRANCH), row),
            pl.BlockSpec((BRANCH, D_MODEL), row),
            pl.BlockSpec((1, D_MODEL), row),
        ],
        out_specs=xblk,
        out_shape=jax.ShapeDtypeStruct((n, D_MODEL), _F32),
        scratch_shapes=[pltpu.VMEM((SGU_TB, BRANCH), _F32)],
        compiler_params=pltpu.CompilerParams(
            dimension_semantics=("parallel",), vmem_limit_bytes=VMEM_LIMIT),
        name="sgu_layer",
    )(x, delta, g, w_in, ln_g, ln_b, ws, bs, w_out, final_g)


def kernel(x, norm_g, final_g, s5_w_in, s5_A_re, s5_A_im, s5_log_dt, s5_B_re, s5_B_im, s5_C_re, s5_C_im, s5_D, s5_w_glu, s5_b_glu, s5_w_out, sgu_w_in, sgu_ln_g, sgu_ln_b, sgu_w_s, sgu_b_s, sgu_w_out):
    kpad, mb, mc, dec_re, dec_im = _s5_operators(
        s5_A_re[0], s5_A_im[0], s5_log_dt[0], s5_B_re[0], s5_B_im[0], s5_C_re[0], s5_C_im[0], s5_D[0])
    u, gate = _s5_in(x, norm_g[0][None, :], s5_w_in[0].astype(_BF))
    y = _s5_mix(u, kpad, mb, mc, dec_re, dec_im)
    delta = _s5_out(y, gate, s5_w_glu[0].astype(_BF), s5_b_glu[0][None, :], s5_w_out[0].astype(_BF))

    mask = jnp.tril(jnp.ones((SGU_BLOCK, SGU_BLOCK), dtype=bool))
    ws = jnp.where(mask[None], sgu_w_s[0], 0.0).astype(_BF)
    bs = jnp.repeat(jnp.transpose(sgu_b_s[0]), SGU_HEAD_DIM, axis=1)
    out = _sgu_layer(x.reshape(BATCH * SEQ, D_MODEL), delta.reshape(BATCH * SEQ, D_MODEL),
                     norm_g[1][None, :], sgu_w_in[0].astype(_BF),
                     sgu_ln_g[0][None, :], sgu_ln_b[0][None, :], ws, bs,
                     sgu_w_out[0].astype(_BF), final_g[None, :])
    return out.reshape(BATCH, SEQ, D_MODEL)
```

```python
import math

import numpy as np
import jax
import jax.numpy as jnp
from jax import lax
from jax.experimental import pallas as pl
from jax.experimental.pallas import tpu as pltpu

D_MODEL = 1024
BATCH = 16
SEQ = 2048
BRANCH = D_MODEL
S5_GROUP = 16
S5_GROUPS = BRANCH // S5_GROUP
S5_STATE = 64
SGU_BLOCK = 128
SGU_HEADS = 8
SGU_HEAD_DIM = BRANCH // SGU_HEADS
RMS_EPS = 1e-6
LN_EPS = 1e-5
RE_CLIP = -1e-4

LANES = 128
TCH = 16
NCHUNK = SEQ // TCH
SLOTS = LANES // S5_GROUP
NJ = BRANCH // LANES
GW = TCH * S5_GROUP
PAIRS_PER_J = SLOTS // 2
VMEM_LIMIT = 56 * 1024 * 1024

_HI = lax.Precision.HIGHEST
_BF = jnp.bfloat16
_F32 = jnp.float32


_GELU_C1 = math.sqrt(2.0 / math.pi)
_GELU_C2 = _GELU_C1 * 0.044715


def _gelu(x):
    inner = x * (_GELU_C1 + _GELU_C2 * (x * x))
    return (0.5 * x) * (1.0 + jnp.tanh(inner))


def _sigmoid(x):
    return jax.nn.sigmoid(x)


def _silu_tanh(x):
    h = 0.5 * x
    return h + h * jnp.tanh(h)


def _dot(a, b):
    return jnp.dot(a, b, preferred_element_type=_F32)


def _wavefront(stages, n):
    for t in range(len(stages) + n - 1):
        for i in range(n):
            k = t - i
            if 0 <= k < len(stages):
                stages[k](i)


def _rms(x, g):
    return x * lax.rsqrt(jnp.mean(x * x, axis=-1, keepdims=True) + RMS_EPS) * g


KPAD = 2 * GW


def _slot_frame_table():
    a = np.arange(SLOTS)[:, None]
    x = np.arange(TCH)[None, :]
    half, k = x // SLOTS, x % SLOTS
    return SLOTS * half + (k - a) % SLOTS


_FRAME_OF_SLOT = _slot_frame_table()


def _s5_ops_kernel(lre_ref, lim_ref, ldr_ref, ldi_ref, btr_ref, bti_ref, cr_ref, ci_ref, d_ref,
                   kpad_ref, mb_ref, mc_ref, dre_ref, dim_ref):
    P = S5_STATE
    lane = lax.broadcasted_iota(jnp.int32, (S5_GROUP, LANES), 1)
    kcol = lax.broadcasted_iota(jnp.int32, (TCH, 1), 0)
    lane256 = lax.broadcasted_iota(jnp.int32, (S5_GROUP, GW), 1)
    row256 = lax.broadcasted_iota(jnp.int32, (S5_GROUP, GW), 0)
    lane_gw = lax.broadcasted_iota(jnp.int32, (GW, LANES), 1)
    nt = (((1,), (1,)), ((), ()))
    zeros64 = jnp.zeros((P, GW), _F32)
    st = [dict() for _ in range(SLOTS)]

    def outer(t_re, t_im, m_re, m_im):
        re = [t_re[k:k + 1] * m_re - t_im[k:k + 1] * m_im for k in range(TCH)]
        im = [t_re[k:k + 1] * m_im + t_im[k:k + 1] * m_re for k in range(TCH)]
        return jnp.concatenate(re, axis=0), jnp.concatenate(im, axis=0)

    def tables(a):
        lam_re, lam_im = lre_ref[a:a + 1, :], lim_ref[a:a + 1, :]
        ldr, ldi = ldr_ref[a:a + 1, :], ldi_ref[a:a + 1, :]

        def power(k):
            mag = jnp.exp(k * ldr)
            return mag * jnp.cos(k * ldi), mag * jnp.sin(k * ldi)

        frame = (kcol & SLOTS) + (((kcol & (SLOTS - 1)) + (SLOTS - a)) & (SLOTS - 1))
        pw_re, pw_im = power(kcol.astype(_F32))
        pi_re, pi_im = power((TCH - 1 - frame).astype(_F32))
        po_re, po_im = power((frame + 1).astype(_F32))
        st[a]["p16"] = power(jnp.full((1, 1), float(TCH), _F32))

        nr, ni = pw_re[1:2] - 1.0, pw_im[1:2]
        den = lam_re * lam_re + lam_im * lam_im
        cf_re = (nr * lam_re + ni * lam_im) / den
        cf_im = (ni * lam_re - nr * lam_im) / den
        b_re, b_im = btr_ref[a], bti_ref[a]
        bb_re = cf_re * b_re - cf_im * b_im
        bb_im = cf_re * b_im + cf_im * b_re
        c_re, c_im = cr_ref[a], ci_ref[a]
        st[a]["bb"] = (bb_re, bb_im)
        st[a]["lc"] = outer(pw_re, pw_im, c_re, c_im)
        st[a]["in"] = outer(pi_re, pi_im, bb_re, bb_im)
        st[a]["ot"] = outer(po_re, po_im, c_re, c_im)

    def taps(a):
        bb_re, bb_im = st[a].pop("bb")
        lc_re, lc_im = st[a].pop("lc")
        once = lane < P
        k = (lax.dot_general(jnp.where(once, bb_re, 0.0), lc_re, nt, precision=_HI, preferred_element_type=_F32)
             - lax.dot_general(jnp.where(once, bb_im, 0.0), lc_im, nt, precision=_HI, preferred_element_type=_F32))
        kpad_ref[a, :, 0:GW] = jnp.zeros((S5_GROUP, GW), _F32)
        kpad_ref[a, :, GW:2 * GW] = k + jnp.where(lane256 == row256, d_ref[a], 0.0)

    def state_in(a):
        q, m = a // 2, a % 2
        mine = (lane_gw >= P) if m else (lane_gw < P)
        in_re, in_im = st[a].pop("in")
        mb_ref[q, m * GW:(m + 1) * GW, 0:LANES] = jnp.where(mine, in_re, 0.0).astype(_BF)
        mb_ref[q, m * GW:(m + 1) * GW, LANES:2 * LANES] = jnp.where(mine, in_im, 0.0).astype(_BF)

    def state_out(a):
        q, m = a // 2, a % 2
        ot_re, ot_im = st[a].pop("ot")
        o_re, o_im = ot_re.T[0:P], ot_im.T[0:P]
        blocks = [zeros64, zeros64, zeros64, zeros64]
        blocks[m], blocks[2 + m] = o_re, -o_im
        mc_ref[a] = jnp.concatenate(blocks, axis=0).astype(_BF)
        if m:
            first = lax.broadcasted_iota(jnp.int32, (1, LANES), 1) < P
            dre_ref[q] = jnp.where(first, st[a - 1]["p16"][0], st[a]["p16"][0])
            dim_ref[q] = jnp.where(first, st[a - 1]["p16"][1], st[a]["p16"][1])

    _wavefront([tables, taps, state_in, state_out], SLOTS)


def _s5_operators(A_re, A_im, log_dt, B_re, B_im, C_re, C_im, D):
    f32 = jnp.float32
    G, P, H = S5_GROUPS, S5_STATE, S5_GROUP
    lam_re = jnp.minimum(A_re.astype(f32), RE_CLIP)
    lam_im = A_im.astype(f32)
    dt = jnp.exp(log_dt.astype(f32))[:, None]

    twice = lambda t: jnp.concatenate([t, t], axis=-1)
    tables = [twice(lam_re), twice(lam_im), twice(lam_re * dt), twice(lam_im * dt)]
    mats = [twice(jnp.swapaxes(B_re.astype(f32), 1, 2)), twice(jnp.swapaxes(B_im.astype(f32), 1, 2)),
            twice(C_re.astype(f32)), twice(C_im.astype(f32))]
    tspec = pl.BlockSpec((SLOTS, LANES), lambda j: (j, 0))
    mspec = pl.BlockSpec((SLOTS, H, LANES), lambda j: (j, 0, 0))
    dspec = pl.BlockSpec((PAIRS_PER_J, 1, LANES), lambda j: (j, 0, 0))
    return [*tables, *mats, D.astype(f32).reshape(G, H, 1)]


IN_CT = 8
IN_MB = 1


def _s5_in_kernel(x_ref, g_ref, w_ref, u_ref, gate_ref):
    rows = BATCH * TCH
    nb = IN_CT // IN_MB
    st = [dict() for _ in range(nb)]

    def norm(i):
        parts = []
        for c in range(i * IN_MB, (i + 1) * IN_MB):
            hc = _rms(x_ref[:, c * TCH:(c + 1) * TCH, :], g_ref[...]).astype(_BF)
            parts.append(jnp.swapaxes(hc, 0, 1).reshape(rows, D_MODEL))
        st[i]["hp"] = jnp.concatenate(parts, axis=0)

    def gate_mm(i):
        gate = _dot(st[i]["hp"], w_ref[:, BRANCH:])
        gate_ref[i * IN_MB:(i + 1) * IN_MB] = (
            (gate * _sigmoid(gate)).astype(_BF).reshape(IN_MB, TCH, BATCH, BRANCH))

    def u_mm(i):
        u = _dot(st[i].pop("hp"), w_ref[:, :BRANCH])
        u_ref[i * IN_MB:(i + 1) * IN_MB] = u.astype(_BF).reshape(IN_MB, TCH, BATCH, BRANCH)

    _wavefront([norm, gate_mm, u_mm], nb)


def _s5_in(x, g, w_in):
    grid = (NCHUNK // IN_CT,)
    blk4 = (IN_CT, TCH, BATCH, BRANCH)
    return pl.pallas_call(
        _s5_in_kernel,
        grid=grid,
        in_specs=[
            pl.BlockSpec((BATCH, IN_CT * TCH, D_MODEL), lambda i: (0, i, 0)),
            pl.BlockSpec((1, D_MODEL), lambda i: (0, 0)),
            pl.BlockSpec((D_MODEL, 2 * BRANCH), lambda i: (0, 0)),
        ],
        out_specs=[
            pl.BlockSpec(blk4, lambda i: (i, 0, 0, 0)),
            pl.BlockSpec(blk4, lambda i: (i, 0, 0, 0)),
        ],
        out_shape=[
            jax.ShapeDtypeStruct((NCHUNK, TCH, BATCH, BRANCH), _BF),
            jax.ShapeDtypeStruct((NCHUNK, TCH, BATCH, BRANCH), _BF),
        ],
        compiler_params=pltpu.CompilerParams(
            dimension_semantics=("parallel",), vmem_limit_bytes=VMEM_LIMIT),
        name="s5_in",
    )(x, g, w_in)


MIX_CT = 64
MIX_R = MIX_CT * BATCH
MIX_CB = 4
MIX_RT = MIX_CB * BATCH
MIX_NH = 4
MIX_HC = MIX_CT // MIX_NH
MIX_HR = MIX_HC * BATCH


def _gather_slots(v, slot):
    rolled = [v[0]] + [pltpu.roll(v[k], k * S5_GROUP, axis=1) for k in range(1, SLOTS)]
    out = []
    for a in range(SLOTS):
        acc = rolled[0]
        for k in range(1, SLOTS):
            acc = jnp.where(slot == (a + k) % SLOTS, rolled[k], acc)
        out.append(acc)
    return out


def _scatter_slots(v, slot):
    out = []
    for k in range(SLOTS):
        acc = v[0]
        for a in range(1, SLOTS):
            acc = jnp.where(slot == (a + k) % SLOTS, v[a], acc)
        out.append(acc if k == 0 else pltpu.roll(acc, (SLOTS - k) * S5_GROUP, axis=1))
    return out


def _column_perms():
    p = np.zeros((SLOTS, GW, GW), np.float32)
    for a in range(SLOTS):
        for x in range(TCH):
            f = int(_FRAME_OF_SLOT[a, x])
            for c in range(S5_GROUP):
                p[a, f * S5_GROUP + c, x * S5_GROUP + c] = 1.0
    return p


_CPERM = _column_perms()


def _s5_mix_kernel(u_ref, lre_ref, lim_ref, ldr_ref, ldi_ref, btr_ref, bti_ref, cr_ref, ci_ref, d_ref, cperm_ref, y_ref,
                   op_ref, xs_ref, ys_ref, ss_ref, hs_ref, carry_ref, kpad_ref, mb_ref, mc_ref, dre_ref, dim_ref):
    ct = pl.program_id(1)

    @pl.when(ct == 0)
    def _():
        carry_ref[...] = jnp.zeros_like(carry_ref)
        _s5_ops_kernel(lre_ref, lim_ref, ldr_ref, ldi_ref, btr_ref, bti_ref, cr_ref, ci_ref, d_ref,
                       kpad_ref, mb_ref, mc_ref, dre_ref, dim_ref)
        for a in range(SLOTS):
            kp = kpad_ref[a]
            rows = []
            for x in range(TCH):
                off = (TCH - int(_FRAME_OF_SLOT[a, x])) * S5_GROUP
                rows.append(kp[:, off:off + GW])
            toep = jnp.concatenate(rows, axis=0).astype(_BF)
            op_ref[a, 0:GW, :] = _dot(toep, cperm_ref[a]).astype(_BF)
            op_ref[a, GW:2 * GW, :] = mc_ref[a]

    slot = lax.broadcasted_iota(jnp.int32, (MIX_RT, LANES), 1) // S5_GROUP
    dre = [dre_ref[q] for q in range(PAIRS_PER_J)]
    dim_ = [dim_ref[q] for q in range(PAIRS_PER_J)]

    def relayout_in(c0):
        r0 = c0 * BATCH
        for half in range(2):
            pieces = [u_ref[c0:c0 + MIX_CB, SLOTS * half + k, :, :].reshape(MIX_RT, LANES) for k in range(SLOTS)]
            groups = _gather_slots(pieces, slot)
            for a in range(SLOTS):
                lo = (a % 2) * GW + half * LANES
                xs_ref[a // 2, r0:r0 + MIX_RT, lo:lo + LANES] = groups[a]

    def relayout_out(c0):
        r0 = c0 * BATCH
        for half in range(2):
            groups = [ys_ref[a, r0:r0 + MIX_RT, half * LANES:(half + 1) * LANES].astype(_BF) for a in range(SLOTS)]
            frames = _scatter_slots(groups, slot)
            for k in range(SLOTS):
                y_ref[c0:c0 + MIX_CB, SLOTS * half + k, :, :] = frames[k].reshape(MIX_CB, BATCH, LANES)

    def recurrence(c0, state):
        for c in range(c0, c0 + MIX_HC):
            r0 = c * BATCH
            new = []
            for q in range(PAIRS_PER_J):
                hre, him = state[2 * q], state[2 * q + 1]
                hs_ref[q, r0:r0 + BATCH, 0:LANES] = hre.astype(_BF)
                hs_ref[q, r0:r0 + BATCH, LANES:2 * LANES] = him.astype(_BF)
                sre = ss_ref[q, r0:r0 + BATCH, 0:LANES]
                sim = ss_ref[q, r0:r0 + BATCH, LANES:2 * LANES]
                new.append(dre[q] * hre - dim_[q] * him + sre)
                new.append(dre[q] * him + dim_[q] * hre + sim)
            state = new
        return state

    blocks = [(h * MIX_HC, h * MIX_HC * BATCH) for h in range(MIX_NH)]
    for c0, r0 in blocks:
        for t in range(MIX_HC // MIX_CB):
            relayout_in(c0 + t * MIX_CB)
        for q in range(PAIRS_PER_J):
            ss_ref[q, r0:r0 + MIX_HR, :] = _dot(xs_ref[q, r0:r0 + MIX_HR, :], mb_ref[q])
    state = [carry_ref[q, k] for q in range(PAIRS_PER_J) for k in range(2)]
    for c0, r0 in blocks:
        state = recurrence(c0, state)
    for q in range(PAIRS_PER_J):
        carry_ref[q, 0] = state[2 * q]
        carry_ref[q, 1] = state[2 * q + 1]
    for c0, r0 in blocks:
        for a in range(SLOTS):
            q, m = a // 2, a % 2
            lhs = jnp.concatenate([xs_ref[q, r0:r0 + MIX_HR, m * GW:(m + 1) * GW], hs_ref[q, r0:r0 + MIX_HR, :]], axis=1)
            ys_ref[a, r0:r0 + MIX_HR, :] = _dot(lhs, op_ref[a])
    for c0, r0 in blocks:
        for t in range(MIX_HC // MIX_CB):
            relayout_out(c0 + t * MIX_CB)


def _s5_mix(u, params):
    grid = (NJ, NCHUNK // MIX_CT)
    blk = (MIX_CT, TCH, BATCH, LANES)
    tspec = pl.BlockSpec((SLOTS, LANES), lambda j, c: (j, 0))
    mspec = pl.BlockSpec((SLOTS, S5_GROUP, LANES), lambda j, c: (j, 0, 0))
    return pl.pallas_call(
        _s5_mix_kernel,
        grid=grid,
        in_specs=[
            pl.BlockSpec(blk, lambda j, c: (c, 0, 0, j)),
            *([tspec] * 4), *([mspec] * 4),
            pl.BlockSpec((SLOTS, S5_GROUP, 1), lambda j, c: (j, 0, 0)),
            pl.BlockSpec((SLOTS, GW, GW), lambda j, c: (0, 0, 0)),
        ],
        out_specs=pl.BlockSpec(blk, lambda j, c: (c, 0, 0, j)),
        out_shape=jax.ShapeDtypeStruct((NCHUNK, TCH, BATCH, BRANCH), _BF),
        scratch_shapes=[
            pltpu.VMEM((SLOTS, 2 * GW, GW), _BF),
            pltpu.VMEM((PAIRS_PER_J, MIX_R, 2 * GW), _BF),
            pltpu.VMEM((SLOTS, MIX_R, GW), _F32),
            pltpu.VMEM((PAIRS_PER_J, MIX_R, GW), _F32),
            pltpu.VMEM((PAIRS_PER_J, MIX_R, GW), _BF),
            pltpu.VMEM((PAIRS_PER_J, 2, BATCH, LANES), _F32),
            pltpu.VMEM((SLOTS, S5_GROUP, KPAD), _F32),
            pltpu.VMEM((PAIRS_PER_J, 2 * GW, GW), _BF),
            pltpu.VMEM((SLOTS, GW, GW), _BF),
            pltpu.VMEM((PAIRS_PER_J, 1, LANES), _F32),
            pltpu.VMEM((PAIRS_PER_J, 1, LANES), _F32),
        ],
        compiler_params=pltpu.CompilerParams(
            dimension_semantics=("parallel", "arbitrary"), vmem_limit_bytes=VMEM_LIMIT),
        name="s5_mix",
    )(u, *params[:9], jnp.asarray(_CPERM, _BF))


def _s5_out_kernel(y_ref, gate_ref, wglu_ref, bglu_ref, wout_ref, o_ref):
    rows = IN_MB * BATCH * TCH
    nb = IN_CT // IN_MB
    st = [dict() for _ in range(nb)]

    def act(i):
        st[i]["y"] = _gelu(y_ref[i * IN_MB:(i + 1) * IN_MB].reshape(rows, BRANCH).astype(_F32))

    def glu_mm(i):
        st[i]["glu"] = _dot(st[i]["y"].astype(_BF), wglu_ref[...])

    def half_glu(i):
        hy = 0.5 * st[i].pop("y")
        st[i]["y"] = hy + hy * jnp.tanh(0.5 * (st[i].pop("glu") + bglu_ref[...]))

    def gating(i):
        y = st[i].pop("y")
        sg = gate_ref[i * IN_MB:(i + 1) * IN_MB].reshape(rows, BRANCH).astype(_F32)
        z = (y * sg).astype(_BF).reshape(IN_MB, TCH, BATCH, BRANCH)
        parts = [jnp.swapaxes(z[k], 0, 1).reshape(BATCH * TCH, BRANCH) for k in range(IN_MB)]
        st[i]["z"] = parts[0] if IN_MB == 1 else jnp.concatenate(parts, axis=0)

    def out_mm(i):
        st[i]["out"] = _dot(st[i].pop("z"), wout_ref[...])

    def residual(i):
        out = st[i].pop("out").reshape(IN_MB, BATCH, TCH, D_MODEL)
        for k in range(IN_MB):
            c = i * IN_MB + k
            o_ref[:, c * TCH:(c + 1) * TCH, :] = out[k]

    _wavefront([act, glu_mm, half_glu, gating, out_mm, residual], nb)


def _s5_out(y, gate, w_glu, b_glu, w_out):
    grid = (NCHUNK // IN_CT,)
    blk4 = (IN_CT, TCH, BATCH, BRANCH)
    xblk = pl.BlockSpec((BATCH, IN_CT * TCH, D_MODEL), lambda i: (0, i, 0))
    return pl.pallas_call(
        _s5_out_kernel,
        grid=grid,
        in_specs=[
            pl.BlockSpec(blk4, lambda i: (i, 0, 0, 0)),
            pl.BlockSpec(blk4, lambda i: (i, 0, 0, 0)),
            pl.BlockSpec((BRANCH, BRANCH), lambda i: (0, 0)),
            pl.BlockSpec((1, BRANCH), lambda i: (0, 0)),
            pl.BlockSpec((BRANCH, D_MODEL), lambda i: (0, 0)),
        ],
        out_specs=xblk,
        out_shape=jax.ShapeDtypeStruct((BATCH, SEQ, D_MODEL), _F32),
        compiler_params=pltpu.CompilerParams(
            dimension_semantics=("parallel",), vmem_limit_bytes=VMEM_LIMIT),
        name="s5_out",
    )(y, gate, w_glu, b_glu, w_out)


SGU_TB = 1024
SGU_RB = 512
SGU_BPB = SGU_RB // SGU_BLOCK


def _sgu_kernel(x_ref, d_ref, g_ref, win_ref, lng_ref, lnb_ref, ws_ref, bs_ref, wout_ref, fg_ref, o_ref, mix_ref):
    blocks = [slice(r0, r0 + SGU_RB) for r0 in range(0, SGU_TB, SGU_RB)]
    st = [dict() for _ in blocks]

    def norm_in(i):
        st[i]["x1"] = x_ref[blocks[i], :] + d_ref[blocks[i], :]
        st[i]["h"] = _rms(st[i]["x1"], g_ref[...]).astype(_BF)

    def proj_v(i):
        st[i]["v"] = _dot(st[i]["h"], win_ref[:, BRANCH:2 * BRANCH])

    def proj_ug(i):
        st[i]["u"] = _dot(st[i]["h"], win_ref[:, :BRANCH])
        st[i]["gate"] = _dot(st[i]["h"], win_ref[:, 2 * BRANCH:])

    def layer_norm(i):
        v = _gelu(st[i].pop("v"))
        mu = jnp.mean(v, axis=-1, keepdims=True)
        vc = v - mu
        var = jnp.mean(vc * vc, axis=-1, keepdims=True)
        st[i]["vn"] = (vc * lax.rsqrt(var + LN_EPS) * lng_ref[...] + lnb_ref[...]).astype(_BF)

    def spatial(i):
        vn = st[i].pop("vn")
        r0 = blocks[i].start
        for hd in range(SGU_HEADS):
            lo = hd * SGU_HEAD_DIM
            rhs = jnp.concatenate(
                [vn[b * SGU_BLOCK:(b + 1) * SGU_BLOCK, lo:lo + SGU_HEAD_DIM] for b in range(SGU_BPB)], axis=1)
            res = _dot(ws_ref[hd], rhs)
            for b in range(SGU_BPB):
                mix_ref[r0 + b * SGU_BLOCK:r0 + (b + 1) * SGU_BLOCK, lo:lo + SGU_HEAD_DIM] = (
                    res[:, b * SGU_HEAD_DIM:(b + 1) * SGU_HEAD_DIM] + bs_ref[:, lo:lo + SGU_HEAD_DIM])

    def activate(i):
        st[i]["ug"] = _gelu(st[i].pop("u")) * _silu_tanh(st[i].pop("gate"))

    def gating(i):
        st[i]["z"] = (st[i].pop("ug") * mix_ref[blocks[i], :]).astype(_BF)

    def proj_out(i):
        st[i]["out"] = _dot(st[i].pop("z"), wout_ref[...])

    def norm_out(i):
        x2 = st[i].pop("x1") + st[i].pop("out")
        o_ref[blocks[i], :] = _rms(x2, fg_ref[...])

    _wavefront([norm_in, proj_v, proj_ug, layer_norm, activate, spatial, gating, proj_out, norm_out], len(blocks))


def _sgu_layer(x, delta, g, w_in, ln_g, ln_b, ws, bs, w_out, final_g):
    n = BATCH * SEQ
    grid = (n // SGU_TB,)
    row = lambda i: (0, 0)
    xblk = pl.BlockSpec((SGU_TB, D_MODEL), lambda i: (i, 0))
    return pl.pallas_call(
        _sgu_kernel,
        grid=grid,
        in_specs=[
            xblk,
            xblk,
            pl.BlockSpec((1, D_MODEL), row),
            pl.BlockSpec((D_MODEL, 3 * BRANCH), row),
            pl.BlockSpec((1, BRANCH), row),
            pl.BlockSpec((1, BRANCH), row),
            pl.BlockSpec((SGU_HEADS, SGU_BLOCK, SGU_BLOCK), lambda i: (0, 0, 0)),
            pl.BlockSpec((SGU_BLOCK, BRANCH), row),
            pl.BlockSpec((BRANCH, D_MODEL), row),
            pl.BlockSpec((1, D_MODEL), row),
        ],
        out_specs=xblk,
        out_shape=jax.ShapeDtypeStruct((n, D_MODEL), _F32),
        scratch_shapes=[pltpu.VMEM((SGU_TB, BRANCH), _F32)],
        compiler_params=pltpu.CompilerParams(
            dimension_semantics=("parallel",), vmem_limit_bytes=VMEM_LIMIT),
        name="sgu_layer",
    )(x, delta, g, w_in, ln_g, ln_b, ws, bs, w_out, final_g)


def kernel(x, norm_g, final_g, s5_w_in, s5_A_re, s5_A_im, s5_log_dt, s5_B_re, s5_B_im, s5_C_re, s5_C_im, s5_D, s5_w_glu, s5_b_glu, s5_w_out, sgu_w_in, sgu_ln_g, sgu_ln_b, sgu_w_s, sgu_b_s, sgu_w_out):
    params = _s5_operators(
        s5_A_re[0], s5_A_im[0], s5_log_dt[0], s5_B_re[0], s5_B_im[0], s5_C_re[0], s5_C_im[0], s5_D[0])
    u, gate = _s5_in(x, norm_g[0][None, :], s5_w_in[0].astype(_BF))
    y = _s5_mix(u, params)
    delta = _s5_out(y, gate, s5_w_glu[0].astype(_BF), s5_b_glu[0][None, :], s5_w_out[0].astype(_BF))

    mask = jnp.tril(jnp.ones((SGU_BLOCK, SGU_BLOCK), dtype=bool))
    ws = jnp.where(mask[None], sgu_w_s[0], 0.0).astype(_BF)
    bs = jnp.repeat(jnp.transpose(sgu_b_s[0]), SGU_HEAD_DIM, axis=1)
    out = _sgu_layer(x.reshape(BATCH * SEQ, D_MODEL), delta.reshape(BATCH * SEQ, D_MODEL),
                     norm_g[1][None, :], sgu_w_in[0].astype(_BF),
                     sgu_ln_g[0][None, :], sgu_ln_b[0][None, :], ws, bs,
                     sgu_w_out[0].astype(_BF), final_g[None, :])
    return out.reshape(BATCH, SEQ, D_MODEL)
```

```python
import math

import numpy as np
import jax
import jax.numpy as jnp
from jax import lax
from jax.experimental import pallas as pl
from jax.experimental.pallas import tpu as pltpu

D_MODEL = 1024
BATCH = 16
SEQ = 2048
BRANCH = D_MODEL
S5_GROUP = 16
S5_GROUPS = BRANCH // S5_GROUP
S5_STATE = 64
SGU_BLOCK = 128
SGU_HEADS = 8
SGU_HEAD_DIM = BRANCH // SGU_HEADS
RMS_EPS = 1e-6
LN_EPS = 1e-5
RE_CLIP = -1e-4

LANES = 128
TCH = 16
NCHUNK = SEQ // TCH
SLOTS = LANES // S5_GROUP
NJ = BRANCH // LANES
GW = TCH * S5_GROUP
PAIRS_PER_J = SLOTS // 2
VMEM_LIMIT = 56 * 1024 * 1024

_HI = lax.Precision.HIGHEST
_BF = jnp.bfloat16
_F32 = jnp.float32


_GELU_C1 = math.sqrt(2.0 / math.pi)
_GELU_C2 = _GELU_C1 * 0.044715


def _gelu(x):
    inner = x * (_GELU_C1 + _GELU_C2 * (x * x))
    return (0.5 * x) * (1.0 + jnp.tanh(inner))


def _sigmoid(x):
    return jax.nn.sigmoid(x)


def _silu_tanh(x):
    h = 0.5 * x
    return h + h * jnp.tanh(h)


def _dot(a, b):
    return jnp.dot(a, b, preferred_element_type=_F32)


def _wavefront(stages, n):
    for t in range(len(stages) + n - 1):
        for i in range(n):
            k = t - i
            if 0 <= k < len(stages):
                stages[k](i)


def _rms(x, g):
    return x * lax.rsqrt(jnp.mean(x * x, axis=-1, keepdims=True) + RMS_EPS) * g


KPAD = 2 * GW


def _slot_frame_table():
    a = np.arange(SLOTS)[:, None]
    x = np.arange(TCH)[None, :]
    half, k = x // SLOTS, x % SLOTS
    return SLOTS * half + (k - a) % SLOTS


_FRAME_OF_SLOT = _slot_frame_table()


def _s5_ops_kernel(lre_ref, lim_ref, ldr_ref, ldi_ref, btr_ref, bti_ref, cr_ref, ci_ref, d_ref,
                   kpad_ref, mb_ref, mc_ref, dre_ref, dim_ref):
    P = S5_STATE
    lane = lax.broadcasted_iota(jnp.int32, (S5_GROUP, LANES), 1)
    kcol = lax.broadcasted_iota(jnp.int32, (TCH, 1), 0)
    lane256 = lax.broadcasted_iota(jnp.int32, (S5_GROUP, GW), 1)
    row256 = lax.broadcasted_iota(jnp.int32, (S5_GROUP, GW), 0)
    lane_gw = lax.broadcasted_iota(jnp.int32, (GW, LANES), 1)
    nt = (((1,), (1,)), ((), ()))
    zeros64 = jnp.zeros((P, GW), _F32)
    st = [dict() for _ in range(SLOTS)]

    def outer(t_re, t_im, m_re, m_im):
        re = [t_re[k:k + 1] * m_re - t_im[k:k + 1] * m_im for k in range(TCH)]
        im = [t_re[k:k + 1] * m_im + t_im[k:k + 1] * m_re for k in range(TCH)]
        return jnp.concatenate(re, axis=0), jnp.concatenate(im, axis=0)

    def tables(a):
        lam_re, lam_im = lre_ref[a:a + 1, :], lim_ref[a:a + 1, :]
        ldr, ldi = ldr_ref[a:a + 1, :], ldi_ref[a:a + 1, :]

        def power(k):
            mag = jnp.exp(k * ldr)
            return mag * jnp.cos(k * ldi), mag * jnp.sin(k * ldi)

        frame = (kcol & SLOTS) + (((kcol & (SLOTS - 1)) + (SLOTS - a)) & (SLOTS - 1))
        pw_re, pw_im = power(kcol.astype(_F32))
        pi_re, pi_im = power((TCH - 1 - frame).astype(_F32))
        po_re, po_im = power((frame + 1).astype(_F32))
        st[a]["p16"] = power(jnp.full((1, 1), float(TCH), _F32))

        nr, ni = pw_re[1:2] - 1.0, pw_im[1:2]
        den = lam_re * lam_re + lam_im * lam_im
        cf_re = (nr * lam_re + ni * lam_im) / den
        cf_im = (ni * lam_re - nr * lam_im) / den
        b_re, b_im = btr_ref[a], bti_ref[a]
        bb_re = cf_re * b_re - cf_im * b_im
        bb_im = cf_re * b_im + cf_im * b_re
        c_re, c_im = cr_ref[a], ci_ref[a]
        st[a]["bb"] = (bb_re, bb_im)
        st[a]["lc"] = outer(pw_re, pw_im, c_re, c_im)
        st[a]["in"] = outer(pi_re, pi_im, bb_re, bb_im)
        st[a]["ot"] = outer(po_re, po_im, c_re, c_im)

    def taps(a):
        bb_re, bb_im = st[a].pop("bb")
        lc_re, lc_im = st[a].pop("lc")
        once = lane < P
        k = (lax.dot_general(jnp.where(once, bb_re, 0.0), lc_re, nt, precision=_HI, preferred_element_type=_F32)
             - lax.dot_general(jnp.where(once, bb_im, 0.0), lc_im, nt, precision=_HI, preferred_element_type=_F32))
        kpad_ref[a, :, 0:GW] = jnp.zeros((S5_GROUP, GW), _F32)
        kpad_ref[a, :, GW:2 * GW] = k + jnp.where(lane256 == row256, d_ref[a], 0.0)

    def state_in(a):
        q, m = a // 2, a % 2
        mine = (lane_gw >= P) if m else (lane_gw < P)
        in_re, in_im = st[a].pop("in")
        mb_ref[q, m * GW:(m + 1) * GW, 0:LANES] = jnp.where(mine, in_re, 0.0).astype(_BF)
        mb_ref[q, m * GW:(m + 1) * GW, LANES:2 * LANES] = jnp.where(mine, in_im, 0.0).astype(_BF)

    def state_out(a):
        q, m = a // 2, a % 2
        ot_re, ot_im = st[a].pop("ot")
        o_re, o_im = ot_re.T[0:P], ot_im.T[0:P]
        blocks = [zeros64, zeros64, zeros64, zeros64]
        blocks[m], blocks[2 + m] = o_re, -o_im
        mc_ref[a] = jnp.concatenate(blocks, axis=0).astype(_BF)
        if m:
            first = lax.broadcasted_iota(jnp.int32, (1, LANES), 1) < P
            dre_ref[q] = jnp.where(first, st[a - 1]["p16"][0], st[a]["p16"][0])
            dim_ref[q] = jnp.where(first, st[a - 1]["p16"][1], st[a]["p16"][1])

    _wavefront([tables, taps, state_in, state_out], SLOTS)


def _s5_operators(A_re, A_im, log_dt, B_re, B_im, C_re, C_im, D):
    f32 = jnp.float32
    G, P, H = S5_GROUPS, S5_STATE, S5_GROUP
    lam_re = jnp.minimum(A_re.astype(f32), RE_CLIP)
    lam_im = A_im.astype(f32)
    dt = jnp.exp(log_dt.astype(f32))[:, None]

    twice = lambda t: jnp.concatenate([t, t], axis=-1)
    tables = [twice(lam_re), twice(lam_im), twice(lam_re * dt), twice(lam_im * dt)]
    mats = [twice(jnp.swapaxes(B_re.astype(f32), 1, 2)), twice(jnp.swapaxes(B_im.astype(f32), 1, 2)),
            twice(C_re.astype(f32)), twice(C_im.astype(f32))]
    return jnp.concatenate(tables, axis=-1), jnp.concatenate(mats, axis=-1), D.astype(f32).reshape(G, H, 1)


IN_CT = 8
IN_MB = 1


def _s5_in_kernel(x_ref, g_ref, w_ref, u_ref, gate_ref):
    rows = BATCH * TCH
    nb = IN_CT // IN_MB
    st = [dict() for _ in range(nb)]

    def norm(i):
        parts = []
        for c in range(i * IN_MB, (i + 1) * IN_MB):
            hc = _rms(x_ref[:, c * TCH:(c + 1) * TCH, :], g_ref[...]).astype(_BF)
            parts.append(jnp.swapaxes(hc, 0, 1).reshape(rows, D_MODEL))
        st[i]["hp"] = jnp.concatenate(parts, axis=0)

    def gate_mm(i):
        gate = _dot(st[i]["hp"], w_ref[:, BRANCH:])
        gate_ref[i * IN_MB:(i + 1) * IN_MB] = (
            (gate * _sigmoid(gate)).astype(_BF).reshape(IN_MB, TCH, BATCH, BRANCH))

    def u_mm(i):
        u = _dot(st[i].pop("hp"), w_ref[:, :BRANCH])
        u_ref[i * IN_MB:(i + 1) * IN_MB] = u.astype(_BF).reshape(IN_MB, TCH, BATCH, BRANCH)

    _wavefront([norm, gate_mm, u_mm], nb)


def _s5_in(x, g, w_in):
    grid = (NCHUNK // IN_CT,)
    blk4 = (IN_CT, TCH, BATCH, BRANCH)
    return pl.pallas_call(
        _s5_in_kernel,
        grid=grid,
        in_specs=[
            pl.BlockSpec((BATCH, IN_CT * TCH, D_MODEL), lambda i: (0, i, 0)),
            pl.BlockSpec((1, D_MODEL), lambda i: (0, 0)),
            pl.BlockSpec((D_MODEL, 2 * BRANCH), lambda i: (0, 0)),
        ],
        out_specs=[
            pl.BlockSpec(blk4, lambda i: (i, 0, 0, 0)),
            pl.BlockSpec(blk4, lambda i: (i, 0, 0, 0)),
        ],
        out_shape=[
            jax.ShapeDtypeStruct((NCHUNK, TCH, BATCH, BRANCH), _BF),
            jax.ShapeDtypeStruct((NCHUNK, TCH, BATCH, BRANCH), _BF),
        ],
        compiler_params=pltpu.CompilerParams(
            dimension_semantics=("parallel",), vmem_limit_bytes=VMEM_LIMIT),
        name="s5_in",
    )(x, g, w_in)


MIX_CT = 64
MIX_R = MIX_CT * BATCH
MIX_CB = 4
MIX_RT = MIX_CB * BATCH
MIX_NH = 4
MIX_HC = MIX_CT // MIX_NH
MIX_HR = MIX_HC * BATCH


def _gather_slots(v, slot):
    rolled = [v[0]] + [pltpu.roll(v[k], k * S5_GROUP, axis=1) for k in range(1, SLOTS)]
    out = []
    for a in range(SLOTS):
        acc = rolled[0]
        for k in range(1, SLOTS):
            acc = jnp.where(slot == (a + k) % SLOTS, rolled[k], acc)
        out.append(acc)
    return out


def _scatter_slots(v, slot):
    out = []
    for k in range(SLOTS):
        acc = v[0]
        for a in range(1, SLOTS):
            acc = jnp.where(slot == (a + k) % SLOTS, v[a], acc)
        out.append(acc if k == 0 else pltpu.roll(acc, (SLOTS - k) * S5_GROUP, axis=1))
    return out


def _column_perms():
    p = np.zeros((SLOTS, GW, GW), np.float32)
    for a in range(SLOTS):
        for x in range(TCH):
            f = int(_FRAME_OF_SLOT[a, x])
            for c in range(S5_GROUP):
                p[a, f * S5_GROUP + c, x * S5_GROUP + c] = 1.0
    return p


_CPERM = _column_perms()


def _s5_mix_kernel(u_ref, tab_ref, mat_ref, d_ref, cperm_ref, y_ref,
                   op_ref, xs_ref, ys_ref, ss_ref, hs_ref, carry_ref, kpad_ref, mb_ref, mc_ref, dre_ref, dim_ref):
    ct = pl.program_id(1)

    @pl.when(ct == 0)
    def _():
        carry_ref[...] = jnp.zeros_like(carry_ref)
        tabs = [tab_ref.at[:, t * LANES:(t + 1) * LANES] for t in range(4)]
        mats = [mat_ref.at[:, :, t * LANES:(t + 1) * LANES] for t in range(4)]
        _s5_ops_kernel(*tabs, *mats, d_ref, kpad_ref, mb_ref, mc_ref, dre_ref, dim_ref)
        for a in range(SLOTS):
            kp = kpad_ref[a]
            rows = []
            for x in range(TCH):
                off = (TCH - int(_FRAME_OF_SLOT[a, x])) * S5_GROUP
                rows.append(kp[:, off:off + GW])
            toep = jnp.concatenate(rows, axis=0).astype(_BF)
            op_ref[a, 0:GW, :] = _dot(toep, cperm_ref[a]).astype(_BF)
            op_ref[a, GW:2 * GW, :] = mc_ref[a]

    slot = lax.broadcasted_iota(jnp.int32, (MIX_RT, LANES), 1) // S5_GROUP
    dre = [dre_ref[q] for q in range(PAIRS_PER_J)]
    dim_ = [dim_ref[q] for q in range(PAIRS_PER_J)]

    def relayout_in(c0):
        r0 = c0 * BATCH
        for half in range(2):
            pieces = [u_ref[c0:c0 + MIX_CB, SLOTS * half + k, :, :].reshape(MIX_RT, LANES) for k in range(SLOTS)]
            groups = _gather_slots(pieces, slot)
            for a in range(SLOTS):
                lo = (a % 2) * GW + half * LANES
                xs_ref[a // 2, r0:r0 + MIX_RT, lo:lo + LANES] = groups[a]

    def relayout_out(c0):
        r0 = c0 * BATCH
        for half in range(2):
            groups = [ys_ref[a, r0:r0 + MIX_RT, half * LANES:(half + 1) * LANES].astype(_BF) for a in range(SLOTS)]
            frames = _scatter_slots(groups, slot)
            for k in range(SLOTS):
                y_ref[c0:c0 + MIX_CB, SLOTS * half + k, :, :] = frames[k].reshape(MIX_CB, BATCH, LANES)

    def recurrence(c0, state):
        for c in range(c0, c0 + MIX_HC):
            r0 = c * BATCH
            new = []
            for q in range(PAIRS_PER_J):
                hre, him = state[2 * q], state[2 * q + 1]
                hs_ref[q, r0:r0 + BATCH, 0:LANES] = hre.astype(_BF)
                hs_ref[q, r0:r0 + BATCH, LANES:2 * LANES] = him.astype(_BF)
                sre = ss_ref[q, r0:r0 + BATCH, 0:LANES]
                sim = ss_ref[q, r0:r0 + BATCH, LANES:2 * LANES]
                new.append(dre[q] * hre - dim_[q] * him + sre)
                new.append(dre[q] * him + dim_[q] * hre + sim)
            state = new
        return state

    blocks = [(h * MIX_HC, h * MIX_HC * BATCH) for h in range(MIX_NH)]
    for c0, r0 in blocks:
        for t in range(MIX_HC // MIX_CB):
            relayout_in(c0 + t * MIX_CB)
        for q in range(PAIRS_PER_J):
            ss_ref[q, r0:r0 + MIX_HR, :] = _dot(xs_ref[q, r0:r0 + MIX_HR, :], mb_ref[q])
    state = [carry_ref[q, k] for q in range(PAIRS_PER_J) for k in range(2)]
    for c0, r0 in blocks:
        state = recurrence(c0, state)
    for q in range(PAIRS_PER_J):
        carry_ref[q, 0] = state[2 * q]
        carry_ref[q, 1] = state[2 * q + 1]
    for c0, r0 in blocks:
        for a in range(SLOTS):
            q, m = a // 2, a % 2
            lhs = jnp.concatenate([xs_ref[q, r0:r0 + MIX_HR, m * GW:(m + 1) * GW], hs_ref[q, r0:r0 + MIX_HR, :]], axis=1)
            ys_ref[a, r0:r0 + MIX_HR, :] = _dot(lhs, op_ref[a])
    for c0, r0 in blocks:
        for t in range(MIX_HC // MIX_CB):
            relayout_out(c0 + t * MIX_CB)


def _s5_mix(u, params):
    grid = (NJ, NCHUNK // MIX_CT)
    blk = (MIX_CT, TCH, BATCH, LANES)
    return pl.pallas_call(
        _s5_mix_kernel,
        grid=grid,
        in_specs=[
            pl.BlockSpec(blk, lambda j, c: (c, 0, 0, j)),
            pl.BlockSpec((SLOTS, 4 * LANES), lambda j, c: (j, 0)),
            pl.BlockSpec((SLOTS, S5_GROUP, 4 * LANES), lambda j, c: (j, 0, 0)),
            pl.BlockSpec((SLOTS, S5_GROUP, 1), lambda j, c: (j, 0, 0)),
            pl.BlockSpec((SLOTS, GW, GW), lambda j, c: (0, 0, 0)),
        ],
        out_specs=pl.BlockSpec(blk, lambda j, c: (c, 0, 0, j)),
        out_shape=jax.ShapeDtypeStruct((NCHUNK, TCH, BATCH, BRANCH), _BF),
        scratch_shapes=[
            pltpu.VMEM((SLOTS, 2 * GW, GW), _BF),
            pltpu.VMEM((PAIRS_PER_J, MIX_R, 2 * GW), _BF),
            pltpu.VMEM((SLOTS, MIX_R, GW), _F32),
            pltpu.VMEM((PAIRS_PER_J, MIX_R, GW), _F32),
            pltpu.VMEM((PAIRS_PER_J, MIX_R, GW), _BF),
            pltpu.VMEM((PAIRS_PER_J, 2, BATCH, LANES), _F32),
            pltpu.VMEM((SLOTS, S5_GROUP, KPAD), _F32),
            pltpu.VMEM((PAIRS_PER_J, 2 * GW, GW), _BF),
            pltpu.VMEM((SLOTS, GW, GW), _BF),
            pltpu.VMEM((PAIRS_PER_J, 1, LANES), _F32),
            pltpu.VMEM((PAIRS_PER_J, 1, LANES), _F32),
        ],
        compiler_params=pltpu.CompilerParams(
            dimension_semantics=("parallel", "arbitrary"), vmem_limit_bytes=VMEM_LIMIT),
        name="s5_mix",
    )(u, *params, jnp.asarray(_CPERM, _BF))


def _s5_out_kernel(y_ref, gate_ref, wglu_ref, bglu_ref, wout_ref, o_ref):
    rows = IN_MB * BATCH * TCH
    nb = IN_CT // IN_MB
    st = [dict() for _ in range(nb)]

    def act(i):
        st[i]["y"] = _gelu(y_ref[i * IN_MB:(i + 1) * IN_MB].reshape(rows, BRANCH).astype(_F32))

    def glu_mm(i):
        st[i]["glu"] = _dot(st[i]["y"].astype(_BF), wglu_ref[...])

    def half_glu(i):
        hy = 0.5 * st[i].pop("y")
        st[i]["y"] = hy + hy * jnp.tanh(0.5 * (st[i].pop("glu") + bglu_ref[...]))

    def gating(i):
        y = st[i].pop("y")
        sg = gate_ref[i * IN_MB:(i + 1) * IN_MB].reshape(rows, BRANCH).astype(_F32)
        z = (y * sg).astype(_BF).reshape(IN_MB, TCH, BATCH, BRANCH)
        parts = [jnp.swapaxes(z[k], 0, 1).reshape(BATCH * TCH, BRANCH) for k in range(IN_MB)]
        st[i]["z"] = parts[0] if IN_MB == 1 else jnp.concatenate(parts, axis=0)

    def out_mm(i):
        st[i]["out"] = _dot(st[i].pop("z"), wout_ref[...])

    def residual(i):
        out = st[i].pop("out").reshape(IN_MB, BATCH, TCH, D_MODEL)
        for k in range(IN_MB):
            c = i * IN_MB + k
            o_ref[:, c * TCH:(c + 1) * TCH, :] = out[k]

    _wavefront([act, glu_mm, half_glu, gating, out_mm, residual], nb)


def _s5_out(y, gate, w_glu, b_glu, w_out):
    grid = (NCHUNK // IN_CT,)
    blk4 = (IN_CT, TCH, BATCH, BRANCH)
    xblk = pl.BlockSpec((BATCH, IN_CT * TCH, D_MODEL), lambda i: (0, i, 0))
    return pl.pallas_call(
        _s5_out_kernel,
        grid=grid,
        in_specs=[
            pl.BlockSpec(blk4, lambda i: (i, 0, 0, 0)),
            pl.BlockSpec(blk4, lambda i: (i, 0, 0, 0)),
            pl.BlockSpec((BRANCH, BRANCH), lambda i: (0, 0)),
            pl.BlockSpec((1, BRANCH), lambda i: (0, 0)),
            pl.BlockSpec((BRANCH, D_MODEL), lambda i: (0, 0)),
        ],
        out_specs=xblk,
        out_shape=jax.ShapeDtypeStruct((BATCH, SEQ, D_MODEL), _F32),
        compiler_params=pltpu.CompilerParams(
            dimension_semantics=("parallel",), vmem_limit_bytes=VMEM_LIMIT),
        name="s5_out",
    )(y, gate, w_glu, b_glu, w_out)


SGU_TB = 1024
SGU_RB = 512
SGU_BPB = SGU_RB // SGU_BLOCK


def _sgu_kernel(x_ref, d_ref, g_ref, win_ref, lng_ref, lnb_ref, ws_ref, bs_ref, wout_ref, fg_ref, o_ref, mix_ref):
    blocks = [slice(r0, r0 + SGU_RB) for r0 in range(0, SGU_TB, SGU_RB)]
    st = [dict() for _ in blocks]

    def norm_in(i):
        st[i]["x1"] = x_ref[blocks[i], :] + d_ref[blocks[i], :]
        st[i]["h"] = _rms(st[i]["x1"], g_ref[...]).astype(_BF)

    def proj_v(i):
        st[i]["v"] = _dot(st[i]["h"], win_ref[:, BRANCH:2 * BRANCH])

    def proj_ug(i):
        st[i]["u"] = _dot(st[i]["h"], win_ref[:, :BRANCH])
        st[i]["gate"] = _dot(st[i]["h"], win_ref[:, 2 * BRANCH:])

    def layer_norm(i):
        v = _gelu(st[i].pop("v"))
        mu = jnp.mean(v, axis=-1, keepdims=True)
        vc = v - mu
        var = jnp.mean(vc * vc, axis=-1, keepdims=True)
        st[i]["vn"] = (vc * lax.rsqrt(var + LN_EPS) * lng_ref[...] + lnb_ref[...]).astype(_BF)

    def spatial(i):
        vn = st[i].pop("vn")
        r0 = blocks[i].start
        for hd in range(SGU_HEADS):
            lo = hd * SGU_HEAD_DIM
            rhs = jnp.concatenate(
                [vn[b * SGU_BLOCK:(b + 1) * SGU_BLOCK, lo:lo + SGU_HEAD_DIM] for b in range(SGU_BPB)], axis=1)
            res = _dot(ws_ref[hd], rhs)
            for b in range(SGU_BPB):
                mix_ref[r0 + b * SGU_BLOCK:r0 + (b + 1) * SGU_BLOCK, lo:lo + SGU_HEAD_DIM] = (
                    res[:, b * SGU_HEAD_DIM:(b + 1) * SGU_HEAD_DIM] + bs_ref[:, lo:lo + SGU_HEAD_DIM])

    def activate(i):
        st[i]["ug"] = _gelu(st[i].pop("u")) * _silu_tanh(st[i].pop("gate"))

    def gating(i):
        st[i]["z"] = (st[i].pop("ug") * mix_ref[blocks[i], :]).astype(_BF)

    def proj_out(i):
        st[i]["out"] = _dot(st[i].pop("z"), wout_ref[...])

    def norm_out(i):
        x2 = st[i].pop("x1") + st[i].pop("out")
        o_ref[blocks[i], :] = _rms(x2, fg_ref[...])

    _wavefront([norm_in, proj_v, proj_ug, layer_norm, activate, spatial, gating, proj_out, norm_out], len(blocks))


def _sgu_layer(x, delta, g, w_in, ln_g, ln_b, ws, bs, w_out, final_g):
    n = BATCH * SEQ
    grid = (n // SGU_TB,)
    row = lambda i: (0, 0)
    xblk = pl.BlockSpec((SGU_TB, D_MODEL), lambda i: (i, 0))
    return pl.pallas_call(
        _sgu_kernel,
        grid=grid,
        in_specs=[
            xblk,
            xblk,
            pl.BlockSpec((1, D_MODEL), row),
            pl.BlockSpec((D_MODEL, 3 * BRANCH), row),
            pl.BlockSpec((1, BRANCH), row),
            pl.BlockSpec((1, BRANCH), row),
            pl.BlockSpec((SGU_HEADS, SGU_BLOCK, SGU_BLOCK), lambda i: (0, 0, 0)),
            pl.BlockSpec((SGU_BLOCK, BRANCH), row),
            pl.BlockSpec((BRANCH, D_MODEL), row),
            pl.BlockSpec((1, D_MODEL), row),
        ],
        out_specs=xblk,
        out_shape=jax.ShapeDtypeStruct((n, D_MODEL), _F32),
        scratch_shapes=[pltpu.VMEM((SGU_TB, BRANCH), _F32)],
        compiler_params=pltpu.CompilerParams(
            dimension_semantics=("parallel",), vmem_limit_bytes=VMEM_LIMIT),
        name="sgu_layer",
    )(x, delta, g, w_in, ln_g, ln_b, ws, bs, w_out, final_g)


def kernel(x, norm_g, final_g, s5_w_in, s5_A_re, s5_A_im, s5_log_dt, s5_B_re, s5_B_im, s5_C_re, s5_C_im, s5_D, s5_w_glu, s5_b_glu, s5_w_out, sgu_w_in, sgu_ln_g, sgu_ln_b, sgu_w_s, sgu_b_s, sgu_w_out):
    params = _s5_operators(
        s5_A_re[0], s5_A_im[0], s5_log_dt[0], s5_B_re[0], s5_B_im[0], s5_C_re[0], s5_C_im[0], s5_D[0])
    u, gate = _s5_in(x, norm_g[0][None, :], s5_w_in[0].astype(_BF))
    y = _s5_mix(u, params)
    delta = _s5_out(y, gate, s5_w_glu[0].astype(_BF), s5_b_glu[0][None, :], s5_w_out[0].astype(_BF))

    mask = jnp.tril(jnp.ones((SGU_BLOCK, SGU_BLOCK), dtype=bool))
    ws = jnp.where(mask[None], sgu_w_s[0], 0.0).astype(_BF)
    bs = jnp.repeat(jnp.transpose(sgu_b_s[0]), SGU_HEAD_DIM, axis=1)
    out = _sgu_layer(x.reshape(BATCH * SEQ, D_MODEL), delta.reshape(BATCH * SEQ, D_MODEL),
                     norm_g[1][None, :], sgu_w_in[0].astype(_BF),
                     sgu_ln_g[0][None, :], sgu_ln_b[0][None, :], ws, bs,
                     sgu_w_out[0].astype(_BF), final_g[None, :])
    return out.reshape(BATCH, SEQ, D_MODEL)
```
